```python
import math
import jax, jax.numpy as jnp
from jax import lax
import numpy as np

D_MODEL = 2048
BATCH = 4
SEQ = 2048
DEPTH = 4
DEC_BATCH = 128
DEC_SEQ = 1
PAST_LEN = 16384
PAGE_SIZE = 128

N_BRANCH = 4
BRANCH_W = D_MODEL // 4
CHUNK = 128
SGU_GROUPS = 4
SGU_GW = BRANCH_W // SGU_GROUPS
CONV_K = 31
SSM_H = 16
SSM_G = BRANCH_W // SSM_H
SSM_P = 64
POOL_WINDOWS = (2, 4, 8, 16)
POOL_GW = BRANCH_W // len(POOL_WINDOWS)
POOL_HIST = max(POOL_WINDOWS) - 1
OFF_A = 0
OFF_B = 2 * BRANCH_W
OFF_C = 4 * BRANCH_W
OFF_D = 5 * BRANCH_W
OFF_GATE = 6 * BRANCH_W
N_IN = OFF_GATE + N_BRANCH * D_MODEL
N_GROUPS = 4
EXP_PER_GROUP = 4
N_EXPERTS = N_GROUPS * EXP_PER_GROUP
TOP_K = 2
D_EXPERT = D_MODEL // 4
EPS = 1e-6

kernel_name = 'hybrid_sgu_conv_s5_pool_hmoe_step'


def _rmsnorm(x, g):
    x32 = x.astype(jnp.float32)
    y = x32 * lax.rsqrt(jnp.mean(x32 * x32, axis=-1, keepdims=True) + EPS)
    return (y * g.astype(jnp.float32)).astype(x.dtype)


def _layernorm(x, g, b):
    x32 = x.astype(jnp.float32)
    xc = x32 - jnp.mean(x32, axis=-1, keepdims=True)
    var = jnp.mean(xc * xc, axis=-1, keepdims=True)
    return (xc * lax.rsqrt(var + EPS) * g.astype(jnp.float32) + b.astype(jnp.float32)).astype(x.dtype)


def _chunk_sgu(u, v, w_s, b_s):
    bsz, L, C = v.shape
    n_chunks = -(-L // CHUNK)
    pad = n_chunks * CHUNK - L
    vp = jnp.pad(v, ((0, 0), (0, pad), (0, 0))).reshape(bsz, n_chunks, CHUNK, SGU_GROUPS, SGU_GW)
    causal = jnp.tril(jnp.ones((CHUNK, CHUNK), dtype=bool))
    w = jnp.where(causal, w_s, jnp.zeros_like(w_s))
    mixed = jnp.einsum('gts,bnsgc->bntgc', w, vp) + jnp.swapaxes(b_s, 0, 1)[None, None, :, :, None]
    mixed = mixed.reshape(bsz, n_chunks * CHUNK, C)[:, :L]
    return u * mixed


def _causal_dwconv(g, buf, w, b):
    xc = jnp.concatenate([buf.astype(g.dtype), g], axis=1)
    out = lax.conv_general_dilated(xc, w[:, None, :].astype(g.dtype), window_strides=(1,), padding='VALID',
                                   dimension_numbers=('NWC', 'WIO', 'NWC'), feature_group_count=g.shape[-1])
    return out + b.astype(g.dtype), xc[:, -(CONV_K - 1):]


def _ssm_combine(e1, e2):
    a1r, a1i, b1r, b1i = e1
    a2r, a2i, b2r, b2i = e2
    return (a2r * a1r - a2i * a1i,
            a2r * a1i + a2i * a1r,
            a2r * b1r - a2i * b1i + b2r,
            a2r * b1i + a2i * b1r + b2i)


def _s5(u, h0_re, h0_im, p):
    f32 = jnp.float32
    bsz, L, _ = u.shape
    u32 = u.astype(f32)
    ug = u32.reshape(bsz, L, SSM_G, SSM_H)
    a_re = p['ssm_a_re'].astype(f32)
    a_im = p['ssm_a_im'].astype(f32)
    dt = jnp.exp(p['ssm_log_dt'].astype(f32))[:, None]
    mag = jnp.exp(a_re * dt)
    lb_re = mag * jnp.cos(a_im * dt)
    lb_im = mag * jnp.sin(a_im * dt)
    den = a_re * a_re + a_im * a_im
    nr = lb_re - 1.0
    k_re = (nr * a_re + lb_im * a_im) / den
    k_im = (lb_im * a_re - nr * a_im) / den
    b_re = p['ssm_b_re'].astype(f32)
    b_im = p['ssm_b_im'].astype(f32)
    bb_re = k_re[..., None] * b_re - k_im[..., None] * b_im
    bb_im = k_re[..., None] * b_im + k_im[..., None] * b_re
    bu_re = jnp.einsum('blgh,gph->blgp', ug, bb_re)
    bu_im = jnp.einsum('blgh,gph->blgp', ug, bb_im)
    la_re = jnp.broadcast_to(lb_re, bu_re.shape)
    la_im = jnp.broadcast_to(lb_im, bu_im.shape)
    cr, ci, xr, xi = lax.associative_scan(_ssm_combine, (la_re, la_im, bu_re, bu_im), axis=1)
    h0r = h0_re.astype(f32)[:, None]
    h0i = h0_im.astype(f32)[:, None]
    xr = xr + cr * h0r - ci * h0i
    xi = xi + cr * h0i + ci * h0r
    y = (jnp.einsum('blgp,ghp->blgh', xr, p['ssm_c_re'].astype(f32))
         - jnp.einsum('blgp,ghp->blgh', xi, p['ssm_c_im'].astype(f32)))
    y = y.reshape(bsz, L, BRANCH_W) + p['ssm_d'].astype(f32) * u32
    z = jax.nn.gelu(y)
    out = z * jax.nn.sigmoid(z @ p['ssm_w_glu'].astype(f32) + p['ssm_b_glu'].astype(f32))
    return out.astype(u.dtype), xr[:, -1].astype(h0_re.dtype), xi[:, -1].astype(h0_im.dtype)


def _multiscale_pool(x_in, buf, pos0, w_pool, scale):
    f32 = jnp.float32
    bsz, L, C = x_in.shape
    xc = jnp.concatenate([buf.astype(x_in.dtype), x_in], axis=1)
    x32 = xc.astype(f32)
    cs = jnp.concatenate([jnp.zeros((bsz, 1, C), f32), jnp.cumsum(x32, axis=1)], axis=1)
    pos = pos0 + jnp.arange(L)
    outs = []
    for gi, win in enumerate(POOL_WINDOWS):
        lo, hi = gi * POOL_GW, (gi + 1) * POOL_GW
        s = cs[:, POOL_HIST + 1:POOL_HIST + 1 + L, lo:hi] - cs[:, POOL_HIST + 1 - win:POOL_HIST + 1 - win + L, lo:hi]
        cnt = jnp.minimum(pos + 1, win).astype(f32)[None, :, None]
        outs.append(s / cnt - x32[:, POOL_HIST:, lo:hi])
    pooled = jnp.stack(outs, axis=2)
    mixed = jnp.einsum('blgc,gcd->blgd', pooled, w_pool.astype(f32)).reshape(bsz, L, C)
    return (mixed * scale.astype(f32)).astype(x_in.dtype), xc[:, -POOL_HIST:]


def _mixer_block(h, conv_buf, pool_buf, s_re, s_im, pos0, p):
    bsz, L, _ = h.shape
    W = BRANCH_W
    proj = h @ p['w_in']
    uv = jax.nn.gelu(proj[..., OFF_A:OFF_A + 2 * W])
    u = uv[..., :W]
    v = _layernorm(uv[..., W:], p['sgu_ln_g'], p['sgu_ln_b'])
    o_a = _chunk_sgu(u, v, p['sgu_w'], p['sgu_b'])
    g = proj[..., OFF_B:OFF_B + W] * jax.nn.sigmoid(proj[..., OFF_B + W:OFF_B + 2 * W])
    c, new_conv = _causal_dwconv(g, conv_buf, p['conv_w'], p['conv_b'])
    o_b = jax.nn.silu(_layernorm(c, p['conv_ln_g'], p['conv_ln_b']))
    o_c, new_re, new_im = _s5(proj[..., OFF_C:OFF_C + W], s_re, s_im, p)
    o_d, new_pool = _multiscale_pool(proj[..., OFF_D:OFF_D + W], pool_buf, pos0, p['pool_w'], p['pool_scale'])
    gates = jax.nn.sigmoid(proj[..., OFF_GATE:].reshape(bsz, L, N_BRANCH, D_MODEL))
    branches = jnp.stack([o_a, o_b, o_c, o_d], axis=2)
    proj_b = jnp.einsum('blnc,ncd->blnd', branches, p['w_branch'])
    merged = jnp.sum(gates * proj_b, axis=2)
    return merged @ p['w_out'], new_conv, new_pool, new_re, new_im, v


def _hier_moe(h, p):
    f32 = jnp.float32
    shp = h.shape
    t = h.reshape(-1, D_MODEL)
    t32 = t.astype(f32)
    g_logits = t32 @ p['router_group_w'].astype(f32) + p['router_group_b'].astype(f32)
    g_prob = jax.nn.softmax(g_logits, axis=-1)
    g_idx = jnp.argmax(g_logits, axis=-1)
    g_w = jnp.max(g_prob, axis=-1)
    g_oh = jax.nn.one_hot(g_idx, N_GROUPS, dtype=f32)
    e_logits = jnp.einsum('td,gde->tge', t32, p['router_expert_w'].astype(f32)) + p['router_expert_b'].astype(f32)
    e_sel = jnp.einsum('tge,tg->te', e_logits, g_oh)
    top_v, top_i = lax.top_k(e_sel, TOP_K)
    top_w = jax.nn.softmax(top_v, axis=-1) * g_w[:, None]
    local = jnp.sum(jax.nn.one_hot(top_i, EXP_PER_GROUP, dtype=f32) * top_w[..., None], axis=1)
    out = jnp.zeros(t.shape, f32)
    for gi in range(N_GROUPS):
        wgt = (local * g_oh[:, gi:gi + 1]).astype(t.dtype)
        lo, hi = gi * EXP_PER_GROUP, (gi + 1) * EXP_PER_GROUP
        a = jnp.einsum('td,edf->tef', t, p['expert_w_gate'][lo:hi])
        b = jnp.einsum('td,edf->tef', t, p['expert_w_up'][lo:hi])
        hid = jax.nn.silu(a) * b * wgt[..., None]
        out = out + jnp.einsum('tef,efd->td', hid, p['expert_w_down'][lo:hi]).astype(f32)
    return out.astype(h.dtype).reshape(shp)


def _layer(x, conv_buf, pool_buf, s_re, s_im, pos0, p):
    mix, new_conv, new_pool, new_re, new_im, v = _mixer_block(_rmsnorm(x, p['norm_mix_g']), conv_buf, pool_buf, s_re, s_im, pos0, p)
    x = x + mix
    x = x + _hier_moe(_rmsnorm(x, p['norm_ffn_g']), p)
    return x, new_conv, new_pool, new_re, new_im, v


def setup_inputs(seed: int = 0) -> dict:
    key = jax.random.key(seed)
    ks = iter(jax.random.split(key, 64))
    f32 = jnp.float32

    def nrm(shape, scale):
        return scale * jax.random.normal(next(ks), shape, f32)

    W = BRANCH_W
    n_idx = jnp.arange(SSM_P, dtype=f32)
    return {
        'x_prompt': nrm((BATCH, SEQ, D_MODEL), 1.0),
        'x_sample': nrm((DEC_BATCH, DEC_SEQ, D_MODEL), 1.0),
        'state_conv': nrm((DEPTH, DEC_BATCH, CONV_K - 1, W), 0.5),
        'state_pool': nrm((DEPTH, DEC_BATCH, POOL_HIST, W), 1.0),
        'state_ssm_re': nrm((DEPTH, DEC_BATCH, SSM_G, SSM_P), 0.1),
        'state_ssm_im': nrm((DEPTH, DEC_BATCH, SSM_G, SSM_P), 0.1),
        'norm_mix_g': 1.0 + nrm((DEPTH, D_MODEL), 0.02),
        'norm_ffn_g': 1.0 + nrm((DEPTH, D_MODEL), 0.02),
        'w_in': nrm((DEPTH, D_MODEL, N_IN), D_MODEL ** -0.5),
        'sgu_ln_g': 1.0 + nrm((DEPTH, W), 0.02),
        'sgu_ln_b': nrm((DEPTH, W), 0.02),
        'sgu_w': nrm((DEPTH, SGU_GROUPS, CHUNK, CHUNK), CHUNK ** -0.5),
        'sgu_b': 1.0 + nrm((DEPTH, SGU_GROUPS, CHUNK), 0.02),
        'conv_w': nrm((DEPTH, CONV_K, W), CONV_K ** -0.5),
        'conv_b': nrm((DEPTH, W), 0.02),
        'conv_ln_g': 1.0 + nrm((DEPTH, W), 0.02),
        'conv_ln_b': nrm((DEPTH, W), 0.02),
        'ssm_a_re': -0.5 + nrm((DEPTH, SSM_G, SSM_P), 0.01),
        'ssm_a_im': math.pi * n_idx + nrm((DEPTH, SSM_G, SSM_P), 0.01),
        'ssm_log_dt': jax.random.uniform(next(ks), (DEPTH, SSM_G), f32, math.log(1e-3), math.log(1e-1)),
        'ssm_b_re': nrm((DEPTH, SSM_G, SSM_P, SSM_H), SSM_H ** -0.5),
        'ssm_b_im': nrm((DEPTH, SSM_G, SSM_P, SSM_H), SSM_H ** -0.5),
        'ssm_c_re': nrm((DEPTH, SSM_G, SSM_H, SSM_P), SSM_P ** -0.5),
        'ssm_c_im': nrm((DEPTH, SSM_G, SSM_H, SSM_P), SSM_P ** -0.5),
        'ssm_d': nrm((DEPTH, W), 1.0),
        'ssm_w_glu': nrm((DEPTH, W, W), W ** -0.5),
        'ssm_b_glu': nrm((DEPTH, W), 0.02),
        'pool_w': nrm((DEPTH, len(POOL_WINDOWS), POOL_GW, POOL_GW), POOL_GW ** -0.5),
        'pool_scale': 1.0 + nrm((DEPTH, W), 0.02),
        'w_branch': nrm((DEPTH, N_BRANCH, W, D_MODEL), W ** -0.5),
        'w_out': nrm((DEPTH, D_MODEL, D_MODEL), D_MODEL ** -0.5),
        'router_group_w': nrm((DEPTH, D_MODEL, N_GROUPS), D_MODEL ** -0.5),
        'router_group_b': nrm((DEPTH, N_GROUPS), 0.01),
        'router_expert_w': nrm((DEPTH, N_GROUPS, D_MODEL, EXP_PER_GROUP), D_MODEL ** -0.5),
        'router_expert_b': nrm((DEPTH, N_GROUPS, EXP_PER_GROUP), 0.01),
        'expert_w_gate': nrm((DEPTH, N_EXPERTS, D_MODEL, D_EXPERT), D_MODEL ** -0.5),
        'expert_w_up': nrm((DEPTH, N_EXPERTS, D_MODEL, D_EXPERT), D_MODEL ** -0.5),
        'expert_w_down': nrm((DEPTH, N_EXPERTS, D_EXPERT, D_MODEL), D_EXPERT ** -0.5),
        'final_norm_g': 1.0 + nrm((D_MODEL,), 0.02),
    }


def reference(x_prompt, x_sample, state_conv, state_pool, state_ssm_re, state_ssm_im,
              norm_mix_g, norm_ffn_g, w_in,
              sgu_ln_g, sgu_ln_b, sgu_w, sgu_b,
              conv_w, conv_b, conv_ln_g, conv_ln_b,
              ssm_a_re, ssm_a_im, ssm_log_dt, ssm_b_re, ssm_b_im, ssm_c_re, ssm_c_im, ssm_d, ssm_w_glu, ssm_b_glu,
              pool_w, pool_scale,
              w_branch, w_out,
              router_group_w, router_group_b, router_expert_w, router_expert_b,
              expert_w_gate, expert_w_up, expert_w_down,
              final_norm_g):
    bp = x_prompt.shape[0]
    conv0 = jnp.zeros((bp, CONV_K - 1, BRANCH_W), x_prompt.dtype)
    pool0 = jnp.zeros((bp, POOL_HIST, BRANCH_W), x_prompt.dtype)
    ssm0 = jnp.zeros((bp, SSM_G, SSM_P), state_ssm_re.dtype)
    yp, ys = x_prompt, x_sample
    conv_p, conv_s, pool_p, pool_s = [], [], [], []
    sre_p, sim_p, sre_s, sim_s, v_s = [], [], [], [], []
    for l in range(DEPTH):
        p = {
            'norm_mix_g': norm_mix_g[l], 'norm_ffn_g': norm_ffn_g[l], 'w_in': w_in[l],
            'sgu_ln_g': sgu_ln_g[l], 'sgu_ln_b': sgu_ln_b[l], 'sgu_w': sgu_w[l], 'sgu_b': sgu_b[l],
            'conv_w': conv_w[l], 'conv_b': conv_b[l], 'conv_ln_g': conv_ln_g[l], 'conv_ln_b': conv_ln_b[l],
            'ssm_a_re': ssm_a_re[l], 'ssm_a_im': ssm_a_im[l], 'ssm_log_dt': ssm_log_dt[l],
            'ssm_b_re': ssm_b_re[l], 'ssm_b_im': ssm_b_im[l], 'ssm_c_re': ssm_c_re[l], 'ssm_c_im': ssm_c_im[l],
            'ssm_d': ssm_d[l], 'ssm_w_glu': ssm_w_glu[l], 'ssm_b_glu': ssm_b_glu[l],
            'pool_w': pool_w[l], 'pool_scale': pool_scale[l],
            'w_branch': w_branch[l], 'w_out': w_out[l],
            'router_group_w': router_group_w[l], 'router_group_b': router_group_b[l],
            'router_expert_w': router_expert_w[l], 'router_expert_b': router_expert_b[l],
            'expert_w_gate': expert_w_gate[l], 'expert_w_up': expert_w_up[l], 'expert_w_down': expert_w_down[l],
        }
        yp, cp, pp, rp, ip, _ = _layer(yp, conv0, pool0, ssm0, ssm0, 0, p)
        ys, cs_, ps, rs, is_, vs = _layer(ys, state_conv[l], state_pool[l], state_ssm_re[l], state_ssm_im[l], PAST_LEN, p)
        conv_p.append(cp); pool_p.append(pp); sre_p.append(rp); sim_p.append(ip)
        conv_s.append(cs_); pool_s.append(ps); sre_s.append(rs); sim_s.append(is_); v_s.append(vs)
    y_prompt = _rmsnorm(yp, final_norm_g)
    y_sample = _rmsnorm(ys, final_norm_g)
    conv_prompt = jnp.stack(conv_p)
    conv_sample = jnp.stack(conv_s)
    pool_prompt = jnp.stack(pool_p)
    pool_sample = jnp.stack(pool_s)
    ssm_re_prompt = jnp.stack(sre_p)
    ssm_im_prompt = jnp.stack(sim_p)
    ssm_re_sample = jnp.stack(sre_s)
    ssm_im_sample = jnp.stack(sim_s)
    chunk_v_sample = jnp.stack(v_s)
    return (y_prompt, y_sample, conv_prompt, conv_sample, pool_prompt, pool_sample,
            ssm_re_prompt, ssm_im_prompt, ssm_re_sample, ssm_im_sample, chunk_v_sample)
```

```python
import functools
import math

import jax
import jax.numpy as jnp
from jax import lax
from jax.experimental import pallas as pl
from jax.experimental.pallas import tpu as pltpu

F32 = jnp.float32
BF16 = jnp.bfloat16
I32 = jnp.int32
U32 = jnp.uint32

EPS = 1e-6
CHUNK = 128
SGU_GROUPS = 4
CONV_K = 31
POOL_WINDOWS = (2, 4, 8, 16)
POOL_HIST = 15
SSM_H = 16
SSM_P = 64
N_GROUPS = 4
EXP_PER_GROUP = 4
N_EXPERTS = 16
PAST_LEN = 16384

LANES = 128
SUBLANES = 8
VMEM_LIMIT = 60 * 1024 * 1024

EXPERT_TILE = 256
TIME_TILE = 256
SCAN_COLS = 512


def _cparams(n_axes):
    return pltpu.CompilerParams(dimension_semantics=("arbitrary",) * n_axes,
                                vmem_limit_bytes=VMEM_LIMIT)


def _gelu(x):
    c = math.sqrt(2.0 / math.pi)
    return 0.5 * x * (1.0 + jnp.tanh(c * (x + 0.044715 * (x * x * x))))


def _sigmoid(x):
    return 0.5 * jnp.tanh(0.5 * x) + 0.5


def _layernorm(x, g, b):
    xc = x - jnp.mean(x, axis=-1, keepdims=True)
    var = jnp.mean(xc * xc, axis=-1, keepdims=True)
    return xc * lax.rsqrt(var + EPS) * g + b


def _rms_scale(x, g):
    return x * lax.rsqrt(jnp.mean(x * x, axis=-1, keepdims=True) + EPS) * g


def _dot(a, b):
    return jnp.dot(a, b, preferred_element_type=F32)


def _cast_rows(src_ref, dst_ref, chunk):
    rows = src_ref.shape[0]

    def body(i, c):
        r = pl.multiple_of(i * chunk, chunk)
        dst_ref[pl.ds(r, chunk), :] = src_ref[pl.ds(r, chunk), :].astype(dst_ref.dtype)
        return c

    lax.fori_loop(0, rows // chunk, body, 0)


def _layer_spec(arr, l):
    nd = arr.ndim
    return pl.BlockSpec((None,) + tuple(arr.shape[1:]), lambda *_: (l,) + (0,) * (nd - 1))


def _pick(n, cands):
    for c in cands:
        if n % c == 0:
            return c
    raise ValueError(f"no tile for {n}")


def _rmsnorm_kernel(x_ref, g_ref, o_ref):
    o_ref[...] = _rms_scale(x_ref[...], g_ref[...]).astype(o_ref.dtype)


def _rmsnorm(x, g_all, l, out_dtype):
    T, D = x.shape
    tm = _pick(T, (640, 512, 384, 256, 128))
    return pl.pallas_call(
        _rmsnorm_kernel,
        out_shape=jax.ShapeDtypeStruct((T, D), out_dtype),
        grid=(T // tm,),
        in_specs=[pl.BlockSpec((tm, D), lambda m: (m, 0)), _layer_spec(g_all, l)],
        out_specs=pl.BlockSpec((tm, D), lambda m: (m, 0)),
        compiler_params=_cparams(1),
        name="rmsnorm",
    )(x, g_all)


def _prep_kernel(are, aim, ldt, bre, bim, sguw, wglu, poolw,
                 lamr_o, lami_o, pwr_o, pwi_o, bbr_o, bbi_o, sguw_o, wglu_o, poolw_o, *, S):
    a_re = are[...]
    a_im = aim[...]
    dt = jnp.exp(ldt[...])
    mag = jnp.exp(a_re * dt)
    lbr = mag * jnp.cos(a_im * dt)
    lbi = mag * jnp.sin(a_im * dt)
    den = a_re * a_re + a_im * a_im
    nr = lbr - 1.0
    kr = (nr * a_re + lbi * a_im) / den
    ki = (lbi * a_re - nr * a_im) / den
    lamr_o[...] = lbr
    lami_o[...] = lbi
    gp = a_re.shape[-1]

    rows = bre.shape[0]
    rc = 64

    def body(i, c):
        r = pl.multiple_of(i * rc, rc)
        br = bre[pl.ds(r, rc), :]
        bi = bim[pl.ds(r, rc), :]
        bbr_o[pl.ds(r, rc), :] = (kr * br - ki * bi).astype(BF16)
        bbi_o[pl.ds(r, rc), :] = (kr * bi + ki * br).astype(BF16)
        return c

    lax.fori_loop(0, rows // rc, body, 0)

    pr, pi = lbr, lbi
    for s in range(S):
        pwr_o[SUBLANES * s:SUBLANES * (s + 1), :] = jnp.broadcast_to(pr, (SUBLANES, gp))
        pwi_o[SUBLANES * s:SUBLANES * (s + 1), :] = jnp.broadcast_to(pi, (SUBLANES, gp))
        pr, pi = pr * lbr - pi * lbi, pr * lbi + pi * lbr

    t_i = lax.broadcasted_iota(I32, (CHUNK, CHUNK), 0)
    s_i = lax.broadcasted_iota(I32, (CHUNK, CHUNK), 1)
    for g in range(SGU_GROUPS):
        sguw_o[g] = jnp.where(t_i >= s_i, sguw[g], 0.0).astype(BF16)
    wglu_o[...] = wglu[...].astype(BF16)
    for g in range(len(POOL_WINDOWS)):
        poolw_o[g] = poolw[g].astype(BF16)


def _prep(a_re, a_im, log_dt, b_re, b_im, sgu_w, w_glu, pool_w, S):
    L, G, P = a_re.shape
    H = b_re.shape[-1]
    GP, W = G * P, G * H
    eye = jnp.eye(G, dtype=F32)
    bre_bd = jnp.einsum("lgph,gk->lghkp", b_re, eye).reshape(L, W, GP)
    bim_bd = jnp.einsum("lgph,gk->lghkp", b_im, eye).reshape(L, W, GP)
    are2 = a_re.reshape(L, 1, GP)
    aim2 = a_im.reshape(L, 1, GP)
    ldt2 = jnp.repeat(log_dt, P, axis=-1).reshape(L, 1, GP)

    def lspec(shape):
        nd = len(shape)
        return pl.BlockSpec((None,) + tuple(shape[1:]), lambda l: (l,) + (0,) * (nd - 1))

    ins = [are2, aim2, ldt2, bre_bd, bim_bd, sgu_w, w_glu, pool_w]
    out_shapes = [
        jax.ShapeDtypeStruct((L, 1, GP), F32), jax.ShapeDtypeStruct((L, 1, GP), F32),
        jax.ShapeDtypeStruct((L, SUBLANES * S, GP), F32), jax.ShapeDtypeStruct((L, SUBLANES * S, GP), F32),
        jax.ShapeDtypeStruct((L, W, GP), BF16), jax.ShapeDtypeStruct((L, W, GP), BF16),
        jax.ShapeDtypeStruct(sgu_w.shape, BF16), jax.ShapeDtypeStruct(w_glu.shape, BF16),
        jax.ShapeDtypeStruct(pool_w.shape, BF16),
    ]
    return pl.pallas_call(
        functools.partial(_prep_kernel, S=S),
        out_shape=out_shapes,
        grid=(L,),
        in_specs=[lspec(a.shape) for a in ins],
        out_specs=[lspec(o.shape) for o in out_shapes],
        compiler_params=_cparams(1),
        name="ssm_prep",
    )(*ins)


def _proj_kernel(h_ref, w_ref, o_ref, wbf_ref):
    @pl.when(pl.program_id(1) == 0)
    def _():
        _cast_rows(w_ref, wbf_ref, 256)

    o_ref[...] = _dot(h_ref[...], wbf_ref[...])


def _proj(h, w_in, l, n_cols):
    T, D = h.shape
    tm = _pick(T, (640, 512, 384, 256, 128))
    tn = 1024
    return pl.pallas_call(
        _proj_kernel,
        out_shape=jax.ShapeDtypeStruct((T, n_cols), F32),
        grid=(n_cols // tn, T // tm),
        in_specs=[pl.BlockSpec((tm, D), lambda n, m: (m, 0)),
                  pl.BlockSpec((None, D, tn), lambda n, m: (l, 0, n))],
        out_specs=pl.BlockSpec((tm, tn), lambda n, m: (m, n)),
        scratch_shapes=[pltpu.VMEM((D, tn), BF16)],
        compiler_params=_cparams(2),
        name="in_proj",
    )(h, w_in)


def _mixer_prompt_kernel(p_ref, lng, lnb, sguw, sgub, cw, cb, clg, clb,
                         lamr, lami, pwr, pwi, bbr, bbi, cr, ci, sd, wglu, bglu, poolw, pscale,
                         o_ref, conv_o, pool_o, sre_o, sim_o,
                         gbuf, pbuf, xr, xi, car_re, car_im, fin_re, fin_im, cm_re, cm_im,
                         *, tt, S, W, GP):
    j = pl.program_id(1)
    nt = pl.num_programs(1)
    GH = 32
    PH = 16

    @pl.when(j == 0)
    def _():
        gbuf[0:GH, :] = jnp.zeros((GH, W), F32)
        pbuf[0:PH, :] = jnp.zeros((PH, W), F32)
        car_re[...] = jnp.zeros_like(car_re)
        car_im[...] = jnp.zeros_like(car_im)

    for c in range(tt // CHUNK):
        rows = slice(c * CHUNK, (c + 1) * CHUNK)
        u = _gelu(p_ref[rows, 0:W])
        v = _layernorm(_gelu(p_ref[rows, W:2 * W]), lng[...], lnb[...])
        vb = v.astype(BF16)
        gw = W // SGU_GROUPS
        for g in range(SGU_GROUPS):
            cols = slice(g * gw, (g + 1) * gw)
            mixed = _dot(sguw[g], vb[:, cols]) + sgub[:, g:g + 1]
            o_ref[rows, cols] = (u[:, cols] * mixed).astype(BF16)

    rc = 64
    for c in range(tt // rc):
        rows = slice(c * rc, (c + 1) * rc)
        gbuf[GH + c * rc:GH + (c + 1) * rc, :] = p_ref[rows, 2 * W:3 * W] * _sigmoid(p_ref[rows, 3 * W:4 * W])
    rc = 32
    base = GH - (CONV_K - 1)
    for c in range(tt // rc):
        acc = jnp.zeros((rc, W), F32)
        for k in range(CONV_K):
            r0 = base + c * rc + k
            acc = acc + cw[k:k + 1, :] * gbuf[r0:r0 + rc, :]
        y = _layernorm(acc + cb[...], clg[...], clb[...])
        o_ref[c * rc:(c + 1) * rc, W:2 * W] = (y * _sigmoid(y)).astype(BF16)
    gbuf[0:GH, :] = gbuf[tt:tt + GH, :]

    @pl.when(j == nt - 1)
    def _():
        conv_o[...] = gbuf[GH - (CONV_K - 1):GH, :]

    uc = p_ref[:, 4 * W:5 * W]
    r_i = lax.broadcasted_iota(I32, (tt, tt), 0)
    c_i = lax.broadcasted_iota(I32, (tt, tt), 1)
    perm = jnp.where(((r_i & (SUBLANES - 1)) * S + (r_i >> 3)) == c_i, 1.0, 0.0).astype(BF16)
    up = _dot(perm, uc.astype(BF16)).astype(BF16)
    xr[...] = _dot(up, bbr[...])
    xi[...] = _dot(up, bbi[...])

    for cbi in range(GP // SCAN_COLS):
        cols = slice(cbi * SCAN_COLS, (cbi + 1) * SCAN_COLS)
        lr = jnp.broadcast_to(lamr[:, cols], (SUBLANES, SCAN_COLS))
        li = jnp.broadcast_to(lami[:, cols], (SUBLANES, SCAN_COLS))

        def step(s, carry, cols=cols, lr=lr, li=li):
            sr, si = carry
            r0 = pl.multiple_of(s * SUBLANES, SUBLANES)
            nr = lr * sr - li * si + xr[pl.ds(r0, SUBLANES), cols]
            ni = lr * si + li * sr + xi[pl.ds(r0, SUBLANES), cols]
            xr[pl.ds(r0, SUBLANES), cols] = nr
            xi[pl.ds(r0, SUBLANES), cols] = ni
            return nr, ni

        z = jnp.zeros((SUBLANES, SCAN_COLS), F32)
        fr, fi = lax.fori_loop(0, S, step, (z, z))
        fin_re[:, cols] = fr
        fin_im[:, cols] = fi

    lsr = pwr[SUBLANES * (S - 1):SUBLANES * (S - 1) + 1, :]
    lsi = pwi[SUBLANES * (S - 1):SUBLANES * (S - 1) + 1, :]
    c_r = car_re[0:1, :]
    c_im = car_im[0:1, :]
    cm_re[0:1, :] = c_r
    cm_im[0:1, :] = c_im
    for q in range(1, SUBLANES):
        f_r = fin_re[q - 1:q, :]
        f_i = fin_im[q - 1:q, :]
        c_r, c_im = f_r + lsr * c_r - lsi * c_im, f_i + lsr * c_im + lsi * c_r
        cm_re[q:q + 1, :] = c_r
        cm_im[q:q + 1, :] = c_im
    n_r = fin_re[SUBLANES - 1:SUBLANES, :] + lsr * c_r - lsi * c_im
    n_i = fin_im[SUBLANES - 1:SUBLANES, :] + lsr * c_im + lsi * c_r
    car_re[0:1, :] = n_r
    car_im[0:1, :] = n_i

    @pl.when(j == nt - 1)
    def _():
        sre_o[...] = n_r
        sim_o[...] = n_i

    for cbi in range(GP // SCAN_COLS):
        cols = slice(cbi * SCAN_COLS, (cbi + 1) * SCAN_COLS)
        mr = cm_re[:, cols]
        mi = cm_im[:, cols]

        def fix(s, c, cols=cols, mr=mr, mi=mi):
            r0 = pl.multiple_of(s * SUBLANES, SUBLANES)
            pr = pwr[pl.ds(r0, SUBLANES), cols]
            pi = pwi[pl.ds(r0, SUBLANES), cols]
            xr[pl.ds(r0, SUBLANES), cols] = xr[pl.ds(r0, SUBLANES), cols] + (pr * mr - pi * mi)
            xi[pl.ds(r0, SUBLANES), cols] = xi[pl.ds(r0, SUBLANES), cols] + (pr * mi + pi * mr)
            return c

        lax.fori_loop(0, S, fix, 0)

    yp = _dot(xr[...].astype(BF16), cr[...]) - _dot(xi[...].astype(BF16), ci[...])
    unperm = jnp.where(((c_i & (SUBLANES - 1)) * S + (c_i >> 3)) == r_i, 1.0, 0.0).astype(BF16)
    y1 = yp.astype(BF16)
    r1 = yp - y1.astype(F32)
    y2 = r1.astype(BF16)
    y3 = (r1 - y2.astype(F32)).astype(BF16)
    y = _dot(unperm, y1) + _dot(unperm, y2) + _dot(unperm, y3)
    z = _gelu(y + sd[...] * uc)
    gl = _dot(z.astype(BF16), wglu[...]) + bglu[...]
    o_ref[:, 2 * W:3 * W] = (z * _sigmoid(gl)).astype(BF16)

    xd = p_ref[:, 5 * W:6 * W]
    pbuf[PH:PH + tt, :] = xd
    gw = W // len(POOL_WINDOWS)
    pos = (j * tt + lax.broadcasted_iota(I32, (tt, 1), 0) + 1).astype(F32)
    for gi, win in enumerate(POOL_WINDOWS):
        cols = slice(gi * gw, (gi + 1) * gw)
        s = xd[:, cols]
        for i in range(1, win):
            s = s + pbuf[PH - i:PH - i + tt, cols]
        cnt = jnp.minimum(pos, float(win))
        pooled = s / cnt - xd[:, cols]
        mixed = _dot(pooled.astype(BF16), poolw[gi]) * pscale[:, cols]
        o_ref[:, 3 * W + gi * gw:3 * W + (gi + 1) * gw] = mixed.astype(BF16)
    pbuf[0:PH, :] = pbuf[tt:tt + PH, :]

    @pl.when(j == nt - 1)
    def _():
        pool_o[...] = pbuf[PH - POOL_HIST:PH, :]


def _mixer_prompt(proj6, B, L, T, l, small, prep, cr, ci):
    W = small["sgu_ln_g"].shape[-1]
    GP = prep["lamr"].shape[-1]
    D4 = 4 * W
    tt = TIME_TILE
    S = tt // SUBLANES
    nt = L // tt
    names_small = ["sgu_ln_g", "sgu_ln_b"]
    ins = [proj6,
           small["sgu_ln_g"], small["sgu_ln_b"], prep["sguw"], small["sgu_bT"],
           small["conv_w"], small["conv_b"], small["conv_ln_g"], small["conv_ln_b"],
           prep["lamr"], prep["lami"], prep["pwr"], prep["pwi"], prep["bbr"], prep["bbi"], cr, ci,
           small["ssm_d"], prep["wglu"], small["ssm_b_glu"], prep["poolw"], small["pool_scale"]]
    del names_small
    in_specs = [pl.BlockSpec((tt, proj6.shape[1]), lambda b, j: (b * nt + j, 0))]
    in_specs += [_layer_spec(a, l) for a in ins[1:]]
    out_shapes = [
        jax.ShapeDtypeStruct((T, D4), BF16),
        jax.ShapeDtypeStruct((B, CONV_K - 1, W), F32),
        jax.ShapeDtypeStruct((B, POOL_HIST, W), F32),
        jax.ShapeDtypeStruct((B, 1, GP), F32),
        jax.ShapeDtypeStruct((B, 1, GP), F32),
    ]
    out_specs = [
        pl.BlockSpec((tt, D4), lambda b, j: (b * nt + j, 0)),
        pl.BlockSpec((None, CONV_K - 1, W), lambda b, j: (b, 0, 0)),
        pl.BlockSpec((None, POOL_HIST, W), lambda b, j: (b, 0, 0)),
        pl.BlockSpec((None, 1, GP), lambda b, j: (b, 0, 0)),
        pl.BlockSpec((None, 1, GP), lambda b, j: (b, 0, 0)),
    ]
    scratch = [
        pltpu.VMEM((32 + tt, W), F32), pltpu.VMEM((16 + tt, W), F32),
        pltpu.VMEM((tt, GP), F32), pltpu.VMEM((tt, GP), F32),
        pltpu.VMEM((SUBLANES, GP), F32), pltpu.VMEM((SUBLANES, GP), F32),
        pltpu.VMEM((SUBLANES, GP), F32), pltpu.VMEM((SUBLANES, GP), F32),
        pltpu.VMEM((SUBLANES, GP), F32), pltpu.VMEM((SUBLANES, GP), F32),
    ]
    return pl.pallas_call(
        functools.partial(_mixer_prompt_kernel, tt=tt, S=S, W=W, GP=GP),
        out_shape=out_shapes,
        grid=(B, nt),
        in_specs=in_specs,
        out_specs=out_specs,
        scratch_shapes=scratch,
        compiler_params=_cparams(2),
        name="mixer_prompt",
    )(*ins)


def _mixer_sample_kernel(p_ref, stc, stp, hre, him, lng, lnb, w00, b0, cw, cb, clg, clb,
                         lamr, lami, bbr, bbi, cr, ci, sd, wglu, bglu, poolw, pscale, o_in,
                         o_ref, g_o, v_o, xre_o, xim_o, *, W):
    del o_in
    u = _gelu(p_ref[:, 0:W])
    v = _layernorm(_gelu(p_ref[:, W:2 * W]), lng[...], lnb[...])
    v_o[...] = v
    o_ref[:, 0:W] = (u * (w00[...] * v + b0[...])).astype(BF16)

    g = p_ref[:, 2 * W:3 * W] * _sigmoid(p_ref[:, 3 * W:4 * W])
    g_o[...] = g
    acc = cw[CONV_K - 1:CONV_K, :] * g
    for k in range(CONV_K - 1):
        acc = acc + cw[k:k + 1, :] * stc[:, k * W:(k + 1) * W]
    y = _layernorm(acc + cb[...], clg[...], clb[...])
    o_ref[:, W:2 * W] = (y * _sigmoid(y)).astype(BF16)

    uc = p_ref[:, 4 * W:5 * W]
    ub = uc.astype(BF16)
    h_r = hre[...]
    h_i = him[...]
    l_r = lamr[...]
    l_i = lami[...]
    x_r = l_r * h_r - l_i * h_i + _dot(ub, bbr[...])
    x_i = l_r * h_i + l_i * h_r + _dot(ub, bbi[...])
    xre_o[...] = x_r
    xim_o[...] = x_i
    yv = _dot(x_r.astype(BF16), cr[...]) - _dot(x_i.astype(BF16), ci[...]) + sd[...] * uc
    z = _gelu(yv)
    gl = _dot(z.astype(BF16), wglu[...]) + bglu[...]
    o_ref[:, 2 * W:3 * W] = (z * _sigmoid(gl)).astype(BF16)

    xd = p_ref[:, 5 * W:6 * W]
    gw = W // len(POOL_WINDOWS)
    for gi, win in enumerate(POOL_WINDOWS):
        cols = slice(gi * gw, (gi + 1) * gw)
        s = xd[:, cols]
        for i in range(1, win):
            r = POOL_HIST - i
            s = s + stp[:, r * W + gi * gw:r * W + (gi + 1) * gw]
        cnt = float(min(PAST_LEN + 1, win))
        pooled = s / cnt - xd[:, cols]
        mixed = _dot(pooled.astype(BF16), poolw[gi]) * pscale[:, cols]
        o_ref[:, 3 * W + gi * gw:3 * W + (gi + 1) * gw] = mixed.astype(BF16)


def _mixer_sample(proj6, o_prev, stc2, stp2, hre2, him2, DB, T, l, small, prep, cr, ci):
    W = small["sgu_ln_g"].shape[-1]
    GP = prep["lamr"].shape[-1]
    D4 = 4 * W
    blk = (T - DB) // DB
    ins = [proj6, stc2, stp2, hre2, him2,
           small["sgu_ln_g"], small["sgu_ln_b"], small["sgu_w00"], small["sgu_b0"],
           small["conv_w"], small["conv_b"], small["conv_ln_g"], small["conv_ln_b"],
           prep["lamr"], prep["lami"], prep["bbr"], prep["bbi"], cr, ci,
           small["ssm_d"], prep["wglu"], small["ssm_b_glu"], prep["poolw"], small["pool_scale"], o_prev]
    in_specs = [pl.BlockSpec((DB, proj6.shape[1]), lambda i: (blk, 0))]
    in_specs += [_layer_spec(a, l) for a in ins[1:-1]]
    in_specs += [pl.BlockSpec(memory_space=pl.ANY)]
    out_shapes = [
        jax.ShapeDtypeStruct((T, D4), BF16),
        jax.ShapeDtypeStruct((DB, W), F32), jax.ShapeDtypeStruct((DB, W), F32),
        jax.ShapeDtypeStruct((DB, GP), F32), jax.ShapeDtypeStruct((DB, GP), F32),
    ]
    out_specs = [
        pl.BlockSpec((DB, D4), lambda i: (blk, 0)),
        pl.BlockSpec((DB, W), lambda i: (0, 0)), pl.BlockSpec((DB, W), lambda i: (0, 0)),
        pl.BlockSpec((DB, GP), lambda i: (0, 0)), pl.BlockSpec((DB, GP), lambda i: (0, 0)),
    ]
    return pl.pallas_call(
        functools.partial(_mixer_sample_kernel, W=W),
        out_shape=out_shapes,
        grid=(1,),
        in_specs=in_specs,
        out_specs=out_specs,
        input_output_aliases={len(ins) - 1: 0},
        compiler_params=_cparams(1),
        name="mixer_sample",
    )(*ins)


def _merge_kernel(h_ref, o_ref, wg0, wg1, wg2, wg3, wb_ref, m_ref, wgbf, wbbf, *, W):
    wgs = (wg0, wg1, wg2, wg3)

    @pl.when(pl.program_id(1) == 0)
    def _():
        for b in range(4):
            _cast_rows(wgs[b], wgbf.at[b], 256)
            wbbf[b] = wb_ref[b].astype(BF16)

    h = h_ref[...]
    acc = None
    for b in range(4):
        gate = _sigmoid(_dot(h, wgbf[b]))
        term = gate * _dot(o_ref[:, b * W:(b + 1) * W], wbbf[b])
        acc = term if acc is None else acc + term
    m_ref[...] = acc.astype(BF16)


def _merge(h, o, w_in, w_branch, l, gate_off):
    T, D = h.shape
    W = w_branch.shape[2]
    tm = _pick(T, (640, 512, 384, 256, 128))
    tn = 256
    nb = D // tn

    def gate_spec(b):
        blk0 = (gate_off + b * D) // tn
        return pl.BlockSpec((None, D, tn), lambda n, m: (l, 0, blk0 + n))

    return pl.pallas_call(
        functools.partial(_merge_kernel, W=W),
        out_shape=jax.ShapeDtypeStruct((T, D), BF16),
        grid=(nb, T // tm),
        in_specs=[pl.BlockSpec((tm, D), lambda n, m: (m, 0)),
                  pl.BlockSpec((tm, 4 * W), lambda n, m: (m, 0)),
                  gate_spec(0), gate_spec(1), gate_spec(2), gate_spec(3),
                  pl.BlockSpec((None, 4, W, tn), lambda n, m: (l, 0, 0, n))],
        out_specs=pl.BlockSpec((tm, tn), lambda n, m: (m, n)),
        scratch_shapes=[pltpu.VMEM((4, D, tn), BF16), pltpu.VMEM((4, W, tn), BF16)],
        compiler_params=_cparams(2),
        name="gated_merge",
    )(h, o, w_in, w_in, w_in, w_in, w_branch)


def _out_kernel(m_ref, x_ref, w_ref, g_ref, wr_ref, x1_ref, hp_ref, lg_ref, wbf, wrh, wrl):
    @pl.when(pl.program_id(0) == 0)
    def _():
        _cast_rows(w_ref, wbf, 256)
        wr = wr_ref[...]
        hi = wr.astype(BF16)
        wrh[...] = hi
        wrl[...] = (wr - hi.astype(F32)).astype(BF16)

    x1 = x_ref[...] + _dot(m_ref[...], wbf[...])
    x1_ref[...] = x1
    h = _rms_scale(x1, g_ref[...])
    hb = h.astype(BF16)
    hl = (h - hb.astype(F32)).astype(BF16)
    lg_ref[...] = _dot(hb, wrh[...]) + _dot(hl, wrh[...]) + _dot(hb, wrl[...])
    bits = lax.bitcast_convert_type(hb.astype(F32), U32)
    half = bits.shape[1] // 2
    hp_ref[...] = (bits[:, :half] >> 16) | (bits[:, half:] & jnp.uint32(0xFFFF0000))


def _out_proj(merged, x, w_out, g_all, wr_all, l):
    T, D = x.shape
    tm = _pick(T, (320, 256, 128))
    R = wr_all.shape[-1]
    return pl.pallas_call(
        _out_kernel,
        out_shape=[jax.ShapeDtypeStruct((T, D), F32), jax.ShapeDtypeStruct((T, D // 2), U32),
                   jax.ShapeDtypeStruct((T, R), F32)],
        grid=(T // tm,),
        in_specs=[pl.BlockSpec((tm, D), lambda m: (m, 0)),
                  pl.BlockSpec((tm, D), lambda m: (m, 0)),
                  pl.BlockSpec((None, D, D), lambda m: (l, 0, 0), pipeline_mode=pl.Buffered(1)),
                  _layer_spec(g_all, l), _layer_spec(wr_all, l)],
        out_specs=[pl.BlockSpec((tm, D), lambda m: (m, 0)),
                   pl.BlockSpec((tm, D // 2), lambda m: (m, 0)),
                   pl.BlockSpec((tm, R), lambda m: (m, 0))],
        scratch_shapes=[pltpu.VMEM((D, D), BF16), pltpu.VMEM((D, R), BF16), pltpu.VMEM((D, R), BF16)],
        compiler_params=_cparams(1),
        name="out_proj",
    )(merged, x, w_out, g_all, wr_all)


def _route_kernel(lg_ref, bias_ref, rt_ref, meta_ref, cnt, off, *, tm, tile, n_tiles_max):
    p = pl.program_id(0)
    m = pl.program_id(1)
    lane = lax.broadcasted_iota(I32, (tm, LANES), 1).astype(F32)
    neg = jnp.float32(-jnp.inf)
    big = jnp.float32(1e9)

    @pl.when((p == 0) & (m == 0))
    def _():
        cnt[...] = jnp.zeros_like(cnt)

    lg = lg_ref[...] + bias_ref[...]
    is_g = lane < N_GROUPS
    gl = jnp.where(is_g, lg, neg)
    gmax = jnp.max(gl, axis=-1, keepdims=True)
    gidx = jnp.min(jnp.where(gl == gmax, lane, big), axis=-1, keepdims=True)
    gsum = jnp.sum(jnp.where(is_g, jnp.exp(gl - gmax), 0.0), axis=-1, keepdims=True)
    g_w = 1.0 / gsum
    lo = N_GROUPS + EXP_PER_GROUP * gidx
    in_grp = (lane >= lo) & (lane < lo + EXP_PER_GROUP)
    el = jnp.where(in_grp, lg, neg)
    v1 = jnp.max(el, axis=-1, keepdims=True)
    i1 = jnp.min(jnp.where(el == v1, lane, big), axis=-1, keepdims=True)
    el2 = jnp.where(lane == i1, neg, el)
    v2 = jnp.max(el2, axis=-1, keepdims=True)
    i2 = jnp.min(jnp.where(el2 == v2, lane, big), axis=-1, keepdims=True)
    e2x = jnp.exp(v2 - v1)
    w1 = g_w / (1.0 + e2x)
    w2 = g_w * e2x / (1.0 + e2x)
    e1 = i1 - N_GROUPS
    e2 = i2 - N_GROUPS
    a1 = jnp.where(lane == e1, 1.0, 0.0)
    a2 = jnp.where(lane == e2, 1.0, 0.0)
    a = a1 + a2

    @pl.when(p == 0)
    def _():
        cnt[0:1, :] = cnt[0:1, :] + jnp.sum(a, axis=0, keepdims=True)

    @pl.when((p == 1) & (m == 0))
    def _():
        counts = cnt[0:1, :]
        tiles = jnp.floor((counts + (tile - 1)) * (1.0 / tile))
        tiles8 = jnp.broadcast_to(tiles, (SUBLANES, LANES)).astype(BF16)
        ri = lax.broadcasted_iota(I32, (LANES, LANES), 0)
        ci = lax.broadcasted_iota(I32, (LANES, LANES), 1)
        upper = jnp.where(ri < ci, 1.0, 0.0).astype(BF16)
        toff = _dot(tiles8, upper)[0:1, :]
        off[0:1, :] = toff * tile
        tend = toff + tiles
        lane1 = lax.broadcasted_iota(I32, (1, LANES), 1).astype(F32)
        n_act = jnp.sum(jnp.where(lane1 == N_EXPERTS - 1, tend, 0.0), axis=-1, keepdims=True)
        texp = jnp.zeros((1, LANES), F32)
        for e in range(N_EXPERTS - 1):
            end_e = jnp.sum(jnp.where(lane1 == e, tend, 0.0), axis=-1, keepdims=True)
            texp = texp + jnp.where(jnp.minimum(lane1, n_act - 1.0) >= end_e, 1.0, 0.0)
        meta_ref[0:1, :] = counts
        meta_ref[1:2, :] = off[0:1, :]
        meta_ref[2:3, :] = texp
        meta_ref[3:4, :] = jnp.broadcast_to(n_act, (1, LANES))
        meta_ref[4:8, :] = jnp.zeros((4, LANES), F32)
        cnt[...] = jnp.zeros_like(cnt)

    @pl.when(p == 1)
    def _():
        ri = lax.broadcasted_iota(I32, (tm, tm), 0)
        ci = lax.broadcasted_iota(I32, (tm, tm), 1)
        ltri = jnp.where(ci < ri, 1.0, 0.0).astype(BF16)
        cum = _dot(ltri, a.astype(BF16)) + cnt[0:1, :] + off[0:1, :]
        pos1 = jnp.sum(a1 * cum, axis=-1, keepdims=True)
        pos2 = jnp.sum(a2 * cum, axis=-1, keepdims=True)
        cnt[0:1, :] = cnt[0:1, :] + jnp.sum(a, axis=0, keepdims=True)
        rt = jnp.where(lane == 0, e1, 0.0)
        rt = jnp.where(lane == 1, e2, rt)
        rt = jnp.where(lane == 2, w1, rt)
        rt = jnp.where(lane == 3, w2, rt)
        rt = jnp.where(lane == 4, pos1, rt)
        rt = jnp.where(lane == 5, pos2, rt)
        rt_ref[...] = rt


def _route(logits, bias_all, l, tile, n_tiles_max):
    T, R = logits.shape
    tm = _pick(T, (640, 512, 384, 256, 128))
    return pl.pallas_call(
        functools.partial(_route_kernel, tm=tm, tile=tile, n_tiles_max=n_tiles_max),
        out_shape=[jax.ShapeDtypeStruct((T, R), F32), jax.ShapeDtypeStruct((SUBLANES, LANES), F32)],
        grid=(2, T // tm),
        in_specs=[pl.BlockSpec((tm, R), lambda p, m: (m, 0)), _layer_spec(bias_all, l)],
        out_specs=[pl.BlockSpec((tm, R), lambda p, m: (m * p, 0)),
                   pl.BlockSpec((SUBLANES, LANES), lambda p, m: (0, 0))],
        scratch_shapes=[pltpu.VMEM((SUBLANES, LANES), F32), pltpu.VMEM((SUBLANES, LANES), F32)],
        compiler_params=_cparams(2),
        name="route",
    )(logits, bias_all)


def _dispatch_kernel(pos_ref, hp_ref, hs_in, hs_ref, sem, *, T, ch):
    del hs_in
    n_chunks = T // ch

    def copies(c):
        out = []
        for u in range(ch):
            t = c * ch + u
            for k in range(2):
                p = pos_ref[k * T + t]
                out.append(pltpu.make_async_copy(hp_ref.at[pl.ds(t, 1)], hs_ref.at[pl.ds(p, 1)], sem))
        return out

    def body(c, carry):
        for cp in copies(c):
            cp.start()

        @pl.when(c > 0)
        def _():
            for cp in copies(c - 1):
                cp.wait()

        return carry

    lax.fori_loop(0, n_chunks, body, 0)
    for cp in copies(n_chunks - 1):
        cp.wait()


def _dispatch(pos_flat, hp, n_pad):
    T, Dh = hp.shape
    ch = _pick(T, (64, 32, 16, 8))
    hs0 = jnp.zeros((n_pad, Dh), U32)
    return pl.pallas_call(
        functools.partial(_dispatch_kernel, T=T, ch=ch),
        out_shape=jax.ShapeDtypeStruct((n_pad, Dh), U32),
        grid_spec=pltpu.PrefetchScalarGridSpec(
            num_scalar_prefetch=1,
            grid=(1,),
            in_specs=[pl.BlockSpec(memory_space=pl.ANY), pl.BlockSpec(memory_space=pl.ANY)],
            out_specs=pl.BlockSpec(memory_space=pl.ANY),
            scratch_shapes=[pltpu.SemaphoreType.DMA],
        ),
        input_output_aliases={2: 0},
        compiler_params=_cparams(1),
        name="dispatch",
    )(pos_flat, hp, hs0)


def _expert_kernel(te_ref, na_ref, hs_ref, wg_ref, wu_ref, wd_ref, ys_ref, wgbf, wubf, wdbf):
    i = pl.program_id(0)
    prev = te_ref[jnp.maximum(i - 1, 0)]
    changed = (i == 0) | (te_ref[i] != prev)

    @pl.when(changed)
    def _():
        _cast_rows(wg_ref, wgbf, 256)
        _cast_rows(wu_ref, wubf, 256)
        _cast_rows(wd_ref, wdbf, 256)

    @pl.when(i < na_ref[0])
    def _():
        w = hs_ref[...]
        half = w.shape[1]
        lo = lax.bitcast_convert_type(w << 16, F32).astype(BF16)
        hi = lax.bitcast_convert_type(w & jnp.uint32(0xFFFF0000), F32).astype(BF16)
        a = _dot(lo, wgbf[0:half, :]) + _dot(hi, wgbf[half:, :])
        b = _dot(lo, wubf[0:half, :]) + _dot(hi, wubf[half:, :])
        hid = (a * _sigmoid(a) * b).astype(BF16)
        ys_ref[...] = _dot(hid, wdbf[...])


def _experts(te, na, hs, wg, wu, wd, l, tile, n_tiles_max):
    n_pad, Dh = hs.shape
    D = 2 * Dh
    F = wg.shape[-1]

    def row_map(i, te_ref, na_ref):
        return (jnp.minimum(i, na_ref[0] - 1), 0)

    return pl.pallas_call(
        _expert_kernel,
        out_shape=jax.ShapeDtypeStruct((n_pad, D), F32),
        grid_spec=pltpu.PrefetchScalarGridSpec(
            num_scalar_prefetch=2,
            grid=(n_tiles_max,),
            in_specs=[pl.BlockSpec((tile, Dh), row_map),
                      pl.BlockSpec((None, None, D, F), lambda i, te_ref, na_ref: (l, te_ref[i], 0, 0)),
                      pl.BlockSpec((None, None, D, F), lambda i, te_ref, na_ref: (l, te_ref[i], 0, 0)),
                      pl.BlockSpec((None, None, F, D), lambda i, te_ref, na_ref: (l, te_ref[i], 0, 0))],
            out_specs=pl.BlockSpec((tile, D), row_map),
            scratch_shapes=[pltpu.VMEM((D, F), BF16), pltpu.VMEM((D, F), BF16), pltpu.VMEM((F, D), BF16)],
        ),
        compiler_params=_cparams(1),
        name="experts",
    )(te, na, hs, wg, wu, wd)


def _combine_kernel(pos_ref, x1_ref, rt_ref, g_ref, ys_ref, x2_ref, h_ref, buf, sem, *, T, tk):
    i = pl.program_id(0)
    n = pl.num_programs(0)

    def copies(step, slot):
        out = []
        for u in range(tk):
            t = step * tk + u
            for k in range(2):
                p = pos_ref[k * T + t]
                out.append(pltpu.make_async_copy(ys_ref.at[pl.ds(p, 1)], buf.at[slot, k, pl.ds(u, 1)],
                                                 sem.at[slot]))
        return out

    slot = lax.rem(i, 2)

    @pl.when(i == 0)
    def _():
        for cp in copies(0, 0):
            cp.start()

    for s in range(2):
        @pl.when((i + 1 < n) & (slot == 1 - s))
        def _(s=s):
            for cp in copies(i + 1, s):
                cp.start()

    for s in range(2):
        @pl.when(slot == s)
        def _(s=s):
            for cp in copies(i, s):
                cp.wait()
            rt = rt_ref[...]
            w1 = rt[:, 2:3]
            w2 = rt[:, 3:4]
            x2 = x1_ref[...] + w1 * buf[s, 0] + w2 * buf[s, 1]
            x2_ref[...] = x2
            h_ref[...] = _rms_scale(x2, g_ref[...]).astype(h_ref.dtype)


def _combine(pos_flat, x1, route, g_all, gl, ys, h_dtype):
    T, D = x1.shape
    tk = 128
    R = route.shape[1]
    if g_all.ndim == 3:
        g_spec = pl.BlockSpec((None, 1, D), lambda i, pos: (gl, 0, 0))
    else:
        g_spec = pl.BlockSpec((1, D), lambda i, pos: (0, 0))
    return pl.pallas_call(
        functools.partial(_combine_kernel, T=T, tk=tk),
        out_shape=[jax.ShapeDtypeStruct((T, D), F32), jax.ShapeDtypeStruct((T, D), h_dtype)],
        grid_spec=pltpu.PrefetchScalarGridSpec(
            num_scalar_prefetch=1,
            grid=(T // tk,),
            in_specs=[pl.BlockSpec((tk, D), lambda i, pos: (i, 0)),
                      pl.BlockSpec((tk, R), lambda i, pos: (i, 0)),
                      g_spec,
                      pl.BlockSpec(memory_space=pl.ANY)],
            out_specs=[pl.BlockSpec((tk, D), lambda i, pos: (i, 0)),
                       pl.BlockSpec((tk, D), lambda i, pos: (i, 0))],
            scratch_shapes=[pltpu.VMEM((2, 2, tk, D), F32), pltpu.SemaphoreType.DMA((2,))],
        ),
        compiler_params=_cparams(1),
        name="combine",
    )(pos_flat, x1, route, g_all, ys)


def kernel(x_prompt, x_sample, state_conv, state_pool, state_ssm_re, state_ssm_im,
           norm_mix_g, norm_ffn_g, w_in,
           sgu_ln_g, sgu_ln_b, sgu_w, sgu_b,
           conv_w, conv_b, conv_ln_g, conv_ln_b,
           ssm_a_re, ssm_a_im, ssm_log_dt, ssm_b_re, ssm_b_im, ssm_c_re, ssm_c_im, ssm_d, ssm_w_glu, ssm_b_glu,
           pool_w, pool_scale,
           w_branch, w_out,
           router_group_w, router_group_b, router_expert_w, router_expert_b,
           expert_w_gate, expert_w_up, expert_w_down,
           final_norm_g):
    B, L, D = x_prompt.shape
    DB = x_sample.shape[0]
    depth = w_in.shape[0]
    W = sgu_ln_g.shape[-1]
    G, P = ssm_a_re.shape[1:]
    GP = G * P
    T = B * L + DB
    gate_off = 6 * W
    assert x_sample.shape[1] == 1 and L % TIME_TILE == 0 and (B * L) % DB == 0

    row = lambda a: a.reshape(depth, 1, a.shape[-1])
    small = {
        "sgu_ln_g": row(sgu_ln_g), "sgu_ln_b": row(sgu_ln_b),
        "sgu_bT": jnp.swapaxes(sgu_b, 1, 2),
        "sgu_w00": row(jnp.repeat(sgu_w[:, :, 0, 0], W // SGU_GROUPS, axis=-1)),
        "sgu_b0": row(jnp.repeat(sgu_b[:, :, 0], W // SGU_GROUPS, axis=-1)),
        "conv_w": conv_w, "conv_b": row(conv_b), "conv_ln_g": row(conv_ln_g), "conv_ln_b": row(conv_ln_b),
        "ssm_d": row(ssm_d), "ssm_b_glu": row(ssm_b_glu), "pool_scale": row(pool_scale),
    }
    S = TIME_TILE // SUBLANES
    lamr, lami, pwr, pwi, bbr, bbi, sguw_bf, wglu_bf, poolw_bf = _prep(
        ssm_a_re, ssm_a_im, ssm_log_dt, ssm_b_re, ssm_b_im, sgu_w, ssm_w_glu, pool_w, S)
    prep = {"lamr": lamr, "lami": lami, "pwr": pwr, "pwi": pwi, "bbr": bbr, "bbi": bbi,
            "sguw": sguw_bf, "wglu": wglu_bf, "poolw": poolw_bf}
    eye = jnp.eye(G, dtype=BF16)
    cr_bd = jnp.einsum("lghp,gk->lgpkh", ssm_c_re.astype(BF16), eye).reshape(depth, GP, W)
    ci_bd = jnp.einsum("lghp,gk->lgpkh", ssm_c_im.astype(BF16), eye).reshape(depth, GP, W)

    wr = jnp.concatenate([router_group_w,
                          jnp.transpose(router_expert_w, (0, 2, 1, 3)).reshape(depth, D, N_EXPERTS)], axis=-1)
    wr = jnp.pad(wr, ((0, 0), (0, 0), (0, LANES - wr.shape[-1])))
    rb = jnp.concatenate([router_group_b, router_expert_b.reshape(depth, N_EXPERTS)], axis=-1)
    rb = jnp.pad(rb, ((0, 0), (0, LANES - rb.shape[-1]))).reshape(depth, 1, LANES)

    norm_mix3 = row(norm_mix_g)
    norm_ffn3 = row(norm_ffn_g)
    final3 = final_norm_g.reshape(1, D)

    n_tiles_max = -(-2 * T // EXPERT_TILE) + N_EXPERTS
    n_pad = n_tiles_max * EXPERT_TILE

    x = jnp.concatenate([x_prompt.reshape(B * L, D), x_sample.reshape(DB, D)], axis=0)
    h = _rmsnorm(x, norm_mix3, 0, BF16)

    stc2 = state_conv.reshape(depth, DB, (CONV_K - 1) * W)
    stp2 = state_pool.reshape(depth, DB, POOL_HIST * W)
    hre2 = state_ssm_re.reshape(depth, DB, GP)
    him2 = state_ssm_im.reshape(depth, DB, GP)

    conv_p, conv_s, pool_p, pool_s = [], [], [], []
    sre_p, sim_p, sre_s, sim_s, v_s = [], [], [], [], []
    y = None
    for l in range(depth):
        proj6 = _proj(h, w_in, l, 6 * W)
        o, cp, pp, rp, ip = _mixer_prompt(proj6, B, L, T, l, small, prep, cr_bd, ci_bd)
        o, g_s, vs, xs_re, xs_im = _mixer_sample(proj6, o, stc2, stp2, hre2, him2, DB, T, l,
                                                 small, prep, cr_bd, ci_bd)
        merged = _merge(h, o, w_in, w_branch, l, gate_off)
        x1, hp, logits = _out_proj(merged, x, w_out, norm_ffn3, wr, l)
        route, meta = _route(logits, rb, l, EXPERT_TILE, n_tiles_max)
        pos_flat = jnp.transpose(route[:, 4:6]).astype(I32).reshape(2 * T)
        te = meta[2, :n_tiles_max].astype(I32)
        na = meta[3, :1].astype(I32)
        hs = _dispatch(pos_flat, hp, n_pad)
        ys = _experts(te, na, hs, expert_w_gate, expert_w_up, expert_w_down, l, EXPERT_TILE, n_tiles_max)
        if l + 1 < depth:
            x, h = _combine(pos_flat, x1, route, norm_mix3, l + 1, ys, BF16)
        else:
            x, y = _combine(pos_flat, x1, route, final3, 0, ys, F32)

        conv_p.append(cp)
        pool_p.append(pp)
        sre_p.append(rp.reshape(B, G, P))
        sim_p.append(ip.reshape(B, G, P))
        conv_s.append(jnp.concatenate([state_conv[l][:, 1:], g_s[:, None, :]], axis=1))
        d_in_s = proj6[B * L:, 5 * W:6 * W]
        pool_s.append(jnp.concatenate([state_pool[l][:, 1:], d_in_s[:, None, :]], axis=1))
        sre_s.append(xs_re.reshape(DB, G, P))
        sim_s.append(xs_im.reshape(DB, G, P))
        v_s.append(vs[:, None, :])

    y_prompt = y[:B * L].reshape(B, L, D)
    y_sample = y[B * L:].reshape(DB, 1, D)
    return (y_prompt, y_sample, jnp.stack(conv_p), jnp.stack(conv_s), jnp.stack(pool_p), jnp.stack(pool_s),
            jnp.stack(sre_p), jnp.stack(sim_p), jnp.stack(sre_s), jnp.stack(sim_s), jnp.stack(v_s))
```

```python
import functools
import math

import jax
import jax.numpy as jnp
from jax import lax
from jax.experimental import pallas as pl
from jax.experimental.pallas import tpu as pltpu

F32 = jnp.float32
BF16 = jnp.bfloat16
I32 = jnp.int32
U32 = jnp.uint32

EPS = 1e-6
CHUNK = 128
SGU_GROUPS = 4
CONV_K = 31
POOL_WINDOWS = (2, 4, 8, 16)
POOL_HIST = 15
SSM_H = 16
SSM_P = 64
N_GROUPS = 4
EXP_PER_GROUP = 4
N_EXPERTS = 16
PAST_LEN = 16384

LANES = 128
SUBLANES = 8
VMEM_LIMIT = 60 * 1024 * 1024

EXPERT_TILE = 256
TIME_TILE = 256
SCAN_COLS = 512


def _cparams(n_axes):
    return pltpu.CompilerParams(dimension_semantics=("arbitrary",) * n_axes,
                                vmem_limit_bytes=VMEM_LIMIT)


def _gelu(x):
    c = math.sqrt(2.0 / math.pi)
    return 0.5 * x * (1.0 + jnp.tanh(c * (x + 0.044715 * (x * x * x))))


def _sigmoid(x):
    return 0.5 * jnp.tanh(0.5 * x) + 0.5


def _layernorm(x, g, b):
    xc = x - jnp.mean(x, axis=-1, keepdims=True)
    var = jnp.mean(xc * xc, axis=-1, keepdims=True)
    return xc * lax.rsqrt(var + EPS) * g + b


def _rms_scale(x, g):
    return x * lax.rsqrt(jnp.mean(x * x, axis=-1, keepdims=True) + EPS) * g


def _dot(a, b):
    return jnp.dot(a, b, preferred_element_type=F32)


def _cast_rows(src_ref, dst_ref, chunk):
    rows = src_ref.shape[0]

    def body(i, c):
        r = pl.multiple_of(i * chunk, chunk)
        dst_ref[pl.ds(r, chunk), :] = src_ref[pl.ds(r, chunk), :].astype(dst_ref.dtype)
        return c

    lax.fori_loop(0, rows // chunk, body, 0)


def _layer_spec(arr, l):
    nd = arr.ndim
    return pl.BlockSpec((None,) + tuple(arr.shape[1:]), lambda *_: (l,) + (0,) * (nd - 1))


def _pick(n, cands):
    for c in cands:
        if n % c == 0:
            return c
    raise ValueError(f"no tile for {n}")


def _rmsnorm_kernel(x_ref, g_ref, o_ref):
    o_ref[...] = _rms_scale(x_ref[...], g_ref[...]).astype(o_ref.dtype)


def _rmsnorm(x, g_all, l, out_dtype):
    T, D = x.shape
    tm = _pick(T, (640, 512, 384, 256, 128))
    return pl.pallas_call(
        _rmsnorm_kernel,
        out_shape=jax.ShapeDtypeStruct((T, D), out_dtype),
        grid=(T // tm,),
        in_specs=[pl.BlockSpec((tm, D), lambda m: (m, 0)), _layer_spec(g_all, l)],
        out_specs=pl.BlockSpec((tm, D), lambda m: (m, 0)),
        compiler_params=_cparams(1),
        name="rmsnorm",
    )(x, g_all)


def _prep_kernel(are, aim, ldt, bre, bim, sguw, wglu, poolw,
                 lamr_o, lami_o, pwr_o, pwi_o, bbr_o, bbi_o, sguw_o, wglu_o, poolw_o, *, S):
    a_re = are[...]
    a_im = aim[...]
    dt = jnp.exp(ldt[...])
    mag = jnp.exp(a_re * dt)
    lbr = mag * jnp.cos(a_im * dt)
    lbi = mag * jnp.sin(a_im * dt)
    den = a_re * a_re + a_im * a_im
    nr = lbr - 1.0
    kr = (nr * a_re + lbi * a_im) / den
    ki = (lbi * a_re - nr * a_im) / den
    lamr_o[...] = lbr
    lami_o[...] = lbi
    gp = a_re.shape[-1]

    rows = bre.shape[0]
    rc = 64

    def body(i, c):
        r = pl.multiple_of(i * rc, rc)
        br = bre[pl.ds(r, rc), :]
        bi = bim[pl.ds(r, rc), :]
        bbr_o[pl.ds(r, rc), :] = (kr * br - ki * bi).astype(BF16)
        bbi_o[pl.ds(r, rc), :] = (kr * bi + ki * br).astype(BF16)
        return c

    lax.fori_loop(0, rows // rc, body, 0)

    pr, pi = lbr, lbi
    for s in range(S):
        pwr_o[SUBLANES * s:SUBLANES * (s + 1), :] = jnp.broadcast_to(pr, (SUBLANES, gp))
        pwi_o[SUBLANES * s:SUBLANES * (s + 1), :] = jnp.broadcast_to(pi, (SUBLANES, gp))
        pr, pi = pr * lbr - pi * lbi, pr * lbi + pi * lbr

    t_i = lax.broadcasted_iota(I32, (CHUNK, CHUNK), 0)
    s_i = lax.broadcasted_iota(I32, (CHUNK, CHUNK), 1)
    for g in range(SGU_GROUPS):
        sguw_o[g] = jnp.where(t_i >= s_i, sguw[g], 0.0).astype(BF16)
    wglu_o[...] = wglu[...].astype(BF16)
    for g in range(len(POOL_WINDOWS)):
        poolw_o[g] = poolw[g].astype(BF16)


def _prep(a_re, a_im, log_dt, b_re, b_im, sgu_w, w_glu, pool_w, S):
    L, G, P = a_re.shape
    H = b_re.shape[-1]
    GP, W = G * P, G * H
    eye = jnp.eye(G, dtype=F32)
    bre_bd = jnp.einsum("lgph,gk->lghkp", b_re, eye).reshape(L, W, GP)
    bim_bd = jnp.einsum("lgph,gk->lghkp", b_im, eye).reshape(L, W, GP)
    are2 = a_re.reshape(L, 1, GP)
    aim2 = a_im.reshape(L, 1, GP)
    ldt2 = jnp.repeat(log_dt, P, axis=-1).reshape(L, 1, GP)

    def lspec(shape):
        nd = len(shape)
        return pl.BlockSpec((None,) + tuple(shape[1:]), lambda l: (l,) + (0,) * (nd - 1))

    ins = [are2, aim2, ldt2, bre_bd, bim_bd, sgu_w, w_glu, pool_w]
    out_shapes = [
        jax.ShapeDtypeStruct((L, 1, GP), F32), jax.ShapeDtypeStruct((L, 1, GP), F32),
        jax.ShapeDtypeStruct((L, SUBLANES * S, GP), F32), jax.ShapeDtypeStruct((L, SUBLANES * S, GP), F32),
        jax.ShapeDtypeStruct((L, W, GP), BF16), jax.ShapeDtypeStruct((L, W, GP), BF16),
        jax.ShapeDtypeStruct(sgu_w.shape, BF16), jax.ShapeDtypeStruct(w_glu.shape, BF16),
        jax.ShapeDtypeStruct(pool_w.shape, BF16),
    ]
    return pl.pallas_call(
        functools.partial(_prep_kernel, S=S),
        out_shape=out_shapes,
        grid=(L,),
        in_specs=[lspec(a.shape) for a in ins],
        out_specs=[lspec(o.shape) for o in out_shapes],
        compiler_params=_cparams(1),
        name="ssm_prep",
    )(*ins)


def _proj_kernel(h_ref, w_ref, o_ref, wbf_ref):
    @pl.when(pl.program_id(1) == 0)
    def _():
        _cast_rows(w_ref, wbf_ref, 256)

    o_ref[...] = _dot(h_ref[...], wbf_ref[...])


def _proj(h, w_in, l, n_cols):
    T, D = h.shape
    tm = _pick(T, (640, 512, 384, 256, 128))
    tn = 1024
    return pl.pallas_call(
        _proj_kernel,
        out_shape=jax.ShapeDtypeStruct((T, n_cols), F32),
        grid=(n_cols // tn, T // tm),
        in_specs=[pl.BlockSpec((tm, D), lambda n, m: (m, 0)),
                  pl.BlockSpec((None, D, tn), lambda n, m: (l, 0, n))],
        out_specs=pl.BlockSpec((tm, tn), lambda n, m: (m, n)),
        scratch_shapes=[pltpu.VMEM((D, tn), BF16)],
        compiler_params=_cparams(2),
        name="in_proj",
    )(h, w_in)


def _mixer_prompt_kernel(p_ref, lng, lnb, sguw, sgub, cw, cb, clg, clb,
                         lamr, lami, pwr, pwi, bbr, bbi, cr, ci, sd, wglu, bglu, poolw, pscale,
                         o_ref, conv_o, pool_o, sre_o, sim_o,
                         gbuf, gsh, pbuf, xr, xi, car_re, car_im, fin_re, fin_im, cm_re, cm_im,
                         *, tt, S, W, GP):
    j = pl.program_id(1)
    nt = pl.num_programs(1)
    GH = 32
    PH = 16

    @pl.when(j == 0)
    def _():
        gbuf[0:GH, :] = jnp.zeros((GH, W), F32)
        pbuf[0:PH, :] = jnp.zeros((PH, W), F32)
        car_re[...] = jnp.zeros_like(car_re)
        car_im[...] = jnp.zeros_like(car_im)

    for c in range(tt // CHUNK):
        rows = slice(c * CHUNK, (c + 1) * CHUNK)
        u = _gelu(p_ref[rows, 0:W])
        v = _layernorm(_gelu(p_ref[rows, W:2 * W]), lng[...], lnb[...])
        vb = v.astype(BF16)
        gw = W // SGU_GROUPS
        for g in range(SGU_GROUPS):
            cols = slice(g * gw, (g + 1) * gw)
            mixed = _dot(sguw[g], vb[:, cols]) + sgub[:, g:g + 1]
            o_ref[rows, cols] = (u[:, cols] * mixed).astype(BF16)

    rc = 64
    for c in range(tt // rc):
        rows = slice(c * rc, (c + 1) * rc)
        gbuf[GH + c * rc:GH + (c + 1) * rc, :] = p_ref[rows, 2 * W:3 * W] * _sigmoid(p_ref[rows, 3 * W:4 * W])
    sh_rows = gsh.shape[1]
    for b in range(1, SUBLANES):
        gsh[b] = gbuf[b:b + sh_rows, :]
    rc = 32
    base = GH - (CONV_K - 1)
    for c in range(tt // rc):
        acc = jnp.zeros((rc, W), F32)
        for k in range(CONV_K):
            r0 = base + c * rc + k
            b, a0 = r0 % SUBLANES, r0 - r0 % SUBLANES
            rows = gbuf[a0:a0 + rc, :] if b == 0 else gsh[b, a0:a0 + rc, :]
            acc = acc + cw[k:k + 1, :] * rows
        y = _layernorm(acc + cb[...], clg[...], clb[...])
        o_ref[c * rc:(c + 1) * rc, W:2 * W] = (y * _sigmoid(y)).astype(BF16)
    gbuf[0:GH, :] = gbuf[tt:tt + GH, :]

    @pl.when(j == nt - 1)
    def _():
        conv_o[...] = gbuf[GH - (CONV_K - 1):GH, :]

    uc = p_ref[:, 4 * W:5 * W]
    r_i = lax.broadcasted_iota(I32, (tt, tt), 0)
    c_i = lax.broadcasted_iota(I32, (tt, tt), 1)
    perm = jnp.where(((r_i & (SUBLANES - 1)) * S + (r_i >> 3)) == c_i, 1.0, 0.0).astype(BF16)
    up = _dot(perm, uc.astype(BF16)).astype(BF16)
    xr[...] = _dot(up, bbr[...])
    xi[...] = _dot(up, bbi[...])

    for cbi in range(GP // SCAN_COLS):
        cols = slice(cbi * SCAN_COLS, (cbi + 1) * SCAN_COLS)
        lr = jnp.broadcast_to(lamr[:, cols], (SUBLANES, SCAN_COLS))
        li = jnp.broadcast_to(lami[:, cols], (SUBLANES, SCAN_COLS))

        def step(s, carry, cols=cols, lr=lr, li=li):
            sr, si = carry
            r0 = pl.multiple_of(s * SUBLANES, SUBLANES)
            nr = lr * sr - li * si + xr[pl.ds(r0, SUBLANES), cols]
            ni = lr * si + li * sr + xi[pl.ds(r0, SUBLANES), cols]
            xr[pl.ds(r0, SUBLANES), cols] = nr
            xi[pl.ds(r0, SUBLANES), cols] = ni
            return nr, ni

        z = jnp.zeros((SUBLANES, SCAN_COLS), F32)
        fr, fi = lax.fori_loop(0, S, step, (z, z))
        fin_re[:, cols] = fr
        fin_im[:, cols] = fi

    lsr = pwr[SUBLANES * (S - 1):SUBLANES * (S - 1) + 1, :]
    lsi = pwi[SUBLANES * (S - 1):SUBLANES * (S - 1) + 1, :]
    c_r = car_re[0:1, :]
    c_im = car_im[0:1, :]
    cm_re[0:1, :] = c_r
    cm_im[0:1, :] = c_im
    for q in range(1, SUBLANES):
        f_r = fin_re[q - 1:q, :]
        f_i = fin_im[q - 1:q, :]
        c_r, c_im = f_r + lsr * c_r - lsi * c_im, f_i + lsr * c_im + lsi * c_r
        cm_re[q:q + 1, :] = c_r
        cm_im[q:q + 1, :] = c_im
    n_r = fin_re[SUBLANES - 1:SUBLANES, :] + lsr * c_r - lsi * c_im
    n_i = fin_im[SUBLANES - 1:SUBLANES, :] + lsr * c_im + lsi * c_r
    car_re[0:1, :] = n_r
    car_im[0:1, :] = n_i

    @pl.when(j == nt - 1)
    def _():
        sre_o[...] = n_r
        sim_o[...] = n_i

    for cbi in range(GP // SCAN_COLS):
        cols = slice(cbi * SCAN_COLS, (cbi + 1) * SCAN_COLS)
        mr = cm_re[:, cols]
        mi = cm_im[:, cols]

        def fix(s, c, cols=cols, mr=mr, mi=mi):
            r0 = pl.multiple_of(s * SUBLANES, SUBLANES)
            pr = pwr[pl.ds(r0, SUBLANES), cols]
            pi = pwi[pl.ds(r0, SUBLANES), cols]
            xr[pl.ds(r0, SUBLANES), cols] = xr[pl.ds(r0, SUBLANES), cols] + (pr * mr - pi * mi)
            xi[pl.ds(r0, SUBLANES), cols] = xi[pl.ds(r0, SUBLANES), cols] + (pr * mi + pi * mr)
            return c

        lax.fori_loop(0, S, fix, 0)

    yp = _dot(xr[...].astype(BF16), cr[...]) - _dot(xi[...].astype(BF16), ci[...])
    unperm = jnp.where(((c_i & (SUBLANES - 1)) * S + (c_i >> 3)) == r_i, 1.0, 0.0).astype(BF16)
    y1 = yp.astype(BF16)
    r1 = yp - y1.astype(F32)
    y2 = r1.astype(BF16)
    y3 = (r1 - y2.astype(F32)).astype(BF16)
    y = _dot(unperm, y1) + _dot(unperm, y2) + _dot(unperm, y3)
    z = _gelu(y + sd[...] * uc)
    gl = _dot(z.astype(BF16), wglu[...]) + bglu[...]
    o_ref[:, 2 * W:3 * W] = (z * _sigmoid(gl)).astype(BF16)

    xd = p_ref[:, 5 * W:6 * W]
    pbuf[PH:PH + tt, :] = xd
    gw = W // len(POOL_WINDOWS)
    pos = (j * tt + lax.broadcasted_iota(I32, (tt, 1), 0) + 1).astype(F32)
    for gi, win in enumerate(POOL_WINDOWS):
        cols = slice(gi * gw, (gi + 1) * gw)
        s = xd[:, cols]
        for i in range(1, win):
            s = s + pbuf[PH - i:PH - i + tt, cols]
        cnt = jnp.minimum(pos, float(win))
        pooled = s / cnt - xd[:, cols]
        mixed = _dot(pooled.astype(BF16), poolw[gi]) * pscale[:, cols]
        o_ref[:, 3 * W + gi * gw:3 * W + (gi + 1) * gw] = mixed.astype(BF16)
    pbuf[0:PH, :] = pbuf[tt:tt + PH, :]

    @pl.when(j == nt - 1)
    def _():
        pool_o[...] = pbuf[PH - POOL_HIST:PH, :]


def _mixer_prompt(proj6, B, L, T, l, small, prep, cr, ci):
    W = small["sgu_ln_g"].shape[-1]
    GP = prep["lamr"].shape[-1]
    D4 = 4 * W
    tt = TIME_TILE
    S = tt // SUBLANES
    nt = L // tt
    names_small = ["sgu_ln_g", "sgu_ln_b"]
    ins = [proj6,
           small["sgu_ln_g"], small["sgu_ln_b"], prep["sguw"], small["sgu_bT"],
           small["conv_w"], small["conv_b"], small["conv_ln_g"], small["conv_ln_b"],
           prep["lamr"], prep["lami"], prep["pwr"], prep["pwi"], prep["bbr"], prep["bbi"], cr, ci,
           small["ssm_d"], prep["wglu"], small["ssm_b_glu"], prep["poolw"], small["pool_scale"]]
    del names_small
    in_specs = [pl.BlockSpec((tt, proj6.shape[1]), lambda b, j: (b * nt + j, 0))]
    in_specs += [_layer_spec(a, l) for a in ins[1:]]
    out_shapes = [
        jax.ShapeDtypeStruct((T, D4), BF16),
        jax.ShapeDtypeStruct((B, CONV_K - 1, W), F32),
        jax.ShapeDtypeStruct((B, POOL_HIST, W), F32),
        jax.ShapeDtypeStruct((B, 1, GP), F32),
        jax.ShapeDtypeStruct((B, 1, GP), F32),
    ]
    out_specs = [
        pl.BlockSpec((tt, D4), lambda b, j: (b * nt + j, 0)),
        pl.BlockSpec((None, CONV_K - 1, W), lambda b, j: (b, 0, 0)),
        pl.BlockSpec((None, POOL_HIST, W), lambda b, j: (b, 0, 0)),
        pl.BlockSpec((None, 1, GP), lambda b, j: (b, 0, 0)),
        pl.BlockSpec((None, 1, GP), lambda b, j: (b, 0, 0)),
    ]
    scratch = [
        pltpu.VMEM((32 + tt, W), F32), pltpu.VMEM((SUBLANES, 32 + tt - SUBLANES, W), F32),
        pltpu.VMEM((16 + tt, W), F32),
        pltpu.VMEM((tt, GP), F32), pltpu.VMEM((tt, GP), F32),
        pltpu.VMEM((SUBLANES, GP), F32), pltpu.VMEM((SUBLANES, GP), F32),
        pltpu.VMEM((SUBLANES, GP), F32), pltpu.VMEM((SUBLANES, GP), F32),
        pltpu.VMEM((SUBLANES, GP), F32), pltpu.VMEM((SUBLANES, GP), F32),
    ]
    return pl.pallas_call(
        functools.partial(_mixer_prompt_kernel, tt=tt, S=S, W=W, GP=GP),
        out_shape=out_shapes,
        grid=(B, nt),
        in_specs=in_specs,
        out_specs=out_specs,
        scratch_shapes=scratch,
        compiler_params=_cparams(2),
        name="mixer_prompt",
    )(*ins)


def _mixer_sample_kernel(p_ref, stc, stp, hre, him, lng, lnb, w00, b0, cw, cb, clg, clb,
                         lamr, lami, bbr, bbi, cr, ci, sd, wglu, bglu, poolw, pscale, o_in,
                         o_ref, g_o, v_o, xre_o, xim_o, *, W):
    del o_in
    u = _gelu(p_ref[:, 0:W])
    v = _layernorm(_gelu(p_ref[:, W:2 * W]), lng[...], lnb[...])
    v_o[...] = v
    o_ref[:, 0:W] = (u * (w00[...] * v + b0[...])).astype(BF16)

    g = p_ref[:, 2 * W:3 * W] * _sigmoid(p_ref[:, 3 * W:4 * W])
    g_o[...] = g
    acc = cw[CONV_K - 1:CONV_K, :] * g
    for k in range(CONV_K - 1):
        acc = acc + cw[k:k + 1, :] * stc[:, k * W:(k + 1) * W]
    y = _layernorm(acc + cb[...], clg[...], clb[...])
    o_ref[:, W:2 * W] = (y * _sigmoid(y)).astype(BF16)

    uc = p_ref[:, 4 * W:5 * W]
    ub = uc.astype(BF16)
    h_r = hre[...]
    h_i = him[...]
    l_r = lamr[...]
    l_i = lami[...]
    x_r = l_r * h_r - l_i * h_i + _dot(ub, bbr[...])
    x_i = l_r * h_i + l_i * h_r + _dot(ub, bbi[...])
    xre_o[...] = x_r
    xim_o[...] = x_i
    yv = _dot(x_r.astype(BF16), cr[...]) - _dot(x_i.astype(BF16), ci[...]) + sd[...] * uc
    z = _gelu(yv)
    gl = _dot(z.astype(BF16), wglu[...]) + bglu[...]
    o_ref[:, 2 * W:3 * W] = (z * _sigmoid(gl)).astype(BF16)

    xd = p_ref[:, 5 * W:6 * W]
    gw = W // len(POOL_WINDOWS)
    for gi, win in enumerate(POOL_WINDOWS):
        cols = slice(gi * gw, (gi + 1) * gw)
        s = xd[:, cols]
        for i in range(1, win):
            r = POOL_HIST - i
            s = s + stp[:, r * W + gi * gw:r * W + (gi + 1) * gw]
        cnt = float(min(PAST_LEN + 1, win))
        pooled = s / cnt - xd[:, cols]
        mixed = _dot(pooled.astype(BF16), poolw[gi]) * pscale[:, cols]
        o_ref[:, 3 * W + gi * gw:3 * W + (gi + 1) * gw] = mixed.astype(BF16)


def _mixer_sample(proj6, o_prev, stc2, stp2, hre2, him2, DB, T, l, small, prep, cr, ci):
    W = small["sgu_ln_g"].shape[-1]
    GP = prep["lamr"].shape[-1]
    D4 = 4 * W
    blk = (T - DB) // DB
    ins = [proj6, stc2, stp2, hre2, him2,
           small["sgu_ln_g"], small["sgu_ln_b"], small["sgu_w00"], small["sgu_b0"],
           small["conv_w"], small["conv_b"], small["conv_ln_g"], small["conv_ln_b"],
           prep["lamr"], prep["lami"], prep["bbr"], prep["bbi"], cr, ci,
           small["ssm_d"], prep["wglu"], small["ssm_b_glu"], prep["poolw"], small["pool_scale"], o_prev]
    in_specs = [pl.BlockSpec((DB, proj6.shape[1]), lambda i: (blk, 0))]
    in_specs += [_layer_spec(a, l) for a in ins[1:-1]]
    in_specs += [pl.BlockSpec(memory_space=pl.ANY)]
    out_shapes = [
        jax.ShapeDtypeStruct((T, D4), BF16),
        jax.ShapeDtypeStruct((DB, W), F32), jax.ShapeDtypeStruct((DB, W), F32),
        jax.ShapeDtypeStruct((DB, GP), F32), jax.ShapeDtypeStruct((DB, GP), F32),
    ]
    out_specs = [
        pl.BlockSpec((DB, D4), lambda i: (blk, 0)),
        pl.BlockSpec((DB, W), lambda i: (0, 0)), pl.BlockSpec((DB, W), lambda i: (0, 0)),
        pl.BlockSpec((DB, GP), lambda i: (0, 0)), pl.BlockSpec((DB, GP), lambda i: (0, 0)),
    ]
    return pl.pallas_call(
        functools.partial(_mixer_sample_kernel, W=W),
        out_shape=out_shapes,
        grid=(1,),
        in_specs=in_specs,
        out_specs=out_specs,
        input_output_aliases={len(ins) - 1: 0},
        compiler_params=_cparams(1),
        name="mixer_sample",
    )(*ins)


def _merge_kernel(h_ref, o_ref, wg0, wg1, wg2, wg3, wb_ref, m_ref, wgbf, wbbf, *, W):
    wgs = (wg0, wg1, wg2, wg3)

    @pl.when(pl.program_id(1) == 0)
    def _():
        for b in range(4):
            _cast_rows(wgs[b], wgbf.at[b], 256)
            wbbf[b] = wb_ref[b].astype(BF16)

    h = h_ref[...]
    acc = None
    for b in range(4):
        gate = _sigmoid(_dot(h, wgbf[b]))
        term = gate * _dot(o_ref[:, b * W:(b + 1) * W], wbbf[b])
        acc = term if acc is None else acc + term
    m_ref[...] = acc.astype(BF16)


def _merge(h, o, w_in, w_branch, l, gate_off):
    T, D = h.shape
    W = w_branch.shape[2]
    tm = _pick(T, (640, 512, 384, 256, 128))
    tn = 256
    nb = D // tn

    def gate_spec(b):
        blk0 = (gate_off + b * D) // tn
        return pl.BlockSpec((None, D, tn), lambda n, m: (l, 0, blk0 + n))

    return pl.pallas_call(
        functools.partial(_merge_kernel, W=W),
        out_shape=jax.ShapeDtypeStruct((T, D), BF16),
        grid=(nb, T // tm),
        in_specs=[pl.BlockSpec((tm, D), lambda n, m: (m, 0)),
                  pl.BlockSpec((tm, 4 * W), lambda n, m: (m, 0)),
                  gate_spec(0), gate_spec(1), gate_spec(2), gate_spec(3),
                  pl.BlockSpec((None, 4, W, tn), lambda n, m: (l, 0, 0, n))],
        out_specs=pl.BlockSpec((tm, tn), lambda n, m: (m, n)),
        scratch_shapes=[pltpu.VMEM((4, D, tn), BF16), pltpu.VMEM((4, W, tn), BF16)],
        compiler_params=_cparams(2),
        name="gated_merge",
    )(h, o, w_in, w_in, w_in, w_in, w_branch)


def _out_kernel(m_ref, x_ref, w_ref, g_ref, wr_ref, x1_ref, hp_ref, lg_ref, wbf, wr2):
    R = wr_ref.shape[-1]

    @pl.when(pl.program_id(0) == 0)
    def _():
        _cast_rows(w_ref, wbf, 256)
        wr = wr_ref[...]
        hi = wr.astype(BF16)
        wr2[:, 0:R] = hi
        wr2[:, R:2 * R] = (wr - hi.astype(F32)).astype(BF16)

    x1 = x_ref[...] + _dot(m_ref[...], wbf[...])
    x1_ref[...] = x1
    h = _rms_scale(x1, g_ref[...])
    hb = h.astype(BF16)
    hl = (h - hb.astype(F32)).astype(BF16)
    both = _dot(hb, wr2[...])
    lg_ref[...] = both[:, 0:R] + both[:, R:2 * R] + _dot(hl, wr2[:, 0:R])
    bits = lax.bitcast_convert_type(hb.astype(F32), U32)
    half = bits.shape[1] // 2
    hp_ref[...] = (bits[:, :half] >> 16) | (bits[:, half:] & jnp.uint32(0xFFFF0000))


def _out_proj(merged, x, w_out, g_all, wr_all, l):
    T, D = x.shape
    tm = _pick(T, (320, 256, 128))
    R = wr_all.shape[-1]
    return pl.pallas_call(
        _out_kernel,
        out_shape=[jax.ShapeDtypeStruct((T, D), F32), jax.ShapeDtypeStruct((T, D // 2), U32),
                   jax.ShapeDtypeStruct((T, R), F32)],
        grid=(T // tm,),
        in_specs=[pl.BlockSpec((tm, D), lambda m: (m, 0)),
                  pl.BlockSpec((tm, D), lambda m: (m, 0)),
                  pl.BlockSpec((None, D, D), lambda m: (l, 0, 0), pipeline_mode=pl.Buffered(1)),
                  _layer_spec(g_all, l), _layer_spec(wr_all, l)],
        out_specs=[pl.BlockSpec((tm, D), lambda m: (m, 0)),
                   pl.BlockSpec((tm, D // 2), lambda m: (m, 0)),
                   pl.BlockSpec((tm, R), lambda m: (m, 0))],
        scratch_shapes=[pltpu.VMEM((D, D), BF16), pltpu.VMEM((D, 2 * R), BF16)],
        compiler_params=_cparams(1),
        name="out_proj",
    )(merged, x, w_out, g_all, wr_all)


def _route_kernel(lg_ref, bias_ref, rt_ref, meta_ref, cnt, off, *, tm, tile, n_tiles_max):
    p = pl.program_id(0)
    m = pl.program_id(1)
    lane = lax.broadcasted_iota(I32, (tm, LANES), 1).astype(F32)
    neg = jnp.float32(-jnp.inf)
    big = jnp.float32(1e9)

    @pl.when((p == 0) & (m == 0))
    def _():
        cnt[...] = jnp.zeros_like(cnt)

    lg = lg_ref[...] + bias_ref[...]
    is_g = lane < N_GROUPS
    gl = jnp.where(is_g, lg, neg)
    gmax = jnp.max(gl, axis=-1, keepdims=True)
    gidx = jnp.min(jnp.where(gl == gmax, lane, big), axis=-1, keepdims=True)
    gsum = jnp.sum(jnp.where(is_g, jnp.exp(gl - gmax), 0.0), axis=-1, keepdims=True)
    g_w = 1.0 / gsum
    lo = N_GROUPS + EXP_PER_GROUP * gidx
    in_grp = (lane >= lo) & (lane < lo + EXP_PER_GROUP)
    el = jnp.where(in_grp, lg, neg)
    v1 = jnp.max(el, axis=-1, keepdims=True)
    i1 = jnp.min(jnp.where(el == v1, lane, big), axis=-1, keepdims=True)
    el2 = jnp.where(lane == i1, neg, el)
    v2 = jnp.max(el2, axis=-1, keepdims=True)
    i2 = jnp.min(jnp.where(el2 == v2, lane, big), axis=-1, keepdims=True)
    e2x = jnp.exp(v2 - v1)
    w1 = g_w / (1.0 + e2x)
    w2 = g_w * e2x / (1.0 + e2x)
    e1 = i1 - N_GROUPS
    e2 = i2 - N_GROUPS
    a1 = jnp.where(lane == e1, 1.0, 0.0)
    a2 = jnp.where(lane == e2, 1.0, 0.0)
    a = a1 + a2

    @pl.when(p == 0)
    def _():
        cnt[0:1, :] = cnt[0:1, :] + jnp.sum(a, axis=0, keepdims=True)

    @pl.when((p == 1) & (m == 0))
    def _():
        counts = cnt[0:1, :]
        tiles = jnp.floor((counts + (tile - 1)) * (1.0 / tile))
        tiles8 = jnp.broadcast_to(tiles, (SUBLANES, LANES)).astype(BF16)
        ri = lax.broadcasted_iota(I32, (LANES, LANES), 0)
        ci = lax.broadcasted_iota(I32, (LANES, LANES), 1)
        upper = jnp.where(ri < ci, 1.0, 0.0).astype(BF16)
        toff = _dot(tiles8, upper)[0:1, :]
        off[0:1, :] = toff * tile
        tend = toff + tiles
        lane1 = lax.broadcasted_iota(I32, (1, LANES), 1).astype(F32)
        n_act = jnp.sum(jnp.where(lane1 == N_EXPERTS - 1, tend, 0.0), axis=-1, keepdims=True)
        texp = jnp.zeros((1, LANES), F32)
        for e in range(N_EXPERTS - 1):
            end_e = jnp.sum(jnp.where(lane1 == e, tend, 0.0), axis=-1, keepdims=True)
            texp = texp + jnp.where(jnp.minimum(lane1, n_act - 1.0) >= end_e, 1.0, 0.0)
        meta_ref[0:1, :] = counts
        meta_ref[1:2, :] = off[0:1, :]
        meta_ref[2:3, :] = texp
        meta_ref[3:4, :] = jnp.broadcast_to(n_act, (1, LANES))
        meta_ref[4:8, :] = jnp.zeros((4, LANES), F32)
        cnt[...] = jnp.zeros_like(cnt)

    @pl.when(p == 1)
    def _():
        ri = lax.broadcasted_iota(I32, (tm, tm), 0)
        ci = lax.broadcasted_iota(I32, (tm, tm), 1)
        ltri = jnp.where(ci < ri, 1.0, 0.0).astype(BF16)
        cum = _dot(ltri, a.astype(BF16)) + cnt[0:1, :] + off[0:1, :]
        pos1 = jnp.sum(a1 * cum, axis=-1, keepdims=True)
        pos2 = jnp.sum(a2 * cum, axis=-1, keepdims=True)
        cnt[0:1, :] = cnt[0:1, :] + jnp.sum(a, axis=0, keepdims=True)
        rt = jnp.where(lane == 0, e1, 0.0)
        rt = jnp.where(lane == 1, e2, rt)
        rt = jnp.where(lane == 2, w1, rt)
        rt = jnp.where(lane == 3, w2, rt)
        rt = jnp.where(lane == 4, pos1, rt)
        rt = jnp.where(lane == 5, pos2, rt)
        rt_ref[...] = rt


def _route(logits, bias_all, l, tile, n_tiles_max):
    T, R = logits.shape
    tm = _pick(T, (640, 512, 384, 256, 128))
    return pl.pallas_call(
        functools.partial(_route_kernel, tm=tm, tile=tile, n_tiles_max=n_tiles_max),
        out_shape=[jax.ShapeDtypeStruct((T, R), F32), jax.ShapeDtypeStruct((SUBLANES, LANES), F32)],
        grid=(2, T // tm),
        in_specs=[pl.BlockSpec((tm, R), lambda p, m: (m, 0)), _layer_spec(bias_all, l)],
        out_specs=[pl.BlockSpec((tm, R), lambda p, m: (m * p, 0)),
                   pl.BlockSpec((SUBLANES, LANES), lambda p, m: (0, 0))],
        scratch_shapes=[pltpu.VMEM((SUBLANES, LANES), F32), pltpu.VMEM((SUBLANES, LANES), F32)],
        compiler_params=_cparams(2),
        name="route",
    )(logits, bias_all)


def _dispatch_kernel(pos_ref, hp_ref, hs_in, hs_ref, stage, sem, *, T, tk):
    del hs_in
    i = pl.program_id(0)
    n = pl.num_programs(0)

    def copies(step, slot):
        out = []
        for u in range(tk):
            t = step * tk + u
            for k in range(2):
                p = pos_ref[k * T + t]
                out.append(pltpu.make_async_copy(stage.at[slot, pl.ds(u, 1)], hs_ref.at[pl.ds(p, 1)],
                                                 sem.at[slot]))
        return out

    def wait_slot(s):
        rows = hs_ref.at[pl.ds(0, 2 * tk)]
        pltpu.make_async_copy(rows, rows, sem.at[s]).wait()

    slot = lax.rem(i, 2)
    for s in range(2):
        @pl.when(slot == s)
        def _(s=s):
            stage[s] = hp_ref[...]
            for cp in copies(i, s):
                cp.start()

        @pl.when((slot == 1 - s) & (i > 0))
        def _(s=s):
            wait_slot(s)

        @pl.when((slot == s) & (i == n - 1))
        def _(s=s):
            wait_slot(s)


def _dispatch(pos_flat, hp, n_pad):
    T, Dh = hp.shape
    tk = 128
    hs0 = jnp.zeros((n_pad, Dh), U32)
    return pl.pallas_call(
        functools.partial(_dispatch_kernel, T=T, tk=tk),
        out_shape=jax.ShapeDtypeStruct((n_pad, Dh), U32),
        grid_spec=pltpu.PrefetchScalarGridSpec(
            num_scalar_prefetch=1,
            grid=(T // tk,),
            in_specs=[pl.BlockSpec((tk, Dh), lambda i, pos: (i, 0)), pl.BlockSpec(memory_space=pl.ANY)],
            out_specs=pl.BlockSpec(memory_space=pl.ANY),
            scratch_shapes=[pltpu.VMEM((2, tk, Dh), U32), pltpu.SemaphoreType.DMA((2,))],
        ),
        input_output_aliases={2: 0},
        compiler_params=_cparams(1),
        name="dispatch",
    )(pos_flat, hp, hs0)


def _expert_kernel(te_ref, na_ref, hs_ref, wg_ref, wu_ref, wd_ref, ys_ref,
                   wgst, wust, wdst, wgbf, wubf, wdbf, sem, *, l):
    i = pl.program_id(0)
    na = na_ref[0]
    e = te_ref[i]
    first = (i < na) & ((i == 0) | (e != te_ref[jnp.maximum(i - 1, 0)]))

    def weight_copies(ex):
        return (pltpu.make_async_copy(wg_ref.at[l, ex], wgst, sem),
                pltpu.make_async_copy(wu_ref.at[l, ex], wust, sem),
                pltpu.make_async_copy(wd_ref.at[l, ex], wdst, sem))

    @pl.when(i == 0)
    def _():
        for cp in weight_copies(e):
            cp.start()

    @pl.when(first)
    def _():
        for cp in weight_copies(e):
            cp.wait()
        _cast_rows(wgst, wgbf, 256)
        _cast_rows(wust, wubf, 256)
        _cast_rows(wdst, wdbf, 256)
        j = lax.while_loop(lambda j: (j < na) & (te_ref[jnp.minimum(j, na - 1)] == e), lambda j: j + 1, i + 1)

        @pl.when(j < na)
        def _():
            for cp in weight_copies(te_ref[jnp.minimum(j, na - 1)]):
                cp.start()

    @pl.when(i < na)
    def _():
        w = hs_ref[...]
        half = w.shape[1]
        lo = lax.bitcast_convert_type(w << 16, F32).astype(BF16)
        hi = lax.bitcast_convert_type(w & jnp.uint32(0xFFFF0000), F32).astype(BF16)
        a = _dot(lo, wgbf[0:half, :]) + _dot(hi, wgbf[half:, :])
        b = _dot(lo, wubf[0:half, :]) + _dot(hi, wubf[half:, :])
        hid = (a * _sigmoid(a) * b).astype(BF16)
        ys_ref[...] = _dot(hid, wdbf[...])


def _experts(te, na, hs, wg, wu, wd, l, tile, n_tiles_max):
    n_pad, Dh = hs.shape
    D = 2 * Dh
    F = wg.shape[-1]

    def row_map(i, te_ref, na_ref):
        return (jnp.minimum(i, na_ref[0] - 1), 0)

    any_spec = pl.BlockSpec(memory_space=pl.ANY)
    return pl.pallas_call(
        functools.partial(_expert_kernel, l=l),
        out_shape=jax.ShapeDtypeStruct((n_pad, D), F32),
        grid_spec=pltpu.PrefetchScalarGridSpec(
            num_scalar_prefetch=2,
            grid=(n_tiles_max,),
            in_specs=[pl.BlockSpec((tile, Dh), row_map), any_spec, any_spec, any_spec],
            out_specs=pl.BlockSpec((tile, D), row_map),
            scratch_shapes=[pltpu.VMEM((D, F), F32), pltpu.VMEM((D, F), F32), pltpu.VMEM((F, D), F32),
                            pltpu.VMEM((D, F), BF16), pltpu.VMEM((D, F), BF16), pltpu.VMEM((F, D), BF16),
                            pltpu.SemaphoreType.DMA],
        ),
        compiler_params=_cparams(1),
        name="experts",
    )(te, na, hs, wg, wu, wd)


def _combine_kernel(pos_ref, x1_ref, rt_ref, g_ref, ys_ref, x2_ref, h_ref, buf, sem, *, T, tk):
    i = pl.program_id(0)
    n = pl.num_programs(0)

    def copies(step, slot):
        out = []
        for u in range(tk):
            t = step * tk + u
            for k in range(2):
                p = pos_ref[k * T + t]
                out.append(pltpu.make_async_copy(ys_ref.at[pl.ds(p, 1)], buf.at[slot, k, pl.ds(u, 1)],
                                                 sem.at[slot]))
        return out

    slot = lax.rem(i, 2)

    @pl.when(i == 0)
    def _():
        for cp in copies(0, 0):
            cp.start()

    for s in range(2):
        @pl.when((i + 1 < n) & (slot == 1 - s))
        def _(s=s):
            for cp in copies(i + 1, s):
                cp.start()

    for s in range(2):
        @pl.when(slot == s)
        def _(s=s):
            pltpu.make_async_copy(buf.at[s], buf.at[s], sem.at[s]).wait()
            rt = rt_ref[...]
            w1 = rt[:, 2:3]
            w2 = rt[:, 3:4]
            x2 = x1_ref[...] + w1 * buf[s, 0] + w2 * buf[s, 1]
            x2_ref[...] = x2
            h_ref[...] = _rms_scale(x2, g_ref[...]).astype(h_ref.dtype)


def _combine(pos_flat, x1, route, g_all, gl, ys, h_dtype):
    T, D = x1.shape
    tk = 128
    R = route.shape[1]
    if g_all.ndim == 3:
        g_spec = pl.BlockSpec((None, 1, D), lambda i, pos: (gl, 0, 0))
    else:
        g_spec = pl.BlockSpec((1, D), lambda i, pos: (0, 0))
    return pl.pallas_call(
        functools.partial(_combine_kernel, T=T, tk=tk),
        out_shape=[jax.ShapeDtypeStruct((T, D), F32), jax.ShapeDtypeStruct((T, D), h_dtype)],
        grid_spec=pltpu.PrefetchScalarGridSpec(
            num_scalar_prefetch=1,
            grid=(T // tk,),
            in_specs=[pl.BlockSpec((tk, D), lambda i, pos: (i, 0)),
                      pl.BlockSpec((tk, R), lambda i, pos: (i, 0)),
                      g_spec,
                      pl.BlockSpec(memory_space=pl.ANY)],
            out_specs=[pl.BlockSpec((tk, D), lambda i, pos: (i, 0)),
                       pl.BlockSpec((tk, D), lambda i, pos: (i, 0))],
            scratch_shapes=[pltpu.VMEM((2, 2, tk, D), F32), pltpu.SemaphoreType.DMA((2,))],
        ),
        compiler_params=_cparams(1),
        name="combine",
    )(pos_flat, x1, route, g_all, ys)


def kernel(x_prompt, x_sample, state_conv, state_pool, state_ssm_re, state_ssm_im,
           norm_mix_g, norm_ffn_g, w_in,
           sgu_ln_g, sgu_ln_b, sgu_w, sgu_b,
           conv_w, conv_b, conv_ln_g, conv_ln_b,
           ssm_a_re, ssm_a_im, ssm_log_dt, ssm_b_re, ssm_b_im, ssm_c_re, ssm_c_im, ssm_d, ssm_w_glu, ssm_b_glu,
           pool_w, pool_scale,
           w_branch, w_out,
           router_group_w, router_group_b, router_expert_w, router_expert_b,
           expert_w_gate, expert_w_up, expert_w_down,
           final_norm_g):
    B, L, D = x_prompt.shape
    DB = x_sample.shape[0]
    depth = w_in.shape[0]
    W = sgu_ln_g.shape[-1]
    G, P = ssm_a_re.shape[1:]
    GP = G * P
    T = B * L + DB
    gate_off = 6 * W
    assert x_sample.shape[1] == 1 and L % TIME_TILE == 0 and (B * L) % DB == 0

    row = lambda a: a.reshape(depth, 1, a.shape[-1])
    small = {
        "sgu_ln_g": row(sgu_ln_g), "sgu_ln_b": row(sgu_ln_b),
        "sgu_bT": jnp.swapaxes(sgu_b, 1, 2),
        "sgu_w00": row(jnp.repeat(sgu_w[:, :, 0, 0], W // SGU_GROUPS, axis=-1)),
        "sgu_b0": row(jnp.repeat(sgu_b[:, :, 0], W // SGU_GROUPS, axis=-1)),
        "conv_w": conv_w, "conv_b": row(conv_b), "conv_ln_g": row(conv_ln_g), "conv_ln_b": row(conv_ln_b),
        "ssm_d": row(ssm_d), "ssm_b_glu": row(ssm_b_glu), "pool_scale": row(pool_scale),
    }
    S = TIME_TILE // SUBLANES
    lamr, lami, pwr, pwi, bbr, bbi, sguw_bf, wglu_bf, poolw_bf = _prep(
        ssm_a_re, ssm_a_im, ssm_log_dt, ssm_b_re, ssm_b_im, sgu_w, ssm_w_glu, pool_w, S)
    prep = {"lamr": lamr, "lami": lami, "pwr": pwr, "pwi": pwi, "bbr": bbr, "bbi": bbi,
            "sguw": sguw_bf, "wglu": wglu_bf, "poolw": poolw_bf}
    eye = jnp.eye(G, dtype=BF16)
    cr_bd = jnp.einsum("lghp,gk->lgpkh", ssm_c_re.astype(BF16), eye).reshape(depth, GP, W)
    ci_bd = jnp.einsum("lghp,gk->lgpkh", ssm_c_im.astype(BF16), eye).reshape(depth, GP, W)

    wr = jnp.concatenate([router_group_w,
                          jnp.transpose(router_expert_w, (0, 2, 1, 3)).reshape(depth, D, N_EXPERTS)], axis=-1)
    wr = jnp.pad(wr, ((0, 0), (0, 0), (0, LANES - wr.shape[-1])))
    rb = jnp.concatenate([router_group_b, router_expert_b.reshape(depth, N_EXPERTS)], axis=-1)
    rb = jnp.pad(rb, ((0, 0), (0, LANES - rb.shape[-1]))).reshape(depth, 1, LANES)

    norm_mix3 = row(norm_mix_g)
    norm_ffn3 = row(norm_ffn_g)
    final3 = final_norm_g.reshape(1, D)

    n_tiles_max = -(-2 * T // EXPERT_TILE) + N_EXPERTS
    n_pad = n_tiles_max * EXPERT_TILE

    x = jnp.concatenate([x_prompt.reshape(B * L, D), x_sample.reshape(DB, D)], axis=0)
    h = _rmsnorm(x, norm_mix3, 0, BF16)

    stc2 = state_conv.reshape(depth, DB, (CONV_K - 1) * W)
    stp2 = state_pool.reshape(depth, DB, POOL_HIST * W)
    hre2 = state_ssm_re.reshape(depth, DB, GP)
    him2 = state_ssm_im.reshape(depth, DB, GP)

    conv_p, conv_s, pool_p, pool_s = [], [], [], []
    sre_p, sim_p, sre_s, sim_s, v_s = [], [], [], [], []
    y = None
    for l in range(depth):
        proj6 = _proj(h, w_in, l, 6 * W)
        o, cp, pp, rp, ip = _mixer_prompt(proj6, B, L, T, l, small, prep, cr_bd, ci_bd)
        o, g_s, vs, xs_re, xs_im = _mixer_sample(proj6, o, stc2, stp2, hre2, him2, DB, T, l,
                                                 small, prep, cr_bd, ci_bd)
        merged = _merge(h, o, w_in, w_branch, l, gate_off)
        x1, hp, logits = _out_proj(merged, x, w_out, norm_ffn3, wr, l)
        route, meta = _route(logits, rb, l, EXPERT_TILE, n_tiles_max)
        pos_flat = jnp.transpose(route[:, 4:6]).astype(I32).reshape(2 * T)
        te = meta[2, :n_tiles_max].astype(I32)
        na = meta[3, :1].astype(I32)
        hs = _dispatch(pos_flat, hp, n_pad)
        ys = _experts(te, na, hs, expert_w_gate, expert_w_up, expert_w_down, l, EXPERT_TILE, n_tiles_max)
        if l + 1 < depth:
            x, h = _combine(pos_flat, x1, route, norm_mix3, l + 1, ys, BF16)
        else:
            x, y = _combine(pos_flat, x1, route, final3, 0, ys, F32)

        conv_p.append(cp)
        pool_p.append(pp)
        sre_p.append(rp.reshape(B, G, P))
        sim_p.append(ip.reshape(B, G, P))
        conv_s.append(jnp.concatenate([state_conv[l][:, 1:], g_s[:, None, :]], axis=1))
        d_in_s = proj6[B * L:, 5 * W:6 * W]
        pool_s.append(jnp.concatenate([state_pool[l][:, 1:], d_in_s[:, None, :]], axis=1))
        sre_s.append(xs_re.reshape(DB, G, P))
        sim_s.append(xs_im.reshape(DB, G, P))
        v_s.append(vs[:, None, :])

    y_prompt = y[:B * L].reshape(B, L, D)
    y_sample = y[B * L:].reshape(DB, 1, D)
    return (y_prompt, y_sample, jnp.stack(conv_p), jnp.stack(conv_s), jnp.stack(pool_p), jnp.stack(pool_s),
            jnp.stack(sre_p), jnp.stack(sim_p), jnp.stack(sre_s), jnp.stack(sim_s), jnp.stack(v_s))
```

```python
import functools
import math

import jax
import jax.numpy as jnp
from jax import lax
from jax.experimental import pallas as pl
from jax.experimental.pallas import tpu as pltpu

F32 = jnp.float32
BF16 = jnp.bfloat16
I32 = jnp.int32
U32 = jnp.uint32

EPS = 1e-6
CHUNK = 128
SGU_GROUPS = 4
CONV_K = 31
POOL_WINDOWS = (2, 4, 8, 16)
POOL_HIST = 15
SSM_H = 16
SSM_P = 64
N_GROUPS = 4
EXP_PER_GROUP = 4
N_EXPERTS = 16
PAST_LEN = 16384

LANES = 128
SUBLANES = 8
VMEM_LIMIT = 60 * 1024 * 1024

EXPERT_TILE = 256
TIME_TILE = 256
SCAN_COLS = 512


def _cparams(n_axes):
    return pltpu.CompilerParams(dimension_semantics=("arbitrary",) * n_axes,
                                vmem_limit_bytes=VMEM_LIMIT)


def _gelu(x):
    c = math.sqrt(2.0 / math.pi)
    return 0.5 * x * (1.0 + jnp.tanh(c * (x + 0.044715 * (x * x * x))))


def _sigmoid(x):
    return 0.5 * jnp.tanh(0.5 * x) + 0.5


def _layernorm(x, g, b):
    xc = x - jnp.mean(x, axis=-1, keepdims=True)
    var = jnp.mean(xc * xc, axis=-1, keepdims=True)
    return xc * lax.rsqrt(var + EPS) * g + b


def _rms_scale(x, g):
    return x * lax.rsqrt(jnp.mean(x * x, axis=-1, keepdims=True) + EPS) * g


def _dot(a, b):
    return jnp.dot(a, b, preferred_element_type=F32)


def _cast_rows(src_ref, dst_ref, chunk):
    rows = src_ref.shape[0]

    def body(i, c):
        r = pl.multiple_of(i * chunk, chunk)
        dst_ref[pl.ds(r, chunk), :] = src_ref[pl.ds(r, chunk), :].astype(dst_ref.dtype)
        return c

    lax.fori_loop(0, rows // chunk, body, 0)


def _layer_spec(arr, l, single=False):
    nd = arr.ndim
    mode = {"pipeline_mode": pl.Buffered(1)} if single else {}
    return pl.BlockSpec((None,) + tuple(arr.shape[1:]), lambda *_: (l,) + (0,) * (nd - 1), **mode)


def _pick(n, cands):
    for c in cands:
        if n % c == 0:
            return c
    raise ValueError(f"no tile for {n}")


def _rmsnorm_kernel(x_ref, g_ref, o_ref):
    o_ref[...] = _rms_scale(x_ref[...], g_ref[...]).astype(o_ref.dtype)


def _rmsnorm(x, g_all, l, out_dtype):
    T, D = x.shape
    tm = _pick(T, (640, 512, 384, 256, 128))
    return pl.pallas_call(
        _rmsnorm_kernel,
        out_shape=jax.ShapeDtypeStruct((T, D), out_dtype),
        grid=(T // tm,),
        in_specs=[pl.BlockSpec((tm, D), lambda m: (m, 0)), _layer_spec(g_all, l)],
        out_specs=pl.BlockSpec((tm, D), lambda m: (m, 0)),
        compiler_params=_cparams(1),
        name="rmsnorm",
    )(x, g_all)


def _prep_kernel(are, aim, ldt, bre, bim, sguw, wglu, poolw,
                 lamr_o, lami_o, pwr_o, pwi_o, bbr_o, bbi_o, sguw_o, wglu_o, poolw_o, *, S):
    a_re = are[...]
    a_im = aim[...]
    dt = jnp.exp(ldt[...])
    mag = jnp.exp(a_re * dt)
    lbr = mag * jnp.cos(a_im * dt)
    lbi = mag * jnp.sin(a_im * dt)
    den = a_re * a_re + a_im * a_im
    nr = lbr - 1.0
    kr = (nr * a_re + lbi * a_im) / den
    ki = (lbi * a_re - nr * a_im) / den
    lamr_o[...] = lbr
    lami_o[...] = lbi
    gp = a_re.shape[-1]

    rows = bre.shape[0]
    rc = 64

    def body(i, c):
        r = pl.multiple_of(i * rc, rc)
        br = bre[pl.ds(r, rc), :]
        bi = bim[pl.ds(r, rc), :]
        bbr_o[pl.ds(r, rc), :] = (kr * br - ki * bi).astype(BF16)
        bbi_o[pl.ds(r, rc), :] = (kr * bi + ki * br).astype(BF16)
        return c

    lax.fori_loop(0, rows // rc, body, 0)

    pr, pi = lbr, lbi
    for s in range(S):
        pwr_o[SUBLANES * s:SUBLANES * (s + 1), :] = jnp.broadcast_to(pr, (SUBLANES, gp))
        pwi_o[SUBLANES * s:SUBLANES * (s + 1), :] = jnp.broadcast_to(pi, (SUBLANES, gp))
        pr, pi = pr * lbr - pi * lbi, pr * lbi + pi * lbr

    t_i = lax.broadcasted_iota(I32, (CHUNK, CHUNK), 0)
    s_i = lax.broadcasted_iota(I32, (CHUNK, CHUNK), 1)
    for g in range(SGU_GROUPS):
        sguw_o[g] = jnp.where(t_i >= s_i, sguw[g], 0.0).astype(BF16)
    wglu_o[...] = wglu[...].astype(BF16)
    for g in range(len(POOL_WINDOWS)):
        poolw_o[g] = poolw[g].astype(BF16)


def _prep(a_re, a_im, log_dt, b_re, b_im, sgu_w, w_glu, pool_w, S):
    L, G, P = a_re.shape
    H = b_re.shape[-1]
    GP, W = G * P, G * H
    eye = jnp.eye(G, dtype=F32)
    bre_bd = jnp.einsum("lgph,gk->lghkp", b_re, eye).reshape(L, W, GP)
    bim_bd = jnp.einsum("lgph,gk->lghkp", b_im, eye).reshape(L, W, GP)
    are2 = a_re.reshape(L, 1, GP)
    aim2 = a_im.reshape(L, 1, GP)
    ldt2 = jnp.repeat(log_dt, P, axis=-1).reshape(L, 1, GP)

    def lspec(shape):
        nd = len(shape)
        return pl.BlockSpec((None,) + tuple(shape[1:]), lambda l: (l,) + (0,) * (nd - 1))

    ins = [are2, aim2, ldt2, bre_bd, bim_bd, sgu_w, w_glu, pool_w]
    out_shapes = [
        jax.ShapeDtypeStruct((L, 1, GP), F32), jax.ShapeDtypeStruct((L, 1, GP), F32),
        jax.ShapeDtypeStruct((L, SUBLANES * S, GP), F32), jax.ShapeDtypeStruct((L, SUBLANES * S, GP), F32),
        jax.ShapeDtypeStruct((L, W, GP), BF16), jax.ShapeDtypeStruct((L, W, GP), BF16),
        jax.ShapeDtypeStruct(sgu_w.shape, BF16), jax.ShapeDtypeStruct(w_glu.shape, BF16),
        jax.ShapeDtypeStruct(pool_w.shape, BF16),
    ]
    return pl.pallas_call(
        functools.partial(_prep_kernel, S=S),
        out_shape=out_shapes,
        grid=(L,),
        in_specs=[lspec(a.shape) for a in ins],
        out_specs=[lspec(o.shape) for o in out_shapes],
        compiler_params=_cparams(1),
        name="ssm_prep",
    )(*ins)


PROJ_CHUNK = 256


def _stream_weight_cols(w_hbm, l, n_cols, stage, sem, consume):
    def copy(c, slot):
        return pltpu.make_async_copy(w_hbm.at[l, :, pl.ds(c * PROJ_CHUNK, PROJ_CHUNK)], stage.at[slot],
                                     sem.at[slot])

    n = n_cols // PROJ_CHUNK
    copy(0, 0).start()
    for c in range(n):
        slot = c % 2
        if c + 1 < n:
            copy(c + 1, 1 - slot).start()
        copy(c, slot).wait()
        consume(c, slot)


N_MIXER_INPUTS = 23


def _mixer_prompt_kernel(*refs, n_batch, **kw):
    o_ref = refs[N_MIXER_INPUTS]
    b = pl.program_id(0)

    @pl.when(b < n_batch)
    def _():
        _mixer_prompt_body(*refs, **kw)

    @pl.when((b == n_batch) & (pl.program_id(1) == 0))
    def _():
        o_ref[...] = jnp.zeros_like(o_ref)


def _mixer_prompt_body(h_ref, w_hbm, lng, lnb, sguw, sgub, cw, cb, clg, clb,
                       lamr, lami, pwr, pwi, bbr, bbi, cr, ci, sd, wglu, bglu, poolw, pscale,
                       o_ref, conv_o, pool_o, sre_o, sim_o,
                       w6bf, wstage, wsem, p_ref,
                       gbuf, gsh, pbuf, xr, xi, car_re, car_im, fin_re, fin_im, cm_re, cm_im,
                       *, l, tt, S, W, GP):
    j = pl.program_id(1)
    nt = pl.num_programs(1)
    GH = 32
    PH = 16

    @pl.when((pl.program_id(0) == 0) & (j == 0))
    def _():
        def consume(c, slot):
            _cast_rows(wstage.at[slot], w6bf.at[:, c * PROJ_CHUNK:(c + 1) * PROJ_CHUNK], 256)

        _stream_weight_cols(w_hbm, l, 6 * W, wstage, wsem, consume)

    @pl.when(j == 0)
    def _():
        gbuf[0:GH, :] = jnp.zeros((GH, W), F32)
        pbuf[0:PH, :] = jnp.zeros((PH, W), F32)
        car_re[...] = jnp.zeros_like(car_re)
        car_im[...] = jnp.zeros_like(car_im)

    h = h_ref[...]
    for c in range(3):
        p_ref[:, 2 * c * W:2 * (c + 1) * W] = _dot(h, w6bf[:, 2 * c * W:2 * (c + 1) * W])

    for c in range(tt // CHUNK):
        rows = slice(c * CHUNK, (c + 1) * CHUNK)
        u = _gelu(p_ref[rows, 0:W])
        v = _layernorm(_gelu(p_ref[rows, W:2 * W]), lng[...], lnb[...])
        vb = v.astype(BF16)
        gw = W // SGU_GROUPS
        for g in range(SGU_GROUPS):
            cols = slice(g * gw, (g + 1) * gw)
            mixed = _dot(sguw[g], vb[:, cols]) + sgub[:, g:g + 1]
            o_ref[rows, cols] = (u[:, cols] * mixed).astype(BF16)

    rc = 64
    for c in range(tt // rc):
        rows = slice(c * rc, (c + 1) * rc)
        gbuf[GH + c * rc:GH + (c + 1) * rc, :] = p_ref[rows, 2 * W:3 * W] * _sigmoid(p_ref[rows, 3 * W:4 * W])
    sh_rows = gsh.shape[1]
    for b in range(1, SUBLANES):
        gsh[b] = gbuf[b:b + sh_rows, :]
    rc = 32
    base = GH - (CONV_K - 1)
    for c in range(tt // rc):
        acc = jnp.zeros((rc, W), F32)
        for k in range(CONV_K):
            r0 = base + c * rc + k
            b, a0 = r0 % SUBLANES, r0 - r0 % SUBLANES
            rows = gbuf[a0:a0 + rc, :] if b == 0 else gsh[b, a0:a0 + rc, :]
            acc = acc + cw[k:k + 1, :] * rows
        y = _layernorm(acc + cb[...], clg[...], clb[...])
        o_ref[c * rc:(c + 1) * rc, W:2 * W] = (y * _sigmoid(y)).astype(BF16)
    gbuf[0:GH, :] = gbuf[tt:tt + GH, :]

    @pl.when(j == nt - 1)
    def _():
        conv_o[...] = gbuf[GH - (CONV_K - 1):GH, :]

    uc = p_ref[:, 4 * W:5 * W]
    r_i = lax.broadcasted_iota(I32, (tt, tt), 0)
    c_i = lax.broadcasted_iota(I32, (tt, tt), 1)
    perm = jnp.where(((r_i & (SUBLANES - 1)) * S + (r_i >> 3)) == c_i, 1.0, 0.0).astype(BF16)
    up = _dot(perm, uc.astype(BF16)).astype(BF16)
    hw, hg = W // 2, GP // 2
    for q in range(2):
        xr[:, q * hg:(q + 1) * hg] = _dot(up[:, q * hw:(q + 1) * hw], bbr[q * hw:(q + 1) * hw, q * hg:(q + 1) * hg])
        xi[:, q * hg:(q + 1) * hg] = _dot(up[:, q * hw:(q + 1) * hw], bbi[q * hw:(q + 1) * hw, q * hg:(q + 1) * hg])

    for cbi in range(GP // SCAN_COLS):
        cols = slice(cbi * SCAN_COLS, (cbi + 1) * SCAN_COLS)
        lr = jnp.broadcast_to(lamr[:, cols], (SUBLANES, SCAN_COLS))
        li = jnp.broadcast_to(lami[:, cols], (SUBLANES, SCAN_COLS))

        def step(s, carry, cols=cols, lr=lr, li=li):
            sr, si = carry
            r0 = pl.multiple_of(s * SUBLANES, SUBLANES)
            nr = lr * sr - li * si + xr[pl.ds(r0, SUBLANES), cols]
            ni = lr * si + li * sr + xi[pl.ds(r0, SUBLANES), cols]
            xr[pl.ds(r0, SUBLANES), cols] = nr
            xi[pl.ds(r0, SUBLANES), cols] = ni
            return nr, ni

        z = jnp.zeros((SUBLANES, SCAN_COLS), F32)
        fr, fi = lax.fori_loop(0, S, step, (z, z))
        fin_re[:, cols] = fr
        fin_im[:, cols] = fi

    lsr = pwr[SUBLANES * (S - 1):SUBLANES * (S - 1) + 1, :]
    lsi = pwi[SUBLANES * (S - 1):SUBLANES * (S - 1) + 1, :]
    c_r = car_re[0:1, :]
    c_im = car_im[0:1, :]
    cm_re[0:1, :] = c_r
    cm_im[0:1, :] = c_im
    for q in range(1, SUBLANES):
        f_r = fin_re[q - 1:q, :]
        f_i = fin_im[q - 1:q, :]
        c_r, c_im = f_r + lsr * c_r - lsi * c_im, f_i + lsr * c_im + lsi * c_r
        cm_re[q:q + 1, :] = c_r
        cm_im[q:q + 1, :] = c_im
    n_r = fin_re[SUBLANES - 1:SUBLANES, :] + lsr * c_r - lsi * c_im
    n_i = fin_im[SUBLANES - 1:SUBLANES, :] + lsr * c_im + lsi * c_r
    car_re[0:1, :] = n_r
    car_im[0:1, :] = n_i

    @pl.when(j == nt - 1)
    def _():
        sre_o[...] = n_r
        sim_o[...] = n_i

    for cbi in range(GP // SCAN_COLS):
        cols = slice(cbi * SCAN_COLS, (cbi + 1) * SCAN_COLS)
        mr = cm_re[:, cols]
        mi = cm_im[:, cols]

        def fix(s, c, cols=cols, mr=mr, mi=mi):
            r0 = pl.multiple_of(s * SUBLANES, SUBLANES)
            pr = pwr[pl.ds(r0, SUBLANES), cols]
            pi = pwi[pl.ds(r0, SUBLANES), cols]
            xr[pl.ds(r0, SUBLANES), cols] = xr[pl.ds(r0, SUBLANES), cols] + (pr * mr - pi * mi)
            xi[pl.ds(r0, SUBLANES), cols] = xi[pl.ds(r0, SUBLANES), cols] + (pr * mi + pi * mr)
            return c

        lax.fori_loop(0, S, fix, 0)

    yp = jnp.concatenate(
        [_dot(xr[:, q * hg:(q + 1) * hg].astype(BF16), cr[q * hg:(q + 1) * hg, q * hw:(q + 1) * hw])
         - _dot(xi[:, q * hg:(q + 1) * hg].astype(BF16), ci[q * hg:(q + 1) * hg, q * hw:(q + 1) * hw])
         for q in range(2)], axis=1)
    unperm = jnp.where(((c_i & (SUBLANES - 1)) * S + (c_i >> 3)) == r_i, 1.0, 0.0).astype(BF16)
    y1 = yp.astype(BF16)
    r1 = yp - y1.astype(F32)
    y2 = r1.astype(BF16)
    y3 = (r1 - y2.astype(F32)).astype(BF16)
    y = _dot(unperm, y1) + _dot(unperm, y2) + _dot(unperm, y3)
    z = _gelu(y + sd[...] * uc)
    gl = _dot(z.astype(BF16), wglu[...]) + bglu[...]
    o_ref[:, 2 * W:3 * W] = (z * _sigmoid(gl)).astype(BF16)

    xd = p_ref[:, 5 * W:6 * W]
    pbuf[PH:PH + tt, :] = xd
    gw = W // len(POOL_WINDOWS)
    pos = (j * tt + lax.broadcasted_iota(I32, (tt, 1), 0) + 1).astype(F32)
    for gi, win in enumerate(POOL_WINDOWS):
        cols = slice(gi * gw, (gi + 1) * gw)
        s = xd[:, cols]
        for i in range(1, win):
            s = s + pbuf[PH - i:PH - i + tt, cols]
        cnt = jnp.minimum(pos, float(win))
        pooled = s / cnt - xd[:, cols]
        mixed = _dot(pooled.astype(BF16), poolw[gi]) * pscale[:, cols]
        o_ref[:, 3 * W + gi * gw:3 * W + (gi + 1) * gw] = mixed.astype(BF16)
    pbuf[0:PH, :] = pbuf[tt:tt + PH, :]

    @pl.when(j == nt - 1)
    def _():
        pool_o[...] = pbuf[PH - POOL_HIST:PH, :]


def _mixer_prompt(h, w_in, B, L, T, l, small, prep, cr, ci):
    W = small["sgu_ln_g"].shape[-1]
    GP = prep["lamr"].shape[-1]
    D = h.shape[1]
    D4 = 4 * W
    tt = TIME_TILE
    S = tt // SUBLANES
    nt = L // tt
    ins = [h, w_in,
           small["sgu_ln_g"], small["sgu_ln_b"], prep["sguw"], small["sgu_bT"],
           small["conv_w"], small["conv_b"], small["conv_ln_g"], small["conv_ln_b"],
           prep["lamr"], prep["lami"], prep["pwr"], prep["pwi"], prep["bbr"], prep["bbi"], cr, ci,
           small["ssm_d"], prep["wglu"], small["ssm_b_glu"], prep["poolw"], small["pool_scale"]]
    assert len(ins) == N_MIXER_INPUTS
    in_specs = [pl.BlockSpec((tt, D), lambda b, j: (jnp.minimum(b * nt + j, B * nt), 0)),
                pl.BlockSpec(memory_space=pl.ANY)]
    in_specs += [_layer_spec(a, l, single=True) for a in ins[2:]]
    out_shapes = [
        jax.ShapeDtypeStruct((T, D4), BF16),
        jax.ShapeDtypeStruct((B, CONV_K - 1, W), F32),
        jax.ShapeDtypeStruct((B, POOL_HIST, W), F32),
        jax.ShapeDtypeStruct((B, 1, GP), F32),
        jax.ShapeDtypeStruct((B, 1, GP), F32),
    ]
    out_specs = [
        pl.BlockSpec((tt, D4), lambda b, j: (jnp.minimum(b * nt + j, B * nt), 0)),
        pl.BlockSpec((None, CONV_K - 1, W), lambda b, j: (jnp.minimum(b, B - 1), 0, 0)),
        pl.BlockSpec((None, POOL_HIST, W), lambda b, j: (jnp.minimum(b, B - 1), 0, 0)),
        pl.BlockSpec((None, 1, GP), lambda b, j: (jnp.minimum(b, B - 1), 0, 0)),
        pl.BlockSpec((None, 1, GP), lambda b, j: (jnp.minimum(b, B - 1), 0, 0)),
    ]
    scratch = [
        pltpu.VMEM((D, 6 * W), BF16), pltpu.VMEM((2, D, PROJ_CHUNK), F32), pltpu.SemaphoreType.DMA((2,)),
        pltpu.VMEM((tt, 6 * W), F32),
        pltpu.VMEM((32 + tt, W), F32), pltpu.VMEM((SUBLANES, 32 + tt - SUBLANES, W), F32),
        pltpu.VMEM((16 + tt, W), F32),
        pltpu.VMEM((tt, GP), F32), pltpu.VMEM((tt, GP), F32),
        pltpu.VMEM((SUBLANES, GP), F32), pltpu.VMEM((SUBLANES, GP), F32),
        pltpu.VMEM((SUBLANES, GP), F32), pltpu.VMEM((SUBLANES, GP), F32),
        pltpu.VMEM((SUBLANES, GP), F32), pltpu.VMEM((SUBLANES, GP), F32),
    ]
    return pl.pallas_call(
        functools.partial(_mixer_prompt_kernel, n_batch=B, l=l, tt=tt, S=S, W=W, GP=GP),
        out_shape=out_shapes,
        grid=(B + 1, nt),
        in_specs=in_specs,
        out_specs=out_specs,
        scratch_shapes=scratch,
        compiler_params=_cparams(2),
        name="mixer_prompt",
    )(*ins)


def _mixer_sample_kernel(h_ref, w_hbm, stc, stp, hre, him, lng, lnb, w00, b0, cw, cb, clg, clb,
                         lamr, lami, bbr, bbi, cr, ci, sd, wglu, bglu, poolw, pscale, o_in,
                         o_ref, g_o, v_o, d_o, xre_o, xim_o,
                         p_ref, wstage, wchunk, wsem, *, l, W):
    del o_in
    h = h_ref[...]

    def consume(c, slot):
        _cast_rows(wstage.at[slot], wchunk, 256)
        p_ref[:, c * PROJ_CHUNK:(c + 1) * PROJ_CHUNK] = _dot(h, wchunk[...])

    _stream_weight_cols(w_hbm, l, 6 * W, wstage, wsem, consume)

    u = _gelu(p_ref[:, 0:W])
    v = _layernorm(_gelu(p_ref[:, W:2 * W]), lng[...], lnb[...])
    v_o[...] = v
    o_ref[:, 0:W] = (u * (w00[...] * v + b0[...])).astype(BF16)

    g = p_ref[:, 2 * W:3 * W] * _sigmoid(p_ref[:, 3 * W:4 * W])
    g_o[...] = g
    acc = cw[CONV_K - 1:CONV_K, :] * g
    for k in range(CONV_K - 1):
        acc = acc + cw[k:k + 1, :] * stc[:, k * W:(k + 1) * W]
    y = _layernorm(acc + cb[...], clg[...], clb[...])
    o_ref[:, W:2 * W] = (y * _sigmoid(y)).astype(BF16)

    uc = p_ref[:, 4 * W:5 * W]
    ub = uc.astype(BF16)
    h_r = hre[...]
    h_i = him[...]
    l_r = lamr[...]
    l_i = lami[...]
    x_r = l_r * h_r - l_i * h_i + _dot(ub, bbr[...])
    x_i = l_r * h_i + l_i * h_r + _dot(ub, bbi[...])
    xre_o[...] = x_r
    xim_o[...] = x_i
    yv = _dot(x_r.astype(BF16), cr[...]) - _dot(x_i.astype(BF16), ci[...]) + sd[...] * uc
    z = _gelu(yv)
    gl = _dot(z.astype(BF16), wglu[...]) + bglu[...]
    o_ref[:, 2 * W:3 * W] = (z * _sigmoid(gl)).astype(BF16)

    xd = p_ref[:, 5 * W:6 * W]
    d_o[...] = xd
    gw = W // len(POOL_WINDOWS)
    for gi, win in enumerate(POOL_WINDOWS):
        cols = slice(gi * gw, (gi + 1) * gw)
        s = xd[:, cols]
        for i in range(1, win):
            r = POOL_HIST - i
            s = s + stp[:, r * W + gi * gw:r * W + (gi + 1) * gw]
        cnt = float(min(PAST_LEN + 1, win))
        pooled = s / cnt - xd[:, cols]
        mixed = _dot(pooled.astype(BF16), poolw[gi]) * pscale[:, cols]
        o_ref[:, 3 * W + gi * gw:3 * W + (gi + 1) * gw] = mixed.astype(BF16)


def _mixer_sample(h, w_in, o_prev, stc2, stp2, hre2, him2, DB, T, l, small, prep, cr, ci):
    W = small["sgu_ln_g"].shape[-1]
    GP = prep["lamr"].shape[-1]
    D = h.shape[1]
    D4 = 4 * W
    blk = (T - DB) // DB
    ins = [h, w_in, stc2, stp2, hre2, him2,
           small["sgu_ln_g"], small["sgu_ln_b"], small["sgu_w00"], small["sgu_b0"],
           small["conv_w"], small["conv_b"], small["conv_ln_g"], small["conv_ln_b"],
           prep["lamr"], prep["lami"], prep["bbr"], prep["bbi"], cr, ci,
           small["ssm_d"], prep["wglu"], small["ssm_b_glu"], prep["poolw"], small["pool_scale"], o_prev]
    in_specs = [pl.BlockSpec((DB, D), lambda i: (blk, 0)), pl.BlockSpec(memory_space=pl.ANY)]
    in_specs += [_layer_spec(a, l, single=True) for a in ins[2:-1]]
    in_specs += [pl.BlockSpec(memory_space=pl.ANY)]
    out_shapes = [
        jax.ShapeDtypeStruct((T, D4), BF16),
        jax.ShapeDtypeStruct((DB, W), F32), jax.ShapeDtypeStruct((DB, W), F32), jax.ShapeDtypeStruct((DB, W), F32),
        jax.ShapeDtypeStruct((DB, GP), F32), jax.ShapeDtypeStruct((DB, GP), F32),
    ]
    out_specs = [
        pl.BlockSpec((DB, D4), lambda i: (blk, 0)),
        pl.BlockSpec((DB, W), lambda i: (0, 0)), pl.BlockSpec((DB, W), lambda i: (0, 0)),
        pl.BlockSpec((DB, W), lambda i: (0, 0)),
        pl.BlockSpec((DB, GP), lambda i: (0, 0)), pl.BlockSpec((DB, GP), lambda i: (0, 0)),
    ]
    return pl.pallas_call(
        functools.partial(_mixer_sample_kernel, l=l, W=W),
        out_shape=out_shapes,
        grid=(1,),
        in_specs=in_specs,
        out_specs=out_specs,
        scratch_shapes=[pltpu.VMEM((DB, 6 * W), F32), pltpu.VMEM((2, D, PROJ_CHUNK), F32),
                        pltpu.VMEM((D, PROJ_CHUNK), BF16), pltpu.SemaphoreType.DMA((2,))],
        input_output_aliases={len(ins) - 1: 0},
        compiler_params=_cparams(1),
        name="mixer_sample",
    )(*ins)


def _merge_kernel(h_ref, o_ref, wg0, wg1, wg2, wg3, wb_ref, m_ref, wgbf, wbbf, *, W):
    wgs = (wg0, wg1, wg2, wg3)

    @pl.when(pl.program_id(1) == 0)
    def _():
        for b in range(4):
            _cast_rows(wgs[b], wgbf.at[b], 256)
            wbbf[b] = wb_ref[b].astype(BF16)

    h = h_ref[...]
    acc = None
    for b in range(4):
        gate = _sigmoid(_dot(h, wgbf[b]))
        term = gate * _dot(o_ref[:, b * W:(b + 1) * W], wbbf[b])
        acc = term if acc is None else acc + term
    m_ref[...] = acc.astype(BF16)


def _merge(h, o, w_in, w_branch, l, gate_off):
    T, D = h.shape
    W = w_branch.shape[2]
    tm = _pick(T, (640, 512, 384, 256, 128))
    tn = 256
    nb = D // tn

    def gate_spec(b):
        blk0 = (gate_off + b * D) // tn
        return pl.BlockSpec((None, D, tn), lambda n, m: (l, 0, blk0 + n))

    return pl.pallas_call(
        functools.partial(_merge_kernel, W=W),
        out_shape=jax.ShapeDtypeStruct((T, D), BF16),
        grid=(nb, T // tm),
        in_specs=[pl.BlockSpec((tm, D), lambda n, m: (m, 0)),
                  pl.BlockSpec((tm, 4 * W), lambda n, m: (m, 0)),
                  gate_spec(0), gate_spec(1), gate_spec(2), gate_spec(3),
                  pl.BlockSpec((None, 4, W, tn), lambda n, m: (l, 0, 0, n))],
        out_specs=pl.BlockSpec((tm, tn), lambda n, m: (m, n)),
        scratch_shapes=[pltpu.VMEM((4, D, tn), BF16), pltpu.VMEM((4, W, tn), BF16)],
        compiler_params=_cparams(2),
        name="gated_merge",
    )(h, o, w_in, w_in, w_in, w_in, w_branch)


def _out_kernel(m_ref, x_ref, w_ref, g_ref, wr_ref, x1_ref, hp_ref, lg_ref, wbf, wr2):
    R = wr_ref.shape[-1]

    @pl.when(pl.program_id(0) == 0)
    def _():
        _cast_rows(w_ref, wbf, 256)
        wr = wr_ref[...]
        hi = wr.astype(BF16)
        wr2[:, 0:R] = hi
        wr2[:, R:2 * R] = (wr - hi.astype(F32)).astype(BF16)

    x1 = x_ref[...] + _dot(m_ref[...], wbf[...])
    x1_ref[...] = x1
    h = _rms_scale(x1, g_ref[...])
    hb = h.astype(BF16)
    hl = (h - hb.astype(F32)).astype(BF16)
    both = _dot(hb, wr2[...])
    lg_ref[...] = both[:, 0:R] + both[:, R:2 * R] + _dot(hl, wr2[:, 0:R])
    bits = lax.bitcast_convert_type(hb.astype(F32), U32)
    half = bits.shape[1] // 2
    hp_ref[...] = (bits[:, :half] >> 16) | (bits[:, half:] & jnp.uint32(0xFFFF0000))


def _out_proj(merged, x, w_out, g_all, wr_all, l):
    T, D = x.shape
    tm = _pick(T, (320, 256, 128))
    R = wr_all.shape[-1]
    return pl.pallas_call(
        _out_kernel,
        out_shape=[jax.ShapeDtypeStruct((T, D), F32), jax.ShapeDtypeStruct((T, D // 2), U32),
                   jax.ShapeDtypeStruct((T, R), F32)],
        grid=(T // tm,),
        in_specs=[pl.BlockSpec((tm, D), lambda m: (m, 0)),
                  pl.BlockSpec((tm, D), lambda m: (m, 0)),
                  pl.BlockSpec((None, D, D), lambda m: (l, 0, 0), pipeline_mode=pl.Buffered(1)),
                  _layer_spec(g_all, l), _layer_spec(wr_all, l)],
        out_specs=[pl.BlockSpec((tm, D), lambda m: (m, 0)),
                   pl.BlockSpec((tm, D // 2), lambda m: (m, 0)),
                   pl.BlockSpec((tm, R), lambda m: (m, 0))],
        scratch_shapes=[pltpu.VMEM((D, D), BF16), pltpu.VMEM((D, 2 * R), BF16)],
        compiler_params=_cparams(1),
        name="out_proj",
    )(merged, x, w_out, g_all, wr_all)


def _route_kernel(lg_ref, bias_ref, rt_ref, meta_ref, cnt, off, *, tm, tile, n_tiles_max):
    p = pl.program_id(0)
    m = pl.program_id(1)
    lane = lax.broadcasted_iota(I32, (tm, LANES), 1).astype(F32)
    neg = jnp.float32(-jnp.inf)
    big = jnp.float32(1e9)

    @pl.when((p == 0) & (m == 0))
    def _():
        cnt[...] = jnp.zeros_like(cnt)

    lg = lg_ref[...] + bias_ref[...]
    is_g = lane < N_GROUPS
    gl = jnp.where(is_g, lg, neg)
    gmax = jnp.max(gl, axis=-1, keepdims=True)
    gidx = jnp.min(jnp.where(gl == gmax, lane, big), axis=-1, keepdims=True)
    gsum = jnp.sum(jnp.where(is_g, jnp.exp(gl - gmax), 0.0), axis=-1, keepdims=True)
    g_w = 1.0 / gsum
    lo = N_GROUPS + EXP_PER_GROUP * gidx
    in_grp = (lane >= lo) & (lane < lo + EXP_PER_GROUP)
    el = jnp.where(in_grp, lg, neg)
    v1 = jnp.max(el, axis=-1, keepdims=True)
    i1 = jnp.min(jnp.where(el == v1, lane, big), axis=-1, keepdims=True)
    el2 = jnp.where(lane == i1, neg, el)
    v2 = jnp.max(el2, axis=-1, keepdims=True)
    i2 = jnp.min(jnp.where(el2 == v2, lane, big), axis=-1, keepdims=True)
    e2x = jnp.exp(v2 - v1)
    w1 = g_w / (1.0 + e2x)
    w2 = g_w * e2x / (1.0 + e2x)
    e1 = i1 - N_GROUPS
    e2 = i2 - N_GROUPS
    a1 = jnp.where(lane == e1, 1.0, 0.0)
    a2 = jnp.where(lane == e2, 1.0, 0.0)
    a = a1 + a2

    @pl.when(p == 0)
    def _():
        cnt[0:1, :] = cnt[0:1, :] + jnp.sum(a, axis=0, keepdims=True)

    @pl.when((p == 1) & (m == 0))
    def _():
        counts = cnt[0:1, :]
        tiles = jnp.floor((counts + (tile - 1)) * (1.0 / tile))
        tiles8 = jnp.broadcast_to(tiles, (SUBLANES, LANES)).astype(BF16)
        ri = lax.broadcasted_iota(I32, (LANES, LANES), 0)
        ci = lax.broadcasted_iota(I32, (LANES, LANES), 1)
        upper = jnp.where(ri < ci, 1.0, 0.0).astype(BF16)
        toff = _dot(tiles8, upper)[0:1, :]
        off[0:1, :] = toff * tile
        tend = toff + tiles
        lane1 = lax.broadcasted_iota(I32, (1, LANES), 1).astype(F32)
        n_act = jnp.sum(jnp.where(lane1 == N_EXPERTS - 1, tend, 0.0), axis=-1, keepdims=True)
        texp = jnp.zeros((1, LANES), F32)
        for e in range(N_EXPERTS - 1):
            end_e = jnp.sum(jnp.where(lane1 == e, tend, 0.0), axis=-1, keepdims=True)
            texp = texp + jnp.where(jnp.minimum(lane1, n_act - 1.0) >= end_e, 1.0, 0.0)
        meta_ref[0:1, :] = counts
        meta_ref[1:2, :] = off[0:1, :]
        meta_ref[2:3, :] = texp
        meta_ref[3:4, :] = jnp.broadcast_to(n_act, (1, LANES))
        meta_ref[4:5, :] = tend
        meta_ref[5:8, :] = jnp.zeros((3, LANES), F32)
        cnt[...] = jnp.zeros_like(cnt)

    @pl.when(p == 1)
    def _():
        ri = lax.broadcasted_iota(I32, (tm, tm), 0)
        ci = lax.broadcasted_iota(I32, (tm, tm), 1)
        ltri = jnp.where(ci < ri, 1.0, 0.0).astype(BF16)
        cum = _dot(ltri, a.astype(BF16)) + cnt[0:1, :] + off[0:1, :]
        pos1 = jnp.sum(a1 * cum, axis=-1, keepdims=True)
        pos2 = jnp.sum(a2 * cum, axis=-1, keepdims=True)
        cnt[0:1, :] = cnt[0:1, :] + jnp.sum(a, axis=0, keepdims=True)
        rt = jnp.where(lane == 0, e1, 0.0)
        rt = jnp.where(lane == 1, e2, rt)
        rt = jnp.where(lane == 2, w1, rt)
        rt = jnp.where(lane == 3, w2, rt)
        rt = jnp.where(lane == 4, pos1, rt)
        rt = jnp.where(lane == 5, pos2, rt)
        rt_ref[...] = rt


def _route(logits, bias_all, l, tile, n_tiles_max):
    T, R = logits.shape
    tm = _pick(T, (640, 512, 384, 256, 128))
    return pl.pallas_call(
        functools.partial(_route_kernel, tm=tm, tile=tile, n_tiles_max=n_tiles_max),
        out_shape=[jax.ShapeDtypeStruct((T, R), F32), jax.ShapeDtypeStruct((SUBLANES, LANES), F32)],
        grid=(2, T // tm),
        in_specs=[pl.BlockSpec((tm, R), lambda p, m: (m, 0)), _layer_spec(bias_all, l)],
        out_specs=[pl.BlockSpec((tm, R), lambda p, m: (m * p, 0)),
                   pl.BlockSpec((SUBLANES, LANES), lambda p, m: (0, 0))],
        scratch_shapes=[pltpu.VMEM((SUBLANES, LANES), F32), pltpu.VMEM((SUBLANES, LANES), F32)],
        compiler_params=_cparams(2),
        name="route",
    )(logits, bias_all)


def _dispatch_kernel(pos_ref, tend_ref, hp_ref, hs_ref, stage, zbuf, sem, zsem, *, T, tk, tile, n_tiles):
    i = pl.program_id(0)
    n = pl.num_programs(0)

    @pl.when(i == 0)
    def _():
        zbuf[...] = jnp.zeros_like(zbuf)

        def last_tile_copy(ex):
            end = tend_ref[ex]
            start = tend_ref[ex - 1] if ex > 0 else 0
            row0 = pl.multiple_of((end - 1) * tile, tile)
            return end > start, pltpu.make_async_copy(zbuf, hs_ref.at[pl.ds(row0, tile)], zsem)

        def spare_tile_copy(k):
            idx = tend_ref[N_EXPERTS - 1] + k
            row0 = pl.multiple_of(jnp.minimum(idx, n_tiles - 1) * tile, tile)
            return idx < n_tiles, pltpu.make_async_copy(zbuf, hs_ref.at[pl.ds(row0, tile)], zsem)

        fills = [last_tile_copy(ex) for ex in range(N_EXPERTS)] + [spare_tile_copy(k) for k in range(N_EXPERTS)]
        for go, cp in fills:
            @pl.when(go)
            def _(cp=cp):
                cp.start()

        for go, cp in fills:
            @pl.when(go)
            def _(cp=cp):
                cp.wait()

    def copies(step, slot):
        out = []
        for u in range(tk):
            t = step * tk + u
            for k in range(2):
                p = pos_ref[k * T + t]
                out.append(pltpu.make_async_copy(stage.at[slot, pl.ds(u, 1)], hs_ref.at[pl.ds(p, 1)],
                                                 sem.at[slot]))
        return out

    def wait_slot(s):
        rows = hs_ref.at[pl.ds(0, 2 * tk)]
        pltpu.make_async_copy(rows, rows, sem.at[s]).wait()

    slot = lax.rem(i, 2)
    for s in range(2):
        @pl.when(slot == s)
        def _(s=s):
            stage[s] = hp_ref[...]
            for cp in copies(i, s):
                cp.start()

        @pl.when((slot == 1 - s) & (i > 0))
        def _(s=s):
            wait_slot(s)

        @pl.when((slot == s) & (i == n - 1))
        def _(s=s):
            wait_slot(s)


def _dispatch(pos_flat, tend, hp, n_pad, tile):
    T, Dh = hp.shape
    tk = 128
    return pl.pallas_call(
        functools.partial(_dispatch_kernel, T=T, tk=tk, tile=tile, n_tiles=n_pad // tile),
        out_shape=jax.ShapeDtypeStruct((n_pad, Dh), U32),
        grid_spec=pltpu.PrefetchScalarGridSpec(
            num_scalar_prefetch=2,
            grid=(T // tk,),
            in_specs=[pl.BlockSpec((tk, Dh), lambda i, pos, tend: (i, 0))],
            out_specs=pl.BlockSpec(memory_space=pl.ANY),
            scratch_shapes=[pltpu.VMEM((2, tk, Dh), U32), pltpu.VMEM((tile, Dh), U32),
                            pltpu.SemaphoreType.DMA((2,)), pltpu.SemaphoreType.DMA],
        ),
        compiler_params=_cparams(1),
        name="dispatch",
    )(pos_flat, tend, hp)


def _expert_kernel(te_ref, na_ref, hs_ref, wg_ref, wu_ref, wd_ref, ys_ref,
                   wgst, wust, wdst, wgbf, wubf, wdbf, sem, *, l):
    i = pl.program_id(0)
    na = na_ref[0]
    e = te_ref[i]
    first = (i < na) & ((i == 0) | (e != te_ref[jnp.maximum(i - 1, 0)]))

    def weight_copies(ex):
        return (pltpu.make_async_copy(wg_ref.at[l, ex], wgst, sem),
                pltpu.make_async_copy(wu_ref.at[l, ex], wust, sem),
                pltpu.make_async_copy(wd_ref.at[l, ex], wdst, sem))

    @pl.when(i == 0)
    def _():
        for cp in weight_copies(e):
            cp.start()

    @pl.when(first)
    def _():
        for cp in weight_copies(e):
            cp.wait()
        _cast_rows(wgst, wgbf, 256)
        _cast_rows(wust, wubf, 256)
        _cast_rows(wdst, wdbf, 256)
        j = lax.while_loop(lambda j: (j < na) & (te_ref[jnp.minimum(j, na - 1)] == e), lambda j: j + 1, i + 1)

        @pl.when(j < na)
        def _():
            for cp in weight_copies(te_ref[jnp.minimum(j, na - 1)]):
                cp.start()

    @pl.when(i < na)
    def _():
        w = hs_ref[...]
        half = w.shape[1]
        lo = lax.bitcast_convert_type(w << 16, F32).astype(BF16)
        hi = lax.bitcast_convert_type(w & jnp.uint32(0xFFFF0000), F32).astype(BF16)
        a = _dot(lo, wgbf[0:half, :]) + _dot(hi, wgbf[half:, :])
        b = _dot(lo, wubf[0:half, :]) + _dot(hi, wubf[half:, :])
        hid = (a * _sigmoid(a) * b).astype(BF16)
        ys_ref[...] = _dot(hid, wdbf[...])

    @pl.when(i >= na)
    def _():
        ys_ref[...] = jnp.zeros_like(ys_ref)


def _experts(te, na, hs, wg, wu, wd, l, tile, n_tiles_max):
    n_pad, Dh = hs.shape
    D = 2 * Dh
    F = wg.shape[-1]

    def row_map(i, te_ref, na_ref):
        return (jnp.minimum(i, na_ref[0] - 1), 0)

    any_spec = pl.BlockSpec(memory_space=pl.ANY)
    return pl.pallas_call(
        functools.partial(_expert_kernel, l=l),
        out_shape=jax.ShapeDtypeStruct((n_pad, D), F32),
        grid_spec=pltpu.PrefetchScalarGridSpec(
            num_scalar_prefetch=2,
            grid=(n_tiles_max,),
            in_specs=[pl.BlockSpec((tile, Dh), row_map), any_spec, any_spec, any_spec],
            out_specs=pl.BlockSpec((tile, D), lambda i, te_ref, na_ref: (i, 0)),
            scratch_shapes=[pltpu.VMEM((D, F), F32), pltpu.VMEM((D, F), F32), pltpu.VMEM((F, D), F32),
                            pltpu.VMEM((D, F), BF16), pltpu.VMEM((D, F), BF16), pltpu.VMEM((F, D), BF16),
                            pltpu.SemaphoreType.DMA],
        ),
        compiler_params=_cparams(1),
        name="experts",
    )(te, na, hs, wg, wu, wd)


def _combine_kernel(pos_ref, x1_ref, rt_ref, g_ref, ys_ref, *rest, T, tk, n_prompt):
    final = n_prompt is not None
    out_a, out_b, buf, sem = rest
    i = pl.program_id(0)
    n = pl.num_programs(0)

    def copies(step, slot):
        out = []
        for u in range(tk):
            t = step * tk + u
            for k in range(2):
                p = pos_ref[k * T + t]
                out.append(pltpu.make_async_copy(ys_ref.at[pl.ds(p, 1)], buf.at[slot, k, pl.ds(u, 1)],
                                                 sem.at[slot]))
        return out

    slot = lax.rem(i, 2)

    @pl.when(i == 0)
    def _():
        for cp in copies(0, 0):
            cp.start()

    for s in range(2):
        @pl.when((i + 1 < n) & (slot == 1 - s))
        def _(s=s):
            for cp in copies(i + 1, s):
                cp.start()

    for s in range(2):
        @pl.when(slot == s)
        def _(s=s):
            pltpu.make_async_copy(buf.at[s], buf.at[s], sem.at[s]).wait()
            rt = rt_ref[...]
            w1 = rt[:, 2:3]
            w2 = rt[:, 3:4]
            x2 = x1_ref[...] + w1 * buf[s, 0] + w2 * buf[s, 1]
            hn = _rms_scale(x2, g_ref[...])
            if not final:
                out_a[...] = x2
                out_b[...] = hn.astype(out_b.dtype)
            else:
                @pl.when(i < n_prompt)
                def _():
                    out_a[...] = hn

                @pl.when(i >= n_prompt)
                def _():
                    out_b[...] = hn


def _combine(pos_flat, x1, route, g_all, gl, ys, n_sample=None):
    T, D = x1.shape
    tk = 128
    R = route.shape[1]
    if n_sample is None:
        n_prompt = None
        g_spec = pl.BlockSpec((None, 1, D), lambda i, pos: (gl, 0, 0))
        out_shape = [jax.ShapeDtypeStruct((T, D), F32), jax.ShapeDtypeStruct((T, D), BF16)]
        out_specs = [pl.BlockSpec((tk, D), lambda i, pos: (i, 0)), pl.BlockSpec((tk, D), lambda i, pos: (i, 0))]
    else:
        assert n_sample == tk
        n_prompt = (T - n_sample) // tk
        g_spec = pl.BlockSpec((1, D), lambda i, pos: (0, 0))
        out_shape = [jax.ShapeDtypeStruct((T - n_sample, D), F32), jax.ShapeDtypeStruct((n_sample, D), F32)]
        out_specs = [pl.BlockSpec((tk, D), lambda i, pos: (jnp.minimum(i, n_prompt - 1), 0)),
                     pl.BlockSpec((tk, D), lambda i, pos: (0, 0))]
    return pl.pallas_call(
        functools.partial(_combine_kernel, T=T, tk=tk, n_prompt=n_prompt),
        out_shape=out_shape,
        grid_spec=pltpu.PrefetchScalarGridSpec(
            num_scalar_prefetch=1,
            grid=(T // tk,),
            in_specs=[pl.BlockSpec((tk, D), lambda i, pos: (i, 0)),
                      pl.BlockSpec((tk, R), lambda i, pos: (i, 0)),
                      g_spec,
                      pl.BlockSpec(memory_space=pl.ANY)],
            out_specs=out_specs,
            scratch_shapes=[pltpu.VMEM((2, 2, tk, D), F32), pltpu.SemaphoreType.DMA((2,))],
        ),
        compiler_params=_cparams(1),
        name="combine",
    )(pos_flat, x1, route, g_all, ys)


def kernel(x_prompt, x_sample, state_conv, state_pool, state_ssm_re, state_ssm_im,
           norm_mix_g, norm_ffn_g, w_in,
           sgu_ln_g, sgu_ln_b, sgu_w, sgu_b,
           conv_w, conv_b, conv_ln_g, conv_ln_b,
           ssm_a_re, ssm_a_im, ssm_log_dt, ssm_b_re, ssm_b_im, ssm_c_re, ssm_c_im, ssm_d, ssm_w_glu, ssm_b_glu,
           pool_w, pool_scale,
           w_branch, w_out,
           router_group_w, router_group_b, router_expert_w, router_expert_b,
           expert_w_gate, expert_w_up, expert_w_down,
           final_norm_g):
    B, L, D = x_prompt.shape
    DB = x_sample.shape[0]
    depth = w_in.shape[0]
    W = sgu_ln_g.shape[-1]
    G, P = ssm_a_re.shape[1:]
    GP = G * P
    T = B * L + DB
    gate_off = 6 * W
    assert x_sample.shape[1] == 1 and L % TIME_TILE == 0 and (B * L) % DB == 0

    row = lambda a: a.reshape(depth, 1, a.shape[-1])
    small = {
        "sgu_ln_g": row(sgu_ln_g), "sgu_ln_b": row(sgu_ln_b),
        "sgu_bT": jnp.swapaxes(sgu_b, 1, 2),
        "sgu_w00": row(jnp.repeat(sgu_w[:, :, 0, 0], W // SGU_GROUPS, axis=-1)),
        "sgu_b0": row(jnp.repeat(sgu_b[:, :, 0], W // SGU_GROUPS, axis=-1)),
        "conv_w": conv_w, "conv_b": row(conv_b), "conv_ln_g": row(conv_ln_g), "conv_ln_b": row(conv_ln_b),
        "ssm_d": row(ssm_d), "ssm_b_glu": row(ssm_b_glu), "pool_scale": row(pool_scale),
    }
    S = TIME_TILE // SUBLANES
    lamr, lami, pwr, pwi, bbr, bbi, sguw_bf, wglu_bf, poolw_bf = _prep(
        ssm_a_re, ssm_a_im, ssm_log_dt, ssm_b_re, ssm_b_im, sgu_w, ssm_w_glu, pool_w, S)
    prep = {"lamr": lamr, "lami": lami, "pwr": pwr, "pwi": pwi, "bbr": bbr, "bbi": bbi,
            "sguw": sguw_bf, "wglu": wglu_bf, "poolw": poolw_bf}
    eye = jnp.eye(G, dtype=BF16)
    cr_bd = jnp.einsum("lghp,gk->lgpkh", ssm_c_re.astype(BF16), eye).reshape(depth, GP, W)
    ci_bd = jnp.einsum("lghp,gk->lgpkh", ssm_c_im.astype(BF16), eye).reshape(depth, GP, W)

    wr = jnp.concatenate([router_group_w,
                          jnp.transpose(router_expert_w, (0, 2, 1, 3)).reshape(depth, D, N_EXPERTS)], axis=-1)
    wr = jnp.pad(wr, ((0, 0), (0, 0), (0, LANES - wr.shape[-1])))
    rb = jnp.concatenate([router_group_b, router_expert_b.reshape(depth, N_EXPERTS)], axis=-1)
    rb = jnp.pad(rb, ((0, 0), (0, LANES - rb.shape[-1]))).reshape(depth, 1, LANES)

    norm_mix3 = row(norm_mix_g)
    norm_ffn3 = row(norm_ffn_g)
    final3 = final_norm_g.reshape(1, D)

    n_tiles_max = -(-2 * T // EXPERT_TILE) + N_EXPERTS
    n_pad = n_tiles_max * EXPERT_TILE

    x = jnp.concatenate([x_prompt.reshape(B * L, D), x_sample.reshape(DB, D)], axis=0)
    h = _rmsnorm(x, norm_mix3, 0, BF16)

    stc2 = state_conv.reshape(depth, DB, (CONV_K - 1) * W)
    stp2 = state_pool.reshape(depth, DB, POOL_HIST * W)
    hre2 = state_ssm_re.reshape(depth, DB, GP)
    him2 = state_ssm_im.reshape(depth, DB, GP)

    conv_p, conv_s, pool_p, pool_s = [], [], [], []
    sre_p, sim_p, sre_s, sim_s, v_s = [], [], [], [], []
    y_p = y_s = None
    for l in range(depth):
        o, cp, pp, rp, ip = _mixer_prompt(h, w_in, B, L, T, l, small, prep, cr_bd, ci_bd)
        o, g_s, vs, d_in_s, xs_re, xs_im = _mixer_sample(h, w_in, o, stc2, stp2, hre2, him2, DB, T, l,
                                                         small, prep, cr_bd, ci_bd)
        merged = _merge(h, o, w_in, w_branch, l, gate_off)
        x1, hp, logits = _out_proj(merged, x, w_out, norm_ffn3, wr, l)
        route, meta = _route(logits, rb, l, EXPERT_TILE, n_tiles_max)
        pos_flat = jnp.transpose(route[:, 4:6]).astype(I32).reshape(2 * T)
        te = meta[2, :n_tiles_max].astype(I32)
        na = meta[3, :1].astype(I32)
        tend = meta[4, :N_EXPERTS].astype(I32)
        hs = _dispatch(pos_flat, tend, hp, n_pad, EXPERT_TILE)
        ys = _experts(te, na, hs, expert_w_gate, expert_w_up, expert_w_down, l, EXPERT_TILE, n_tiles_max)
        if l + 1 < depth:
            x, h = _combine(pos_flat, x1, route, norm_mix3, l + 1, ys)
        else:
            y_p, y_s = _combine(pos_flat, x1, route, final3, 0, ys, n_sample=DB)

        conv_p.append(cp)
        pool_p.append(pp)
        sre_p.append(rp.reshape(B, G, P))
        sim_p.append(ip.reshape(B, G, P))
        conv_s.append(jnp.concatenate([state_conv[l][:, 1:], g_s[:, None, :]], axis=1))
        pool_s.append(jnp.concatenate([state_pool[l][:, 1:], d_in_s[:, None, :]], axis=1))
        sre_s.append(xs_re.reshape(DB, G, P))
        sim_s.append(xs_im.reshape(DB, G, P))
        v_s.append(vs[:, None, :])

    y_prompt = y_p.reshape(B, L, D)
    y_sample = y_s.reshape(DB, 1, D)
    return (y_prompt, y_sample, jnp.stack(conv_p), jnp.stack(conv_s), jnp.stack(pool_p), jnp.stack(pool_s),
            jnp.stack(sre_p), jnp.stack(sim_p), jnp.stack(sre_s), jnp.stack(sim_s), jnp.stack(v_s))
```

```python
import functools
import math

import jax
import jax.numpy as jnp
from jax import lax
from jax.experimental import pallas as pl
from jax.experimental.pallas import tpu as pltpu

F32 = jnp.float32
BF16 = jnp.bfloat16
I32 = jnp.int32
U32 = jnp.uint32

EPS = 1e-6
CHUNK = 128
SGU_GROUPS = 4
CONV_K = 31
POOL_WINDOWS = (2, 4, 8, 16)
POOL_HIST = 15
SSM_H = 16
SSM_P = 64
N_GROUPS = 4
EXP_PER_GROUP = 4
N_EXPERTS = 16
PAST_LEN = 16384

LANES = 128
SUBLANES = 8
VMEM_LIMIT = 60 * 1024 * 1024

EXPERT_TILE = 256
TIME_TILE = 256
SCAN_COLS = 512


def _cparams(n_axes):
    return pltpu.CompilerParams(dimension_semantics=("arbitrary",) * n_axes,
                                vmem_limit_bytes=VMEM_LIMIT)


def _gelu(x):
    c = math.sqrt(2.0 / math.pi)
    return 0.5 * x * (1.0 + jnp.tanh(c * (x + 0.044715 * (x * x * x))))


def _sigmoid(x):
    return 0.5 * jnp.tanh(0.5 * x) + 0.5


def _layernorm(x, g, b):
    xc = x - jnp.mean(x, axis=-1, keepdims=True)
    var = jnp.mean(xc * xc, axis=-1, keepdims=True)
    return xc * lax.rsqrt(var + EPS) * g + b


def _rms_scale(x, g):
    return x * lax.rsqrt(jnp.mean(x * x, axis=-1, keepdims=True) + EPS) * g


def _dot(a, b):
    return jnp.dot(a, b, preferred_element_type=F32)


def _cast_rows(src_ref, dst_ref, chunk):
    rows = src_ref.shape[0]

    def body(i, c):
        r = pl.multiple_of(i * chunk, chunk)
        dst_ref[pl.ds(r, chunk), :] = src_ref[pl.ds(r, chunk), :].astype(dst_ref.dtype)
        return c

    lax.fori_loop(0, rows // chunk, body, 0)


def _layer_spec(arr, l, single=False):
    nd = arr.ndim
    mode = {"pipeline_mode": pl.Buffered(1)} if single else {}
    return pl.BlockSpec((None,) + tuple(arr.shape[1:]), lambda *_: (l,) + (0,) * (nd - 1), **mode)


def _pick(n, cands):
    for c in cands:
        if n % c == 0:
            return c
    raise ValueError(f"no tile for {n}")


def _rmsnorm_kernel(x_ref, g_ref, o_ref):
    o_ref[...] = _rms_scale(x_ref[...], g_ref[...]).astype(o_ref.dtype)


def _rmsnorm(x, g_all, l, out_dtype):
    T, D = x.shape
    tm = _pick(T, (640, 512, 384, 256, 128))
    return pl.pallas_call(
        _rmsnorm_kernel,
        out_shape=jax.ShapeDtypeStruct((T, D), out_dtype),
        grid=(T // tm,),
        in_specs=[pl.BlockSpec((tm, D), lambda m: (m, 0)), _layer_spec(g_all, l)],
        out_specs=pl.BlockSpec((tm, D), lambda m: (m, 0)),
        compiler_params=_cparams(1),
        name="rmsnorm",
    )(x, g_all)


def _split3(x):
    p1 = x.astype(BF16)
    r1 = x - p1.astype(F32)
    p2 = r1.astype(BF16)
    p3 = (r1 - p2.astype(F32)).astype(BF16)
    return p1, p2, p3


def _prep_kernel(are, aim, ldt, bre, bim, cre, cim, sguw, wglu, poolw,
                 lamr_o, lami_o, pwr_o, pwi_o, bbr_o, bbi_o, cr_o, ci_o, sguw_o, wglu_o, poolw_o, *, S, P, H):
    a_re = are[...]
    a_im = aim[...]
    dt = jnp.exp(ldt[...])
    mag = jnp.exp(a_re * dt)
    lbr = mag * jnp.cos(a_im * dt)
    lbi = mag * jnp.sin(a_im * dt)
    den = a_re * a_re + a_im * a_im
    nr = lbr - 1.0
    kr = (nr * a_re + lbi * a_im) / den
    ki = (lbi * a_re - nr * a_im) / den
    lamr_o[...] = lbr
    lami_o[...] = lbi
    gp = a_re.shape[-1]

    w = bre.shape[0]
    sh_p, sh_h = P.bit_length() - 1, H.bit_length() - 1
    rc = 128
    pe = lax.broadcasted_iota(I32, (P, gp), 0)
    ce = lax.broadcasted_iota(I32, (P, gp), 1)
    rep_p = jnp.where((ce & (P - 1)) == pe, 1.0, 0.0).astype(BF16)
    for i in range(w // rc):
        rows = slice(i * rc, (i + 1) * rc)
        r_i = lax.broadcasted_iota(I32, (rc, gp), 0) + i * rc
        c_i = lax.broadcasted_iota(I32, (rc, gp), 1)
        diag = (r_i >> sh_h) == (c_i >> sh_p)
        br = sum(_dot(piece, rep_p) for piece in _split3(bre[rows, :]))
        bi = sum(_dot(piece, rep_p) for piece in _split3(bim[rows, :]))
        bbr_o[rows, :] = jnp.where(diag, kr * br - ki * bi, 0.0).astype(BF16)
        bbi_o[rows, :] = jnp.where(diag, kr * bi + ki * br, 0.0).astype(BF16)

    he = lax.broadcasted_iota(I32, (H, w), 0)
    ce = lax.broadcasted_iota(I32, (H, w), 1)
    rep_h = jnp.where((ce & (H - 1)) == he, 1.0, 0.0).astype(BF16)
    rc = 256
    for i in range(gp // rc):
        rows = slice(i * rc, (i + 1) * rc)
        r_i = lax.broadcasted_iota(I32, (rc, w), 0) + i * rc
        c_i = lax.broadcasted_iota(I32, (rc, w), 1)
        diag = (r_i >> sh_p) == (c_i >> sh_h)
        cr_o[rows, :] = jnp.where(diag, _dot(cre[rows, :].astype(BF16), rep_h), 0.0).astype(BF16)
        ci_o[rows, :] = jnp.where(diag, _dot(cim[rows, :].astype(BF16), rep_h), 0.0).astype(BF16)

    pr, pi = lbr, lbi
    for s in range(S):
        pwr_o[SUBLANES * s:SUBLANES * (s + 1), :] = jnp.broadcast_to(pr, (SUBLANES, gp))
        pwi_o[SUBLANES * s:SUBLANES * (s + 1), :] = jnp.broadcast_to(pi, (SUBLANES, gp))
        pr, pi = pr * lbr - pi * lbi, pr * lbi + pi * lbr

    t_i = lax.broadcasted_iota(I32, (CHUNK, CHUNK), 0)
    s_i = lax.broadcasted_iota(I32, (CHUNK, CHUNK), 1)
    for g in range(SGU_GROUPS):
        sguw_o[g] = jnp.where(t_i >= s_i, sguw[g], 0.0).astype(BF16)
    wglu_o[...] = wglu[...].astype(BF16)
    for g in range(len(POOL_WINDOWS)):
        poolw_o[g] = poolw[g].astype(BF16)


def _prep(a_re, a_im, log_dt, b_re, b_im, c_re, c_im, sgu_w, w_glu, pool_w, S):
    L, G, P = a_re.shape
    H = b_re.shape[-1]
    GP, W = G * P, G * H
    assert P & (P - 1) == 0 and H & (H - 1) == 0
    bre_t = jnp.swapaxes(b_re, 2, 3).reshape(L, W, P)
    bim_t = jnp.swapaxes(b_im, 2, 3).reshape(L, W, P)
    cre_t = jnp.swapaxes(c_re, 2, 3).reshape(L, GP, H)
    cim_t = jnp.swapaxes(c_im, 2, 3).reshape(L, GP, H)
    are2 = a_re.reshape(L, 1, GP)
    aim2 = a_im.reshape(L, 1, GP)
    ldt2 = jnp.repeat(log_dt, P, axis=-1).reshape(L, 1, GP)

    def lspec(shape):
        nd = len(shape)
        return pl.BlockSpec((None,) + tuple(shape[1:]), lambda l: (l,) + (0,) * (nd - 1))

    ins = [are2, aim2, ldt2, bre_t, bim_t, cre_t, cim_t, sgu_w, w_glu, pool_w]
    out_shapes = [
        jax.ShapeDtypeStruct((L, 1, GP), F32), jax.ShapeDtypeStruct((L, 1, GP), F32),
        jax.ShapeDtypeStruct((L, SUBLANES * S, GP), F32), jax.ShapeDtypeStruct((L, SUBLANES * S, GP), F32),
        jax.ShapeDtypeStruct((L, W, GP), BF16), jax.ShapeDtypeStruct((L, W, GP), BF16),
        jax.ShapeDtypeStruct((L, GP, W), BF16), jax.ShapeDtypeStruct((L, GP, W), BF16),
        jax.ShapeDtypeStruct(sgu_w.shape, BF16), jax.ShapeDtypeStruct(w_glu.shape, BF16),
        jax.ShapeDtypeStruct(pool_w.shape, BF16),
    ]
    return pl.pallas_call(
        functools.partial(_prep_kernel, S=S, P=P, H=H),
        out_shape=out_shapes,
        grid=(L,),
        in_specs=[lspec(a.shape) for a in ins],
        out_specs=[lspec(o.shape) for o in out_shapes],
        compiler_params=_cparams(1),
        name="ssm_prep",
    )(*ins)


PROJ_CHUNK = 256


def _stream_weight_cols(w_hbm, l, n_cols, stage, sem, consume):
    def copy(c, slot):
        return pltpu.make_async_copy(w_hbm.at[l, :, pl.ds(c * PROJ_CHUNK, PROJ_CHUNK)], stage.at[slot],
                                     sem.at[slot])

    n = n_cols // PROJ_CHUNK
    copy(0, 0).start()
    for c in range(n):
        slot = c % 2
        if c + 1 < n:
            copy(c + 1, 1 - slot).start()
        copy(c, slot).wait()
        consume(c, slot)


N_MIXER_INPUTS = 23


def _mixer_prompt_kernel(*refs, n_batch, **kw):
    o_ref = refs[N_MIXER_INPUTS]
    b = pl.program_id(0)

    @pl.when(b < n_batch)
    def _():
        _mixer_prompt_body(*refs, **kw)

    @pl.when((b == n_batch) & (pl.program_id(1) == 0))
    def _():
        o_ref[...] = jnp.zeros_like(o_ref)


def _mixer_prompt_body(h_ref, w_hbm, lng, lnb, sguw, sgub, cw, cb, clg, clb,
                       lamr, lami, pwr, pwi, bbr, bbi, cr, ci, sd, wglu, bglu, poolw, pscale,
                       o_ref, conv_o, pool_o, sre_o, sim_o,
                       w6bf, wstage, wsem, p_ref,
                       gbuf, gsh, pbuf, xr, xi, car_re, car_im, fin_re, fin_im, cm_re, cm_im,
                       *, l, tt, S, W, GP):
    j = pl.program_id(1)
    nt = pl.num_programs(1)
    GH = 32
    PH = 16

    @pl.when((pl.program_id(0) == 0) & (j == 0))
    def _():
        def consume(c, slot):
            _cast_rows(wstage.at[slot], w6bf.at[:, c * PROJ_CHUNK:(c + 1) * PROJ_CHUNK], 256)

        _stream_weight_cols(w_hbm, l, 6 * W, wstage, wsem, consume)

    @pl.when(j == 0)
    def _():
        gbuf[0:GH, :] = jnp.zeros((GH, W), F32)
        pbuf[0:PH, :] = jnp.zeros((PH, W), F32)
        car_re[...] = jnp.zeros_like(car_re)
        car_im[...] = jnp.zeros_like(car_im)

    h = h_ref[...]
    for c in range(3):
        p_ref[:, 2 * c * W:2 * (c + 1) * W] = _dot(h, w6bf[:, 2 * c * W:2 * (c + 1) * W])

    for c in range(tt // CHUNK):
        rows = slice(c * CHUNK, (c + 1) * CHUNK)
        u = _gelu(p_ref[rows, 0:W])
        v = _layernorm(_gelu(p_ref[rows, W:2 * W]), lng[...], lnb[...])
        vb = v.astype(BF16)
        gw = W // SGU_GROUPS
        for g in range(SGU_GROUPS):
            cols = slice(g * gw, (g + 1) * gw)
            mixed = _dot(sguw[g], vb[:, cols]) + sgub[:, g:g + 1]
            o_ref[rows, cols] = (u[:, cols] * mixed).astype(BF16)

    rc = 64
    for c in range(tt // rc):
        rows = slice(c * rc, (c + 1) * rc)
        gbuf[GH + c * rc:GH + (c + 1) * rc, :] = p_ref[rows, 2 * W:3 * W] * _sigmoid(p_ref[rows, 3 * W:4 * W])
    sh_rows = gsh.shape[1]
    for b in range(1, SUBLANES):
        gsh[b] = gbuf[b:b + sh_rows, :]
    rc = 32
    base = GH - (CONV_K - 1)
    for c in range(tt // rc):
        acc = jnp.zeros((rc, W), F32)
        for k in range(CONV_K):
            r0 = base + c * rc + k
            b, a0 = r0 % SUBLANES, r0 - r0 % SUBLANES
            rows = gbuf[a0:a0 + rc, :] if b == 0 else gsh[b, a0:a0 + rc, :]
            wk = jnp.concatenate([cw[SUBLANES * k:SUBLANES * (k + 1), :]] * (rc // SUBLANES), axis=0)
            acc = acc + wk * rows
        y = _layernorm(acc + cb[...], clg[...], clb[...])
        o_ref[c * rc:(c + 1) * rc, W:2 * W] = (y * _sigmoid(y)).astype(BF16)
    gbuf[0:GH, :] = gbuf[tt:tt + GH, :]

    uc = p_ref[:, 4 * W:5 * W]
    r_i = lax.broadcasted_iota(I32, (tt, tt), 0)
    c_i = lax.broadcasted_iota(I32, (tt, tt), 1)
    perm = jnp.where(((r_i & (SUBLANES - 1)) * S + (r_i >> 3)) == c_i, 1.0, 0.0).astype(BF16)
    up = _dot(perm, uc.astype(BF16)).astype(BF16)
    hw, hg = W // 2, GP // 2
    for q in range(2):
        xr[:, q * hg:(q + 1) * hg] = _dot(up[:, q * hw:(q + 1) * hw], bbr[q * hw:(q + 1) * hw, q * hg:(q + 1) * hg])
        xi[:, q * hg:(q + 1) * hg] = _dot(up[:, q * hw:(q + 1) * hw], bbi[q * hw:(q + 1) * hw, q * hg:(q + 1) * hg])

    for cbi in range(GP // SCAN_COLS):
        cols = slice(cbi * SCAN_COLS, (cbi + 1) * SCAN_COLS)
        lr = jnp.broadcast_to(lamr[:, cols], (SUBLANES, SCAN_COLS))
        li = jnp.broadcast_to(lami[:, cols], (SUBLANES, SCAN_COLS))

        def step(s, carry, cols=cols, lr=lr, li=li):
            sr, si = carry
            r0 = pl.multiple_of(s * SUBLANES, SUBLANES)
            nr = lr * sr - li * si + xr[pl.ds(r0, SUBLANES), cols]
            ni = lr * si + li * sr + xi[pl.ds(r0, SUBLANES), cols]
            xr[pl.ds(r0, SUBLANES), cols] = nr
            xi[pl.ds(r0, SUBLANES), cols] = ni
            return nr, ni

        z = jnp.zeros((SUBLANES, SCAN_COLS), F32)
        fr, fi = lax.fori_loop(0, S, step, (z, z), unroll=True)
        fin_re[:, cols] = fr
        fin_im[:, cols] = fi

    lsr = pwr[SUBLANES * (S - 1):SUBLANES * (S - 1) + 1, :]
    lsi = pwi[SUBLANES * (S - 1):SUBLANES * (S - 1) + 1, :]
    c_r = car_re[0:1, :]
    c_im = car_im[0:1, :]
    cm_re[0:1, :] = c_r
    cm_im[0:1, :] = c_im
    for q in range(1, SUBLANES):
        f_r = fin_re[q - 1:q, :]
        f_i = fin_im[q - 1:q, :]
        c_r, c_im = f_r + lsr * c_r - lsi * c_im, f_i + lsr * c_im + lsi * c_r
        cm_re[q:q + 1, :] = c_r
        cm_im[q:q + 1, :] = c_im
    n_r = fin_re[SUBLANES - 1:SUBLANES, :] + lsr * c_r - lsi * c_im
    n_i = fin_im[SUBLANES - 1:SUBLANES, :] + lsr * c_im + lsi * c_r
    car_re[0:1, :] = n_r
    car_im[0:1, :] = n_i

    for cbi in range(GP // SCAN_COLS):
        cols = slice(cbi * SCAN_COLS, (cbi + 1) * SCAN_COLS)
        mr = cm_re[:, cols]
        mi = cm_im[:, cols]

        def fix(s, c, cols=cols, mr=mr, mi=mi):
            r0 = pl.multiple_of(s * SUBLANES, SUBLANES)
            pr = pwr[pl.ds(r0, SUBLANES), cols]
            pi = pwi[pl.ds(r0, SUBLANES), cols]
            xr[pl.ds(r0, SUBLANES), cols] = xr[pl.ds(r0, SUBLANES), cols] + (pr * mr - pi * mi)
            xi[pl.ds(r0, SUBLANES), cols] = xi[pl.ds(r0, SUBLANES), cols] + (pr * mi + pi * mr)
            return c

        lax.fori_loop(0, S, fix, 0, unroll=True)

    yp = jnp.concatenate(
        [_dot(xr[:, q * hg:(q + 1) * hg].astype(BF16), cr[q * hg:(q + 1) * hg, q * hw:(q + 1) * hw])
         - _dot(xi[:, q * hg:(q + 1) * hg].astype(BF16), ci[q * hg:(q + 1) * hg, q * hw:(q + 1) * hw])
         for q in range(2)], axis=1)
    unperm = jnp.where(((c_i & (SUBLANES - 1)) * S + (c_i >> 3)) == r_i, 1.0, 0.0).astype(BF16)
    y1 = yp.astype(BF16)
    r1 = yp - y1.astype(F32)
    y2 = r1.astype(BF16)
    y3 = (r1 - y2.astype(F32)).astype(BF16)
    y = _dot(unperm, y1) + _dot(unperm, y2) + _dot(unperm, y3)
    z = _gelu(y + sd[...] * uc)
    gl = _dot(z.astype(BF16), wglu[...]) + bglu[...]
    o_ref[:, 2 * W:3 * W] = (z * _sigmoid(gl)).astype(BF16)

    xd = p_ref[:, 5 * W:6 * W]
    pbuf[PH:PH + tt, :] = xd
    gw = W // len(POOL_WINDOWS)
    pos = (j * tt + lax.broadcasted_iota(I32, (tt, 1), 0) + 1).astype(F32)
    for gi, win in enumerate(POOL_WINDOWS):
        cols = slice(gi * gw, (gi + 1) * gw)
        s = xd[:, cols]
        for i in range(1, win):
            s = s + pbuf[PH - i:PH - i + tt, cols]
        cnt = jnp.minimum(pos, float(win))
        pooled = s / cnt - xd[:, cols]
        mixed = _dot(pooled.astype(BF16), poolw[gi]) * pscale[:, cols]
        o_ref[:, 3 * W + gi * gw:3 * W + (gi + 1) * gw] = mixed.astype(BF16)
    pbuf[0:PH, :] = pbuf[tt:tt + PH, :]

    @pl.when(j == nt - 1)
    def _():
        conv_o[...] = gbuf[GH - (CONV_K - 1):GH, :]
        pool_o[...] = pbuf[PH - POOL_HIST:PH, :]
        sre_o[...] = car_re[0:1, :]
        sim_o[...] = car_im[0:1, :]


def _mixer_prompt(h, w_in, B, L, T, l, small, prep, cr, ci):
    W = small["sgu_ln_g"].shape[-1]
    GP = prep["lamr"].shape[-1]
    D = h.shape[1]
    D4 = 4 * W
    tt = TIME_TILE
    S = tt // SUBLANES
    nt = L // tt
    ins = [h, w_in,
           small["sgu_ln_g"], small["sgu_ln_b"], prep["sguw"], small["sgu_bT"],
           small["conv_w8"], small["conv_b"], small["conv_ln_g"], small["conv_ln_b"],
           prep["lamr"], prep["lami"], prep["pwr"], prep["pwi"], prep["bbr"], prep["bbi"], cr, ci,
           small["ssm_d"], prep["wglu"], small["ssm_b_glu"], prep["poolw"], small["pool_scale"]]
    assert len(ins) == N_MIXER_INPUTS
    in_specs = [pl.BlockSpec((tt, D), lambda b, j: (jnp.minimum(b * nt + j, B * nt), 0)),
                pl.BlockSpec(memory_space=pl.ANY)]
    in_specs += [_layer_spec(a, l, single=True) for a in ins[2:]]
    out_shapes = [
        jax.ShapeDtypeStruct((T, D4), BF16),
        jax.ShapeDtypeStruct((B, CONV_K - 1, W), F32),
        jax.ShapeDtypeStruct((B, POOL_HIST, W), F32),
        jax.ShapeDtypeStruct((B, 1, GP), F32),
        jax.ShapeDtypeStruct((B, 1, GP), F32),
    ]
    out_specs = [
        pl.BlockSpec((tt, D4), lambda b, j: (jnp.minimum(b * nt + j, B * nt), 0)),
        pl.BlockSpec((None, CONV_K - 1, W), lambda b, j: (jnp.minimum(b, B - 1), 0, 0)),
        pl.BlockSpec((None, POOL_HIST, W), lambda b, j: (jnp.minimum(b, B - 1), 0, 0)),
        pl.BlockSpec((None, 1, GP), lambda b, j: (jnp.minimum(b, B - 1), 0, 0)),
        pl.BlockSpec((None, 1, GP), lambda b, j: (jnp.minimum(b, B - 1), 0, 0)),
    ]
    scratch = [
        pltpu.VMEM((D, 6 * W), BF16), pltpu.VMEM((2, D, PROJ_CHUNK), F32), pltpu.SemaphoreType.DMA((2,)),
        pltpu.VMEM((tt, 6 * W), F32),
        pltpu.VMEM((32 + tt, W), F32), pltpu.VMEM((SUBLANES, 32 + tt - SUBLANES, W), F32),
        pltpu.VMEM((16 + tt, W), F32),
        pltpu.VMEM((tt, GP), F32), pltpu.VMEM((tt, GP), F32),
        pltpu.VMEM((SUBLANES, GP), F32), pltpu.VMEM((SUBLANES, GP), F32),
        pltpu.VMEM((SUBLANES, GP), F32), pltpu.VMEM((SUBLANES, GP), F32),
        pltpu.VMEM((SUBLANES, GP), F32), pltpu.VMEM((SUBLANES, GP), F32),
    ]
    return pl.pallas_call(
        functools.partial(_mixer_prompt_kernel, n_batch=B, l=l, tt=tt, S=S, W=W, GP=GP),
        out_shape=out_shapes,
        grid=(B + 1, nt),
        in_specs=in_specs,
        out_specs=out_specs,
        scratch_shapes=scratch,
        compiler_params=_cparams(2),
        name="mixer_prompt",
    )(*ins)


def _mixer_sample_kernel(h_ref, w_hbm, stc, stp, hre, him, lng, lnb, w00, b0, cw, cb, clg, clb,
                         lamr, lami, bbr, bbi, cr, ci, sd, wglu, bglu, poolw, pscale, o_in,
                         o_ref, g_o, v_o, d_o, xre_o, xim_o,
                         p_ref, wstage, wchunk, wsem, *, l, W):
    del o_in
    h = h_ref[...]

    def consume(c, slot):
        _cast_rows(wstage.at[slot], wchunk, 256)
        p_ref[:, c * PROJ_CHUNK:(c + 1) * PROJ_CHUNK] = _dot(h, wchunk[...])

    _stream_weight_cols(w_hbm, l, 6 * W, wstage, wsem, consume)

    u = _gelu(p_ref[:, 0:W])
    v = _layernorm(_gelu(p_ref[:, W:2 * W]), lng[...], lnb[...])
    v_o[...] = v
    o_ref[:, 0:W] = (u * (w00[...] * v + b0[...])).astype(BF16)

    g = p_ref[:, 2 * W:3 * W] * _sigmoid(p_ref[:, 3 * W:4 * W])
    g_o[...] = g
    acc = cw[CONV_K - 1:CONV_K, :] * g
    for k in range(CONV_K - 1):
        acc = acc + cw[k:k + 1, :] * stc[:, k * W:(k + 1) * W]
    y = _layernorm(acc + cb[...], clg[...], clb[...])
    o_ref[:, W:2 * W] = (y * _sigmoid(y)).astype(BF16)

    uc = p_ref[:, 4 * W:5 * W]
    ub = uc.astype(BF16)
    h_r = hre[...]
    h_i = him[...]
    l_r = lamr[...]
    l_i = lami[...]
    x_r = l_r * h_r - l_i * h_i + _dot(ub, bbr[...])
    x_i = l_r * h_i + l_i * h_r + _dot(ub, bbi[...])
    xre_o[...] = x_r
    xim_o[...] = x_i
    yv = _dot(x_r.astype(BF16), cr[...]) - _dot(x_i.astype(BF16), ci[...]) + sd[...] * uc
    z = _gelu(yv)
    gl = _dot(z.astype(BF16), wglu[...]) + bglu[...]
    o_ref[:, 2 * W:3 * W] = (z * _sigmoid(gl)).astype(BF16)

    xd = p_ref[:, 5 * W:6 * W]
    d_o[...] = xd
    gw = W // len(POOL_WINDOWS)
    for gi, win in enumerate(POOL_WINDOWS):
        cols = slice(gi * gw, (gi + 1) * gw)
        s = xd[:, cols]
        for i in range(1, win):
            r = POOL_HIST - i
            s = s + stp[:, r * W + gi * gw:r * W + (gi + 1) * gw]
        cnt = float(min(PAST_LEN + 1, win))
        pooled = s / cnt - xd[:, cols]
        mixed = _dot(pooled.astype(BF16), poolw[gi]) * pscale[:, cols]
        o_ref[:, 3 * W + gi * gw:3 * W + (gi + 1) * gw] = mixed.astype(BF16)


def _mixer_sample(h, w_in, o_prev, stc2, stp2, hre2, him2, DB, T, l, small, prep, cr, ci):
    W = small["sgu_ln_g"].shape[-1]
    GP = prep["lamr"].shape[-1]
    D = h.shape[1]
    D4 = 4 * W
    blk = (T - DB) // DB
    ins = [h, w_in, stc2, stp2, hre2, him2,
           small["sgu_ln_g"], small["sgu_ln_b"], small["sgu_w00"], small["sgu_b0"],
           small["conv_w"], small["conv_b"], small["conv_ln_g"], small["conv_ln_b"],
           prep["lamr"], prep["lami"], prep["bbr"], prep["bbi"], cr, ci,
           small["ssm_d"], prep["wglu"], small["ssm_b_glu"], prep["poolw"], small["pool_scale"], o_prev]
    in_specs = [pl.BlockSpec((DB, D), lambda i: (blk, 0)), pl.BlockSpec(memory_space=pl.ANY)]
    in_specs += [_layer_spec(a, l, single=True) for a in ins[2:-1]]
    in_specs += [pl.BlockSpec(memory_space=pl.ANY)]
    out_shapes = [
        jax.ShapeDtypeStruct((T, D4), BF16),
        jax.ShapeDtypeStruct((DB, W), F32), jax.ShapeDtypeStruct((DB, W), F32), jax.ShapeDtypeStruct((DB, W), F32),
        jax.ShapeDtypeStruct((DB, GP), F32), jax.ShapeDtypeStruct((DB, GP), F32),
    ]
    out_specs = [
        pl.BlockSpec((DB, D4), lambda i: (blk, 0)),
        pl.BlockSpec((DB, W), lambda i: (0, 0)), pl.BlockSpec((DB, W), lambda i: (0, 0)),
        pl.BlockSpec((DB, W), lambda i: (0, 0)),
        pl.BlockSpec((DB, GP), lambda i: (0, 0)), pl.BlockSpec((DB, GP), lambda i: (0, 0)),
    ]
    return pl.pallas_call(
        functools.partial(_mixer_sample_kernel, l=l, W=W),
        out_shape=out_shapes,
        grid=(1,),
        in_specs=in_specs,
        out_specs=out_specs,
        scratch_shapes=[pltpu.VMEM((DB, 6 * W), F32), pltpu.VMEM((2, D, PROJ_CHUNK), F32),
                        pltpu.VMEM((D, PROJ_CHUNK), BF16), pltpu.SemaphoreType.DMA((2,))],
        input_output_aliases={len(ins) - 1: 0},
        compiler_params=_cparams(1),
        name="mixer_sample",
    )(*ins)


def _merge_kernel(h_ref, o_ref, wg0, wg1, wg2, wg3, wb_ref, m_ref, wgbf, wbbf, *, W):
    wgs = (wg0, wg1, wg2, wg3)

    @pl.when(pl.program_id(1) == 0)
    def _():
        for b in range(4):
            _cast_rows(wgs[b], wgbf.at[b], 256)
            wbbf[b] = wb_ref[b].astype(BF16)

    h = h_ref[...]
    acc = None
    for b in range(4):
        gate = _sigmoid(_dot(h, wgbf[b]))
        term = gate * _dot(o_ref[:, b * W:(b + 1) * W], wbbf[b])
        acc = term if acc is None else acc + term
    m_ref[...] = acc.astype(BF16)


def _merge(h, o, w_in, w_branch, l, gate_off):
    T, D = h.shape
    W = w_branch.shape[2]
    tm = _pick(T, (640, 512, 384, 256, 128))
    tn = 256
    nb = D // tn

    def gate_spec(b):
        blk0 = (gate_off + b * D) // tn
        return pl.BlockSpec((None, D, tn), lambda n, m: (l, 0, blk0 + n))

    return pl.pallas_call(
        functools.partial(_merge_kernel, W=W),
        out_shape=jax.ShapeDtypeStruct((T, D), BF16),
        grid=(nb, T // tm),
        in_specs=[pl.BlockSpec((tm, D), lambda n, m: (m, 0)),
                  pl.BlockSpec((tm, 4 * W), lambda n, m: (m, 0)),
                  gate_spec(0), gate_spec(1), gate_spec(2), gate_spec(3),
                  pl.BlockSpec((None, 4, W, tn), lambda n, m: (l, 0, 0, n))],
        out_specs=pl.BlockSpec((tm, tn), lambda n, m: (m, n)),
        scratch_shapes=[pltpu.VMEM((4, D, tn), BF16), pltpu.VMEM((4, W, tn), BF16)],
        compiler_params=_cparams(2),
        name="gated_merge",
    )(h, o, w_in, w_in, w_in, w_in, w_branch)


def _out_kernel(m_ref, x_ref, w_ref, g_ref, wr_ref, x1_ref, hp_ref, lg_ref, wbf, wr2):
    R = wr_ref.shape[-1]

    @pl.when(pl.program_id(0) == 0)
    def _():
        _cast_rows(w_ref, wbf, 256)
        wr = wr_ref[...]
        hi = wr.astype(BF16)
        wr2[:, 0:R] = hi
        wr2[:, R:2 * R] = (wr - hi.astype(F32)).astype(BF16)

    x1 = x_ref[...] + _dot(m_ref[...], wbf[...])
    x1_ref[...] = x1
    h = _rms_scale(x1, g_ref[...])
    hb = h.astype(BF16)
    hl = (h - hb.astype(F32)).astype(BF16)
    both = _dot(hb, wr2[...])
    lg_ref[...] = both[:, 0:R] + both[:, R:2 * R] + _dot(hl, wr2[:, 0:R])
    bits = lax.bitcast_convert_type(hb.astype(F32), U32)
    half = bits.shape[1] // 2
    hp_ref[...] = (bits[:, :half] >> 16) | (bits[:, half:] & jnp.uint32(0xFFFF0000))


def _out_proj(merged, x, w_out, g_all, wr_all, l):
    T, D = x.shape
    tm = _pick(T, (320, 256, 128))
    R = wr_all.shape[-1]
    return pl.pallas_call(
        _out_kernel,
        out_shape=[jax.ShapeDtypeStruct((T, D), F32), jax.ShapeDtypeStruct((T, D // 2), U32),
                   jax.ShapeDtypeStruct((T, R), F32)],
        grid=(T // tm,),
        in_specs=[pl.BlockSpec((tm, D), lambda m: (m, 0)),
                  pl.BlockSpec((tm, D), lambda m: (m, 0)),
                  pl.BlockSpec((None, D, D), lambda m: (l, 0, 0), pipeline_mode=pl.Buffered(1)),
                  _layer_spec(g_all, l), _layer_spec(wr_all, l)],
        out_specs=[pl.BlockSpec((tm, D), lambda m: (m, 0)),
                   pl.BlockSpec((tm, D // 2), lambda m: (m, 0)),
                   pl.BlockSpec((tm, R), lambda m: (m, 0))],
        scratch_shapes=[pltpu.VMEM((D, D), BF16), pltpu.VMEM((D, 2 * R), BF16)],
        compiler_params=_cparams(1),
        name="out_proj",
    )(merged, x, w_out, g_all, wr_all)


def _route_kernel(lg_ref, bias_ref, rt_ref, meta_ref, cnt, off, *, tm, tile, n_tiles_max):
    p = pl.program_id(0)
    m = pl.program_id(1)
    lane = lax.broadcasted_iota(I32, (tm, LANES), 1).astype(F32)
    neg = jnp.float32(-jnp.inf)
    big = jnp.float32(1e9)

    @pl.when((p == 0) & (m == 0))
    def _():
        cnt[...] = jnp.zeros_like(cnt)

    lg = lg_ref[...] + bias_ref[...]
    is_g = lane < N_GROUPS
    gl = jnp.where(is_g, lg, neg)
    gmax = jnp.max(gl, axis=-1, keepdims=True)
    gidx = jnp.min(jnp.where(gl == gmax, lane, big), axis=-1, keepdims=True)
    gsum = jnp.sum(jnp.where(is_g, jnp.exp(gl - gmax), 0.0), axis=-1, keepdims=True)
    g_w = 1.0 / gsum
    lo = N_GROUPS + EXP_PER_GROUP * gidx
    in_grp = (lane >= lo) & (lane < lo + EXP_PER_GROUP)
    el = jnp.where(in_grp, lg, neg)
    v1 = jnp.max(el, axis=-1, keepdims=True)
    i1 = jnp.min(jnp.where(el == v1, lane, big), axis=-1, keepdims=True)
    el2 = jnp.where(lane == i1, neg, el)
    v2 = jnp.max(el2, axis=-1, keepdims=True)
    i2 = jnp.min(jnp.where(el2 == v2, lane, big), axis=-1, keepdims=True)
    e2x = jnp.exp(v2 - v1)
    w1 = g_w / (1.0 + e2x)
    w2 = g_w * e2x / (1.0 + e2x)
    e1 = i1 - N_GROUPS
    e2 = i2 - N_GROUPS
    a1 = jnp.where(lane == e1, 1.0, 0.0)
    a2 = jnp.where(lane == e2, 1.0, 0.0)
    a = a1 + a2

    @pl.when(p == 0)
    def _():
        cnt[0:1, :] = cnt[0:1, :] + jnp.sum(a, axis=0, keepdims=True)

    @pl.when((p == 1) & (m == 0))
    def _():
        counts = cnt[0:1, :]
        tiles = jnp.floor((counts + (tile - 1)) * (1.0 / tile))
        tiles8 = jnp.broadcast_to(tiles, (SUBLANES, LANES)).astype(BF16)
        ri = lax.broadcasted_iota(I32, (LANES, LANES), 0)
        ci = lax.broadcasted_iota(I32, (LANES, LANES), 1)
        upper = jnp.where(ri < ci, 1.0, 0.0).astype(BF16)
        toff = _dot(tiles8, upper)[0:1, :]
        off[0:1, :] = toff * tile
        tend = toff + tiles
        lane1 = lax.broadcasted_iota(I32, (1, LANES), 1).astype(F32)
        n_act = jnp.sum(jnp.where(lane1 == N_EXPERTS - 1, tend, 0.0), axis=-1, keepdims=True)
        texp = jnp.zeros((1, LANES), F32)
        for e in range(N_EXPERTS - 1):
            end_e = jnp.sum(jnp.where(lane1 == e, tend, 0.0), axis=-1, keepdims=True)
            texp = texp + jnp.where(jnp.minimum(lane1, n_act - 1.0) >= end_e, 1.0, 0.0)
        meta_ref[0:1, :] = counts
        meta_ref[1:2, :] = off[0:1, :]
        meta_ref[2:3, :] = texp
        meta_ref[3:4, :] = jnp.broadcast_to(n_act, (1, LANES))
        meta_ref[4:5, :] = tend
        meta_ref[5:8, :] = jnp.zeros((3, LANES), F32)
        cnt[...] = jnp.zeros_like(cnt)

    @pl.when(p == 1)
    def _():
        ri = lax.broadcasted_iota(I32, (tm, tm), 0)
        ci = lax.broadcasted_iota(I32, (tm, tm), 1)
        ltri = jnp.where(ci < ri, 1.0, 0.0).astype(BF16)
        cum = _dot(ltri, a.astype(BF16)) + cnt[0:1, :] + off[0:1, :]
        pos1 = jnp.sum(a1 * cum, axis=-1, keepdims=True)
        pos2 = jnp.sum(a2 * cum, axis=-1, keepdims=True)
        cnt[0:1, :] = cnt[0:1, :] + jnp.sum(a, axis=0, keepdims=True)
        rt = jnp.where(lane == 0, e1, 0.0)
        rt = jnp.where(lane == 1, e2, rt)
        rt = jnp.where(lane == 2, w1, rt)
        rt = jnp.where(lane == 3, w2, rt)
        rt = jnp.where(lane == 4, pos1, rt)
        rt = jnp.where(lane == 5, pos2, rt)
        rt_ref[...] = rt


def _route(logits, bias_all, l, tile, n_tiles_max):
    T, R = logits.shape
    tm = _pick(T, (640, 512, 384, 256, 128))
    return pl.pallas_call(
        functools.partial(_route_kernel, tm=tm, tile=tile, n_tiles_max=n_tiles_max),
        out_shape=[jax.ShapeDtypeStruct((T, R), F32), jax.ShapeDtypeStruct((SUBLANES, LANES), F32)],
        grid=(2, T // tm),
        in_specs=[pl.BlockSpec((tm, R), lambda p, m: (m, 0)), _layer_spec(bias_all, l)],
        out_specs=[pl.BlockSpec((tm, R), lambda p, m: (m * p, 0)),
                   pl.BlockSpec((SUBLANES, LANES), lambda p, m: (0, 0))],
        scratch_shapes=[pltpu.VMEM((SUBLANES, LANES), F32), pltpu.VMEM((SUBLANES, LANES), F32)],
        compiler_params=_cparams(2),
        name="route",
    )(logits, bias_all)


def _dispatch_kernel(pos_ref, tend_ref, hp_ref, hs_ref, stage, zbuf, sem, zsem, *, T, tk, tile, n_tiles):
    i = pl.program_id(0)
    n = pl.num_programs(0)

    @pl.when(i == 0)
    def _():
        zbuf[...] = jnp.zeros_like(zbuf)

        def last_tile_copy(ex):
            end = tend_ref[ex]
            start = tend_ref[ex - 1] if ex > 0 else 0
            row0 = pl.multiple_of((end - 1) * tile, tile)
            return end > start, pltpu.make_async_copy(zbuf, hs_ref.at[pl.ds(row0, tile)], zsem)

        def spare_tile_copy(k):
            idx = tend_ref[N_EXPERTS - 1] + k
            row0 = pl.multiple_of(jnp.minimum(idx, n_tiles - 1) * tile, tile)
            return idx < n_tiles, pltpu.make_async_copy(zbuf, hs_ref.at[pl.ds(row0, tile)], zsem)

        fills = [last_tile_copy(ex) for ex in range(N_EXPERTS)] + [spare_tile_copy(k) for k in range(N_EXPERTS)]
        for go, cp in fills:
            @pl.when(go)
            def _(cp=cp):
                cp.start()

        for go, cp in fills:
            @pl.when(go)
            def _(cp=cp):
                cp.wait()

    def copies(step, slot):
        out = []
        for u in range(tk):
            t = step * tk + u
            for k in range(2):
                p = pos_ref[k * T + t]
                out.append(pltpu.make_async_copy(stage.at[slot, pl.ds(u, 1)], hs_ref.at[pl.ds(p, 1)],
                                                 sem.at[slot]))
        return out

    def wait_slot(s):
        rows = hs_ref.at[pl.ds(0, 2 * tk)]
        pltpu.make_async_copy(rows, rows, sem.at[s]).wait()

    slot = lax.rem(i, 2)
    for s in range(2):
        @pl.when(slot == s)
        def _(s=s):
            stage[s] = hp_ref[...]
            for cp in copies(i, s):
                cp.start()

        @pl.when((slot == 1 - s) & (i > 0))
        def _(s=s):
            wait_slot(s)

        @pl.when((slot == s) & (i == n - 1))
        def _(s=s):
            wait_slot(s)


def _dispatch(pos_flat, tend, hp, n_pad, tile):
    T, Dh = hp.shape
    tk = 128
    return pl.pallas_call(
        functools.partial(_dispatch_kernel, T=T, tk=tk, tile=tile, n_tiles=n_pad // tile),
        out_shape=jax.ShapeDtypeStruct((n_pad, Dh), U32),
        grid_spec=pltpu.PrefetchScalarGridSpec(
            num_scalar_prefetch=2,
            grid=(T // tk,),
            in_specs=[pl.BlockSpec((tk, Dh), lambda i, pos, tend: (i, 0))],
            out_specs=pl.BlockSpec(memory_space=pl.ANY),
            scratch_shapes=[pltpu.VMEM((2, tk, Dh), U32), pltpu.VMEM((tile, Dh), U32),
                            pltpu.SemaphoreType.DMA((2,)), pltpu.SemaphoreType.DMA],
        ),
        compiler_params=_cparams(1),
        name="dispatch",
    )(pos_flat, tend, hp)


def _expert_kernel(te_ref, na_ref, hs_ref, wg_ref, wu_ref, wd_ref, ys_ref,
                   wgst, wust, wdst, wgbf, wubf, wdbf, sem, *, l):
    i = pl.program_id(0)
    na = na_ref[0]
    e = te_ref[i]
    first = (i < na) & ((i == 0) | (e != te_ref[jnp.maximum(i - 1, 0)]))

    def weight_copies(ex):
        return (pltpu.make_async_copy(wg_ref.at[l, ex], wgst, sem),
                pltpu.make_async_copy(wu_ref.at[l, ex], wust, sem),
                pltpu.make_async_copy(wd_ref.at[l, ex], wdst, sem))

    @pl.when(i == 0)
    def _():
        for cp in weight_copies(e):
            cp.start()

    @pl.when(first)
    def _():
        for cp in weight_copies(e):
            cp.wait()
        _cast_rows(wgst, wgbf, 256)
        _cast_rows(wust, wubf, 256)
        _cast_rows(wdst, wdbf, 256)
        j = lax.while_loop(lambda j: (j < na) & (te_ref[jnp.minimum(j, na - 1)] == e), lambda j: j + 1, i + 1)

        @pl.when(j < na)
        def _():
            for cp in weight_copies(te_ref[jnp.minimum(j, na - 1)]):
                cp.start()

    @pl.when(i < na)
    def _():
        w = hs_ref[...]
        half = w.shape[1]
        lo = lax.bitcast_convert_type(w << 16, F32).astype(BF16)
        hi = lax.bitcast_convert_type(w & jnp.uint32(0xFFFF0000), F32).astype(BF16)
        a = _dot(lo, wgbf[0:half, :]) + _dot(hi, wgbf[half:, :])
        b = _dot(lo, wubf[0:half, :]) + _dot(hi, wubf[half:, :])
        hid = (a * _sigmoid(a) * b).astype(BF16)
        ys_ref[...] = _dot(hid, wdbf[...])

    @pl.when(i >= na)
    def _():
        ys_ref[...] = jnp.zeros_like(ys_ref)


def _experts(te, na, hs, wg, wu, wd, l, tile, n_tiles_max):
    n_pad, Dh = hs.shape
    D = 2 * Dh
    F = wg.shape[-1]

    def row_map(i, te_ref, na_ref):
        return (jnp.minimum(i, na_ref[0] - 1), 0)

    any_spec = pl.BlockSpec(memory_space=pl.ANY)
    return pl.pallas_call(
        functools.partial(_expert_kernel, l=l),
        out_shape=jax.ShapeDtypeStruct((n_pad, D), F32),
        grid_spec=pltpu.PrefetchScalarGridSpec(
            num_scalar_prefetch=2,
            grid=(n_tiles_max,),
            in_specs=[pl.BlockSpec((tile, Dh), row_map), any_spec, any_spec, any_spec],
            out_specs=pl.BlockSpec((tile, D), lambda i, te_ref, na_ref: (i, 0)),
            scratch_shapes=[pltpu.VMEM((D, F), F32), pltpu.VMEM((D, F), F32), pltpu.VMEM((F, D), F32),
                            pltpu.VMEM((D, F), BF16), pltpu.VMEM((D, F), BF16), pltpu.VMEM((F, D), BF16),
                            pltpu.SemaphoreType.DMA],
        ),
        compiler_params=_cparams(1),
        name="experts",
    )(te, na, hs, wg, wu, wd)


def _combine_kernel(pos_ref, x1_ref, rt_ref, g_ref, ys_ref, *rest, T, tk, n_prompt):
    final = n_prompt is not None
    out_a, out_b, buf, sem = rest
    i = pl.program_id(0)
    n = pl.num_programs(0)

    def copies(step, slot):
        out = []
        for u in range(tk):
            t = step * tk + u
            for k in range(2):
                p = pos_ref[k * T + t]
                out.append(pltpu.make_async_copy(ys_ref.at[pl.ds(p, 1)], buf.at[slot, k, pl.ds(u, 1)],
                                                 sem.at[slot]))
        return out

    slot = lax.rem(i, 2)

    @pl.when(i == 0)
    def _():
        for cp in copies(0, 0):
            cp.start()

    for s in range(2):
        @pl.when((i + 1 < n) & (slot == 1 - s))
        def _(s=s):
            for cp in copies(i + 1, s):
                cp.start()

    for s in range(2):
        @pl.when(slot == s)
        def _(s=s):
            pltpu.make_async_copy(buf.at[s], buf.at[s], sem.at[s]).wait()
            rt = rt_ref[...]
            w1 = rt[:, 2:3]
            w2 = rt[:, 3:4]
            x2 = x1_ref[...] + w1 * buf[s, 0] + w2 * buf[s, 1]
            hn = _rms_scale(x2, g_ref[...])
            if not final:
                out_a[...] = x2
                out_b[...] = hn.astype(out_b.dtype)
            else:
                @pl.when(i < n_prompt)
                def _():
                    out_a[...] = hn

                @pl.when(i >= n_prompt)
                def _():
                    out_b[...] = hn


def _combine(pos_flat, x1, route, g_all, gl, ys, n_sample=None):
    T, D = x1.shape
    tk = 128
    R = route.shape[1]
    if n_sample is None:
        n_prompt = None
        g_spec = pl.BlockSpec((None, 1, D), lambda i, pos: (gl, 0, 0))
        out_shape = [jax.ShapeDtypeStruct((T, D), F32), jax.ShapeDtypeStruct((T, D), BF16)]
        out_specs = [pl.BlockSpec((tk, D), lambda i, pos: (i, 0)), pl.BlockSpec((tk, D), lambda i, pos: (i, 0))]
    else:
        assert n_sample == tk
        n_prompt = (T - n_sample) // tk
        g_spec = pl.BlockSpec((1, D), lambda i, pos: (0, 0))
        out_shape = [jax.ShapeDtypeStruct((T - n_sample, D), F32), jax.ShapeDtypeStruct((n_sample, D), F32)]
        out_specs = [pl.BlockSpec((tk, D), lambda i, pos: (jnp.minimum(i, n_prompt - 1), 0)),
                     pl.BlockSpec((tk, D), lambda i, pos: (0, 0))]
    return pl.pallas_call(
        functools.partial(_combine_kernel, T=T, tk=tk, n_prompt=n_prompt),
        out_shape=out_shape,
        grid_spec=pltpu.PrefetchScalarGridSpec(
            num_scalar_prefetch=1,
            grid=(T // tk,),
            in_specs=[pl.BlockSpec((tk, D), lambda i, pos: (i, 0)),
                      pl.BlockSpec((tk, R), lambda i, pos: (i, 0)),
                      g_spec,
                      pl.BlockSpec(memory_space=pl.ANY)],
            out_specs=out_specs,
            scratch_shapes=[pltpu.VMEM((2, 2, tk, D), F32), pltpu.SemaphoreType.DMA((2,))],
        ),
        compiler_params=_cparams(1),
        name="combine",
    )(pos_flat, x1, route, g_all, ys)


def kernel(x_prompt, x_sample, state_conv, state_pool, state_ssm_re, state_ssm_im,
           norm_mix_g, norm_ffn_g, w_in,
           sgu_ln_g, sgu_ln_b, sgu_w, sgu_b,
           conv_w, conv_b, conv_ln_g, conv_ln_b,
           ssm_a_re, ssm_a_im, ssm_log_dt, ssm_b_re, ssm_b_im, ssm_c_re, ssm_c_im, ssm_d, ssm_w_glu, ssm_b_glu,
           pool_w, pool_scale,
           w_branch, w_out,
           router_group_w, router_group_b, router_expert_w, router_expert_b,
           expert_w_gate, expert_w_up, expert_w_down,
           final_norm_g):
    B, L, D = x_prompt.shape
    DB = x_sample.shape[0]
    depth = w_in.shape[0]
    W = sgu_ln_g.shape[-1]
    G, P = ssm_a_re.shape[1:]
    GP = G * P
    T = B * L + DB
    gate_off = 6 * W
    assert x_sample.shape[1] == 1 and L % TIME_TILE == 0 and (B * L) % DB == 0

    row = lambda a: a.reshape(depth, 1, a.shape[-1])
    small = {
        "sgu_ln_g": row(sgu_ln_g), "sgu_ln_b": row(sgu_ln_b),
        "sgu_bT": jnp.swapaxes(sgu_b, 1, 2),
        "sgu_w00": row(jnp.repeat(sgu_w[:, :, 0, 0], W // SGU_GROUPS, axis=-1)),
        "sgu_b0": row(jnp.repeat(sgu_b[:, :, 0], W // SGU_GROUPS, axis=-1)),
        "conv_w": conv_w, "conv_w8": jnp.repeat(conv_w, SUBLANES, axis=1), "conv_b": row(conv_b), "conv_ln_g": row(conv_ln_g), "conv_ln_b": row(conv_ln_b),
        "ssm_d": row(ssm_d), "ssm_b_glu": row(ssm_b_glu), "pool_scale": row(pool_scale),
    }
    S = TIME_TILE // SUBLANES
    lamr, lami, pwr, pwi, bbr, bbi, cr_bd, ci_bd, sguw_bf, wglu_bf, poolw_bf = _prep(
        ssm_a_re, ssm_a_im, ssm_log_dt, ssm_b_re, ssm_b_im, ssm_c_re, ssm_c_im, sgu_w, ssm_w_glu, pool_w, S)
    prep = {"lamr": lamr, "lami": lami, "pwr": pwr, "pwi": pwi, "bbr": bbr, "bbi": bbi,
            "sguw": sguw_bf, "wglu": wglu_bf, "poolw": poolw_bf}

    wr = jnp.concatenate([router_group_w,
                          jnp.transpose(router_expert_w, (0, 2, 1, 3)).reshape(depth, D, N_EXPERTS)], axis=-1)
    wr = jnp.pad(wr, ((0, 0), (0, 0), (0, LANES - wr.shape[-1])))
    rb = jnp.concatenate([router_group_b, router_expert_b.reshape(depth, N_EXPERTS)], axis=-1)
    rb = jnp.pad(rb, ((0, 0), (0, LANES - rb.shape[-1]))).reshape(depth, 1, LANES)

    norm_mix3 = row(norm_mix_g)
    norm_ffn3 = row(norm_ffn_g)
    final3 = final_norm_g.reshape(1, D)

    n_tiles_max = -(-2 * T // EXPERT_TILE) + N_EXPERTS
    n_pad = n_tiles_max * EXPERT_TILE

    x = jnp.concatenate([x_prompt.reshape(B * L, D), x_sample.reshape(DB, D)], axis=0)
    h = _rmsnorm(x, norm_mix3, 0, BF16)

    stc2 = state_conv.reshape(depth, DB, (CONV_K - 1) * W)
    stp2 = state_pool.reshape(depth, DB, POOL_HIST * W)
    hre2 = state_ssm_re.reshape(depth, DB, GP)
    him2 = state_ssm_im.reshape(depth, DB, GP)

    conv_p, conv_s, pool_p, pool_s = [], [], [], []
    sre_p, sim_p, sre_s, sim_s, v_s = [], [], [], [], []
    y_p = y_s = None
    for l in range(depth):
        o, cp, pp, rp, ip = _mixer_prompt(h, w_in, B, L, T, l, small, prep, cr_bd, ci_bd)
        o, g_s, vs, d_in_s, xs_re, xs_im = _mixer_sample(h, w_in, o, stc2, stp2, hre2, him2, DB, T, l,
                                                         small, prep, cr_bd, ci_bd)
        merged = _merge(h, o, w_in, w_branch, l, gate_off)
        x1, hp, logits = _out_proj(merged, x, w_out, norm_ffn3, wr, l)
        route, meta = _route(logits, rb, l, EXPERT_TILE, n_tiles_max)
        pos_flat = jnp.transpose(route[:, 4:6]).astype(I32).reshape(2 * T)
        te = meta[2, :n_tiles_max].astype(I32)
        na = meta[3, :1].astype(I32)
        tend = meta[4, :N_EXPERTS].astype(I32)
        hs = _dispatch(pos_flat, tend, hp, n_pad, EXPERT_TILE)
        ys = _experts(te, na, hs, expert_w_gate, expert_w_up, expert_w_down, l, EXPERT_TILE, n_tiles_max)
        if l + 1 < depth:
            x, h = _combine(pos_flat, x1, route, norm_mix3, l + 1, ys)
        else:
            y_p, y_s = _combine(pos_flat, x1, route, final3, 0, ys, n_sample=DB)

        conv_p.append(cp)
        pool_p.append(pp)
        sre_p.append(rp.reshape(B, G, P))
        sim_p.append(ip.reshape(B, G, P))
        conv_s.append(jnp.concatenate([state_conv[l][:, 1:], g_s[:, None, :]], axis=1))
        pool_s.append(jnp.concatenate([state_pool[l][:, 1:], d_in_s[:, None, :]], axis=1))
        sre_s.append(xs_re.reshape(DB, G, P))
        sim_s.append(xs_im.reshape(DB, G, P))
        v_s.append(vs[:, None, :])

    y_prompt = y_p.reshape(B, L, D)
    y_sample = y_s.reshape(DB, 1, D)
    return (y_prompt, y_sample, jnp.stack(conv_p), jnp.stack(conv_s), jnp.stack(pool_p), jnp.stack(pool_s),
            jnp.stack(sre_p), jnp.stack(sim_p), jnp.stack(sre_s), jnp.stack(sim_s), jnp.stack(v_s))
```

```python
import functools
import math

import jax
import jax.numpy as jnp
from jax import lax
from jax.experimental import pallas as pl
from jax.experimental.pallas import tpu as pltpu

F32 = jnp.float32
BF16 = jnp.bfloat16
I32 = jnp.int32
U32 = jnp.uint32

EPS = 1e-6
CHUNK = 128
SGU_GROUPS = 4
CONV_K = 31
POOL_WINDOWS = (2, 4, 8, 16)
POOL_HIST = 15
SSM_H = 16
SSM_P = 64
N_GROUPS = 4
EXP_PER_GROUP = 4
N_EXPERTS = 16
PAST_LEN = 16384

LANES = 128
SUBLANES = 8
VMEM_LIMIT = 60 * 1024 * 1024

EXPERT_TILE = 256
TIME_TILE = 256
SCAN_COLS = 512


def _cparams(n_axes):
    return pltpu.CompilerParams(dimension_semantics=("arbitrary",) * n_axes,
                                vmem_limit_bytes=VMEM_LIMIT)


def _gelu(x):
    c = math.sqrt(2.0 / math.pi)
    return 0.5 * x * (1.0 + jnp.tanh(c * (x + 0.044715 * (x * x * x))))


def _sigmoid(x):
    return 0.5 * jnp.tanh(0.5 * x) + 0.5


def _layernorm(x, g, b):
    xc = x - jnp.mean(x, axis=-1, keepdims=True)
    var = jnp.mean(xc * xc, axis=-1, keepdims=True)
    return xc * lax.rsqrt(var + EPS) * g + b


def _rms_scale(x, g):
    return x * lax.rsqrt(jnp.mean(x * x, axis=-1, keepdims=True) + EPS) * g


def _dot(a, b):
    return jnp.dot(a, b, preferred_element_type=F32)


def _cast_rows(src_ref, dst_ref, chunk):
    rows = src_ref.shape[0]

    def body(i, c):
        r = pl.multiple_of(i * chunk, chunk)
        dst_ref[pl.ds(r, chunk), :] = src_ref[pl.ds(r, chunk), :].astype(dst_ref.dtype)
        return c

    lax.fori_loop(0, rows // chunk, body, 0)


def _layer_spec(arr, l, single=False):
    nd = arr.ndim
    mode = {"pipeline_mode": pl.Buffered(1)} if single else {}
    return pl.BlockSpec((None,) + tuple(arr.shape[1:]), lambda *_: (l,) + (0,) * (nd - 1), **mode)


def _pick(n, cands):
    for c in cands:
        if n % c == 0:
            return c
    raise ValueError(f"no tile for {n}")


def _rmsnorm_kernel(x_ref, g_ref, o_ref):
    o_ref[...] = _rms_scale(x_ref[...], g_ref[...]).astype(o_ref.dtype)


def _rmsnorm(x, g_all, l, out_dtype):
    T, D = x.shape
    tm = _pick(T, (640, 512, 384, 256, 128))
    return pl.pallas_call(
        _rmsnorm_kernel,
        out_shape=jax.ShapeDtypeStruct((T, D), out_dtype),
        grid=(T // tm,),
        in_specs=[pl.BlockSpec((tm, D), lambda m: (m, 0)), _layer_spec(g_all, l)],
        out_specs=pl.BlockSpec((tm, D), lambda m: (m, 0)),
        compiler_params=_cparams(1),
        name="rmsnorm",
    )(x, g_all)


def _split3(x):
    p1 = x.astype(BF16)
    r1 = x - p1.astype(F32)
    p2 = r1.astype(BF16)
    p3 = (r1 - p2.astype(F32)).astype(BF16)
    return p1, p2, p3


def _prep_kernel(are, aim, ldt, bre, bim, cre, cim, sguw, wglu, poolw,
                 lamr_o, lami_o, pwr_o, pwi_o, bbr_o, bbi_o, cr_o, ci_o, sguw_o, wglu_o, poolw_o, *, S, P, H):
    a_re = are[...]
    a_im = aim[...]
    dt = jnp.exp(ldt[...])
    mag = jnp.exp(a_re * dt)
    lbr = mag * jnp.cos(a_im * dt)
    lbi = mag * jnp.sin(a_im * dt)
    den = a_re * a_re + a_im * a_im
    nr = lbr - 1.0
    kr = (nr * a_re + lbi * a_im) / den
    ki = (lbi * a_re - nr * a_im) / den
    lamr_o[...] = lbr
    lami_o[...] = lbi
    gp = a_re.shape[-1]

    w = bre.shape[0]
    sh_p, sh_h = P.bit_length() - 1, H.bit_length() - 1
    rc = 128
    pe = lax.broadcasted_iota(I32, (P, gp), 0)
    ce = lax.broadcasted_iota(I32, (P, gp), 1)
    rep_p = jnp.where((ce & (P - 1)) == pe, 1.0, 0.0).astype(BF16)
    for i in range(w // rc):
        rows = slice(i * rc, (i + 1) * rc)
        r_i = lax.broadcasted_iota(I32, (rc, gp), 0) + i * rc
        c_i = lax.broadcasted_iota(I32, (rc, gp), 1)
        diag = (r_i >> sh_h) == (c_i >> sh_p)
        br = sum(_dot(piece, rep_p) for piece in _split3(bre[rows, :]))
        bi = sum(_dot(piece, rep_p) for piece in _split3(bim[rows, :]))
        bbr_o[rows, :] = jnp.where(diag, kr * br - ki * bi, 0.0).astype(BF16)
        bbi_o[rows, :] = jnp.where(diag, kr * bi + ki * br, 0.0).astype(BF16)

    he = lax.broadcasted_iota(I32, (H, w), 0)
    ce = lax.broadcasted_iota(I32, (H, w), 1)
    rep_h = jnp.where((ce & (H - 1)) == he, 1.0, 0.0).astype(BF16)
    rc = 256
    for i in range(gp // rc):
        rows = slice(i * rc, (i + 1) * rc)
        r_i = lax.broadcasted_iota(I32, (rc, w), 0) + i * rc
        c_i = lax.broadcasted_iota(I32, (rc, w), 1)
        diag = (r_i >> sh_p) == (c_i >> sh_h)
        cr_o[rows, :] = jnp.where(diag, _dot(cre[rows, :].astype(BF16), rep_h), 0.0).astype(BF16)
        ci_o[rows, :] = jnp.where(diag, _dot(cim[rows, :].astype(BF16), rep_h), 0.0).astype(BF16)

    pr, pi = lbr, lbi
    for s in range(S):
        pwr_o[SUBLANES * s:SUBLANES * (s + 1), :] = jnp.broadcast_to(pr, (SUBLANES, gp))
        pwi_o[SUBLANES * s:SUBLANES * (s + 1), :] = jnp.broadcast_to(pi, (SUBLANES, gp))
        pr, pi = pr * lbr - pi * lbi, pr * lbi + pi * lbr

    t_i = lax.broadcasted_iota(I32, (CHUNK, CHUNK), 0)
    s_i = lax.broadcasted_iota(I32, (CHUNK, CHUNK), 1)
    for g in range(SGU_GROUPS):
        sguw_o[g] = jnp.where(t_i >= s_i, sguw[g], 0.0).astype(BF16)
    wglu_o[...] = wglu[...].astype(BF16)
    for g in range(len(POOL_WINDOWS)):
        poolw_o[g] = poolw[g].astype(BF16)


def _prep(a_re, a_im, log_dt, b_re, b_im, c_re, c_im, sgu_w, w_glu, pool_w, S):
    L, G, P = a_re.shape
    H = b_re.shape[-1]
    GP, W = G * P, G * H
    assert P & (P - 1) == 0 and H & (H - 1) == 0
    bre_t = jnp.swapaxes(b_re, 2, 3).reshape(L, W, P)
    bim_t = jnp.swapaxes(b_im, 2, 3).reshape(L, W, P)
    cre_t = jnp.swapaxes(c_re, 2, 3).reshape(L, GP, H)
    cim_t = jnp.swapaxes(c_im, 2, 3).reshape(L, GP, H)
    are2 = a_re.reshape(L, 1, GP)
    aim2 = a_im.reshape(L, 1, GP)
    ldt2 = jnp.repeat(log_dt, P, axis=-1).reshape(L, 1, GP)

    def lspec(shape):
        nd = len(shape)
        return pl.BlockSpec((None,) + tuple(shape[1:]), lambda l: (l,) + (0,) * (nd - 1))

    ins = [are2, aim2, ldt2, bre_t, bim_t, cre_t, cim_t, sgu_w, w_glu, pool_w]
    out_shapes = [
        jax.ShapeDtypeStruct((L, 1, GP), F32), jax.ShapeDtypeStruct((L, 1, GP), F32),
        jax.ShapeDtypeStruct((L, SUBLANES * S, GP), F32), jax.ShapeDtypeStruct((L, SUBLANES * S, GP), F32),
        jax.ShapeDtypeStruct((L, W, GP), BF16), jax.ShapeDtypeStruct((L, W, GP), BF16),
        jax.ShapeDtypeStruct((L, GP, W), BF16), jax.ShapeDtypeStruct((L, GP, W), BF16),
        jax.ShapeDtypeStruct(sgu_w.shape, BF16), jax.ShapeDtypeStruct(w_glu.shape, BF16),
        jax.ShapeDtypeStruct(pool_w.shape, BF16),
    ]
    return pl.pallas_call(
        functools.partial(_prep_kernel, S=S, P=P, H=H),
        out_shape=out_shapes,
        grid=(L,),
        in_specs=[lspec(a.shape) for a in ins],
        out_specs=[lspec(o.shape) for o in out_shapes],
        compiler_params=_cparams(1),
        name="ssm_prep",
    )(*ins)


PROJ_CHUNK = 256


def _stream_weight_cols(w_hbm, l, n_cols, stage, sem, consume):
    def copy(c, slot):
        return pltpu.make_async_copy(w_hbm.at[l, :, pl.ds(c * PROJ_CHUNK, PROJ_CHUNK)], stage.at[slot],
                                     sem.at[slot])

    n = n_cols // PROJ_CHUNK
    copy(0, 0).start()
    for c in range(n):
        slot = c % 2
        if c + 1 < n:
            copy(c + 1, 1 - slot).start()
        copy(c, slot).wait()
        consume(c, slot)


N_MIXER_INPUTS = 23


def _mixer_prompt_kernel(*refs, n_batch, **kw):
    o_ref = refs[N_MIXER_INPUTS]
    b = pl.program_id(0)

    @pl.when(b < n_batch)
    def _():
        _mixer_prompt_body(*refs, **kw)

    @pl.when((b == n_batch) & (pl.program_id(1) == 0))
    def _():
        o_ref[...] = jnp.zeros_like(o_ref)


def _mixer_prompt_body(h_ref, w_hbm, lng, lnb, sguw, sgub, cw, cb, clg, clb,
                       lamr, lami, pwr, pwi, bbr, bbi, cr, ci, sd, wglu, bglu, poolw, pscale,
                       o_ref, conv_o, pool_o, sre_o, sim_o,
                       w6bf, wstage, wsem, p_ref,
                       gbuf, gsh, pbuf, xr, xi, car_re, car_im, fin_re, fin_im, cm_re, cm_im,
                       *, l, tt, S, W, GP):
    j = pl.program_id(1)
    nt = pl.num_programs(1)
    GH = 32
    PH = 16

    @pl.when((pl.program_id(0) == 0) & (j == 0))
    def _():
        def consume(c, slot):
            _cast_rows(wstage.at[slot], w6bf.at[:, c * PROJ_CHUNK:(c + 1) * PROJ_CHUNK], 256)

        _stream_weight_cols(w_hbm, l, 6 * W, wstage, wsem, consume)

    @pl.when(j == 0)
    def _():
        gbuf[0:GH, :] = jnp.zeros((GH, W), F32)
        pbuf[0:PH, :] = jnp.zeros((PH, W), F32)
        car_re[...] = jnp.zeros_like(car_re)
        car_im[...] = jnp.zeros_like(car_im)

    h = h_ref[...]
    for c in range(3):
        p_ref[:, 2 * c * W:2 * (c + 1) * W] = _dot(h, w6bf[:, 2 * c * W:2 * (c + 1) * W])

    for c in range(tt // CHUNK):
        rows = slice(c * CHUNK, (c + 1) * CHUNK)
        u = _gelu(p_ref[rows, 0:W])
        v = _layernorm(_gelu(p_ref[rows, W:2 * W]), lng[...], lnb[...])
        vb = v.astype(BF16)
        gw = W // SGU_GROUPS
        for g in range(SGU_GROUPS):
            cols = slice(g * gw, (g + 1) * gw)
            mixed = _dot(sguw[g], vb[:, cols]) + sgub[:, g:g + 1]
            o_ref[rows, cols] = (u[:, cols] * mixed).astype(BF16)

    rc = 64
    for c in range(tt // rc):
        rows = slice(c * rc, (c + 1) * rc)
        gbuf[GH + c * rc:GH + (c + 1) * rc, :] = p_ref[rows, 2 * W:3 * W] * _sigmoid(p_ref[rows, 3 * W:4 * W])
    sh_rows = gsh.shape[1]
    for b in range(1, SUBLANES):
        gsh[b] = gbuf[b:b + sh_rows, :]
    rc = 32
    base = GH - (CONV_K - 1)
    for c in range(tt // rc):
        acc = jnp.zeros((rc, W), F32)
        for k in range(CONV_K):
            r0 = base + c * rc + k
            b, a0 = r0 % SUBLANES, r0 - r0 % SUBLANES
            rows = gbuf[a0:a0 + rc, :] if b == 0 else gsh[b, a0:a0 + rc, :]
            wk = jnp.concatenate([cw[SUBLANES * k:SUBLANES * (k + 1), :]] * (rc // SUBLANES), axis=0)
            acc = acc + wk * rows
        y = _layernorm(acc + cb[...], clg[...], clb[...])
        o_ref[c * rc:(c + 1) * rc, W:2 * W] = (y * _sigmoid(y)).astype(BF16)
    gbuf[0:GH, :] = gbuf[tt:tt + GH, :]

    uc = p_ref[:, 4 * W:5 * W]
    r_i = lax.broadcasted_iota(I32, (tt, tt), 0)
    c_i = lax.broadcasted_iota(I32, (tt, tt), 1)
    perm = jnp.where(((r_i & (SUBLANES - 1)) * S + (r_i >> 3)) == c_i, 1.0, 0.0).astype(BF16)
    up = _dot(perm, uc.astype(BF16)).astype(BF16)
    hw, hg = W // 2, GP // 2
    for q in range(2):
        xr[:, q * hg:(q + 1) * hg] = _dot(up[:, q * hw:(q + 1) * hw], bbr[q * hw:(q + 1) * hw, q * hg:(q + 1) * hg])
        xi[:, q * hg:(q + 1) * hg] = _dot(up[:, q * hw:(q + 1) * hw], bbi[q * hw:(q + 1) * hw, q * hg:(q + 1) * hg])

    for cbi in range(GP // SCAN_COLS):
        cols = slice(cbi * SCAN_COLS, (cbi + 1) * SCAN_COLS)
        lr = jnp.broadcast_to(lamr[:, cols], (SUBLANES, SCAN_COLS))
        li = jnp.broadcast_to(lami[:, cols], (SUBLANES, SCAN_COLS))

        def step(s, carry, cols=cols, lr=lr, li=li):
            sr, si = carry
            r0 = pl.multiple_of(s * SUBLANES, SUBLANES)
            nr = lr * sr - li * si + xr[pl.ds(r0, SUBLANES), cols]
            ni = lr * si + li * sr + xi[pl.ds(r0, SUBLANES), cols]
            xr[pl.ds(r0, SUBLANES), cols] = nr
            xi[pl.ds(r0, SUBLANES), cols] = ni
            return nr, ni

        z = jnp.zeros((SUBLANES, SCAN_COLS), F32)
        fr, fi = lax.fori_loop(0, S, step, (z, z), unroll=True)
        fin_re[:, cols] = fr
        fin_im[:, cols] = fi

    lsr = pwr[SUBLANES * (S - 1):SUBLANES * (S - 1) + 1, :]
    lsi = pwi[SUBLANES * (S - 1):SUBLANES * (S - 1) + 1, :]
    c_r = car_re[0:1, :]
    c_im = car_im[0:1, :]
    cm_re[0:1, :] = c_r
    cm_im[0:1, :] = c_im
    for q in range(1, SUBLANES):
        f_r = fin_re[q - 1:q, :]
        f_i = fin_im[q - 1:q, :]
        c_r, c_im = f_r + lsr * c_r - lsi * c_im, f_i + lsr * c_im + lsi * c_r
        cm_re[q:q + 1, :] = c_r
        cm_im[q:q + 1, :] = c_im
    n_r = fin_re[SUBLANES - 1:SUBLANES, :] + lsr * c_r - lsi * c_im
    n_i = fin_im[SUBLANES - 1:SUBLANES, :] + lsr * c_im + lsi * c_r
    car_re[0:1, :] = n_r
    car_im[0:1, :] = n_i

    for cbi in range(GP // SCAN_COLS):
        cols = slice(cbi * SCAN_COLS, (cbi + 1) * SCAN_COLS)
        mr = cm_re[:, cols]
        mi = cm_im[:, cols]

        def fix(s, c, cols=cols, mr=mr, mi=mi):
            r0 = pl.multiple_of(s * SUBLANES, SUBLANES)
            pr = pwr[pl.ds(r0, SUBLANES), cols]
            pi = pwi[pl.ds(r0, SUBLANES), cols]
            xr[pl.ds(r0, SUBLANES), cols] = xr[pl.ds(r0, SUBLANES), cols] + (pr * mr - pi * mi)
            xi[pl.ds(r0, SUBLANES), cols] = xi[pl.ds(r0, SUBLANES), cols] + (pr * mi + pi * mr)
            return c

        lax.fori_loop(0, S, fix, 0, unroll=True)

    yp = jnp.concatenate(
        [_dot(xr[:, q * hg:(q + 1) * hg].astype(BF16), cr[q * hg:(q + 1) * hg, q * hw:(q + 1) * hw])
         - _dot(xi[:, q * hg:(q + 1) * hg].astype(BF16), ci[q * hg:(q + 1) * hg, q * hw:(q + 1) * hw])
         for q in range(2)], axis=1)
    unperm = jnp.where(((c_i & (SUBLANES - 1)) * S + (c_i >> 3)) == r_i, 1.0, 0.0).astype(BF16)
    y1 = yp.astype(BF16)
    r1 = yp - y1.astype(F32)
    y2 = r1.astype(BF16)
    y3 = (r1 - y2.astype(F32)).astype(BF16)
    y = _dot(unperm, y1) + _dot(unperm, y2) + _dot(unperm, y3)
    z = _gelu(y + sd[...] * uc)
    gl = _dot(z.astype(BF16), wglu[...]) + bglu[...]
    o_ref[:, 2 * W:3 * W] = (z * _sigmoid(gl)).astype(BF16)

    xd = p_ref[:, 5 * W:6 * W]
    pbuf[PH:PH + tt, :] = xd
    gw = W // len(POOL_WINDOWS)
    pos = (j * tt + lax.broadcasted_iota(I32, (tt, 1), 0) + 1).astype(F32)
    for gi, win in enumerate(POOL_WINDOWS):
        cols = slice(gi * gw, (gi + 1) * gw)
        s = xd[:, cols]
        for i in range(1, win):
            s = s + pbuf[PH - i:PH - i + tt, cols]
        cnt = jnp.minimum(pos, float(win))
        pooled = s / cnt - xd[:, cols]
        mixed = _dot(pooled.astype(BF16), poolw[gi]) * pscale[:, cols]
        o_ref[:, 3 * W + gi * gw:3 * W + (gi + 1) * gw] = mixed.astype(BF16)
    pbuf[0:PH, :] = pbuf[tt:tt + PH, :]

    @pl.when(j == nt - 1)
    def _():
        conv_o[...] = gbuf[GH - (CONV_K - 1):GH, :]
        pool_o[...] = pbuf[PH - POOL_HIST:PH, :]
        sre_o[...] = car_re[0:1, :]
        sim_o[...] = car_im[0:1, :]


def _mixer_prompt(h, w_in, B, L, T, l, small, prep, cr, ci):
    W = small["sgu_ln_g"].shape[-1]
    GP = prep["lamr"].shape[-1]
    D = h.shape[1]
    D4 = 4 * W
    tt = TIME_TILE
    S = tt // SUBLANES
    nt = L // tt
    ins = [h, w_in,
           small["sgu_ln_g"], small["sgu_ln_b"], prep["sguw"], small["sgu_bT"],
           small["conv_w8"], small["conv_b"], small["conv_ln_g"], small["conv_ln_b"],
           prep["lamr"], prep["lami"], prep["pwr"], prep["pwi"], prep["bbr"], prep["bbi"], cr, ci,
           small["ssm_d"], prep["wglu"], small["ssm_b_glu"], prep["poolw"], small["pool_scale"]]
    assert len(ins) == N_MIXER_INPUTS
    in_specs = [pl.BlockSpec((tt, D), lambda b, j: (jnp.minimum(b * nt + j, B * nt), 0)),
                pl.BlockSpec(memory_space=pl.ANY)]
    in_specs += [_layer_spec(a, l, single=True) for a in ins[2:]]
    out_shapes = [
        jax.ShapeDtypeStruct((T, D4), BF16),
        jax.ShapeDtypeStruct((B, CONV_K - 1, W), F32),
        jax.ShapeDtypeStruct((B, POOL_HIST, W), F32),
        jax.ShapeDtypeStruct((B, 1, GP), F32),
        jax.ShapeDtypeStruct((B, 1, GP), F32),
    ]
    out_specs = [
        pl.BlockSpec((tt, D4), lambda b, j: (jnp.minimum(b * nt + j, B * nt), 0)),
        pl.BlockSpec((None, CONV_K - 1, W), lambda b, j: (jnp.minimum(b, B - 1), 0, 0)),
        pl.BlockSpec((None, POOL_HIST, W), lambda b, j: (jnp.minimum(b, B - 1), 0, 0)),
        pl.BlockSpec((None, 1, GP), lambda b, j: (jnp.minimum(b, B - 1), 0, 0)),
        pl.BlockSpec((None, 1, GP), lambda b, j: (jnp.minimum(b, B - 1), 0, 0)),
    ]
    scratch = [
        pltpu.VMEM((D, 6 * W), BF16), pltpu.VMEM((2, D, PROJ_CHUNK), F32), pltpu.SemaphoreType.DMA((2,)),
        pltpu.VMEM((tt, 6 * W), F32),
        pltpu.VMEM((32 + tt, W), F32), pltpu.VMEM((SUBLANES, 32 + tt - SUBLANES, W), F32),
        pltpu.VMEM((16 + tt, W), F32),
        pltpu.VMEM((tt, GP), F32), pltpu.VMEM((tt, GP), F32),
        pltpu.VMEM((SUBLANES, GP), F32), pltpu.VMEM((SUBLANES, GP), F32),
        pltpu.VMEM((SUBLANES, GP), F32), pltpu.VMEM((SUBLANES, GP), F32),
        pltpu.VMEM((SUBLANES, GP), F32), pltpu.VMEM((SUBLANES, GP), F32),
    ]
    return pl.pallas_call(
        functools.partial(_mixer_prompt_kernel, n_batch=B, l=l, tt=tt, S=S, W=W, GP=GP),
        out_shape=out_shapes,
        grid=(B + 1, nt),
        in_specs=in_specs,
        out_specs=out_specs,
        scratch_shapes=scratch,
        compiler_params=_cparams(2),
        name="mixer_prompt",
    )(*ins)


def _mixer_sample_kernel(h_ref, w_hbm, stc, stp, hre, him, lng, lnb, w00, b0, cw, cb, clg, clb,
                         lamr, lami, bbr, bbi, cr, ci, sd, wglu, bglu, poolw, pscale, o_in,
                         o_ref, g_o, v_o, d_o, xre_o, xim_o,
                         p_ref, wstage, wchunk, wsem, *, l, W):
    del o_in
    h = h_ref[...]

    def consume(c, slot):
        _cast_rows(wstage.at[slot], wchunk, 256)
        p_ref[:, c * PROJ_CHUNK:(c + 1) * PROJ_CHUNK] = _dot(h, wchunk[...])

    _stream_weight_cols(w_hbm, l, 6 * W, wstage, wsem, consume)

    u = _gelu(p_ref[:, 0:W])
    v = _layernorm(_gelu(p_ref[:, W:2 * W]), lng[...], lnb[...])
    v_o[...] = v
    o_ref[:, 0:W] = (u * (w00[...] * v + b0[...])).astype(BF16)

    g = p_ref[:, 2 * W:3 * W] * _sigmoid(p_ref[:, 3 * W:4 * W])
    g_o[...] = g
    acc = cw[CONV_K - 1:CONV_K, :] * g
    for k in range(CONV_K - 1):
        acc = acc + cw[k:k + 1, :] * stc[:, k * W:(k + 1) * W]
    y = _layernorm(acc + cb[...], clg[...], clb[...])
    o_ref[:, W:2 * W] = (y * _sigmoid(y)).astype(BF16)

    uc = p_ref[:, 4 * W:5 * W]
    ub = uc.astype(BF16)
    h_r = hre[...]
    h_i = him[...]
    l_r = lamr[...]
    l_i = lami[...]
    x_r = l_r * h_r - l_i * h_i + _dot(ub, bbr[...])
    x_i = l_r * h_i + l_i * h_r + _dot(ub, bbi[...])
    xre_o[...] = x_r
    xim_o[...] = x_i
    yv = _dot(x_r.astype(BF16), cr[...]) - _dot(x_i.astype(BF16), ci[...]) + sd[...] * uc
    z = _gelu(yv)
    gl = _dot(z.astype(BF16), wglu[...]) + bglu[...]
    o_ref[:, 2 * W:3 * W] = (z * _sigmoid(gl)).astype(BF16)

    xd = p_ref[:, 5 * W:6 * W]
    d_o[...] = xd
    gw = W // len(POOL_WINDOWS)
    for gi, win in enumerate(POOL_WINDOWS):
        cols = slice(gi * gw, (gi + 1) * gw)
        s = xd[:, cols]
        for i in range(1, win):
            r = POOL_HIST - i
            s = s + stp[:, r * W + gi * gw:r * W + (gi + 1) * gw]
        cnt = float(min(PAST_LEN + 1, win))
        pooled = s / cnt - xd[:, cols]
        mixed = _dot(pooled.astype(BF16), poolw[gi]) * pscale[:, cols]
        o_ref[:, 3 * W + gi * gw:3 * W + (gi + 1) * gw] = mixed.astype(BF16)


def _mixer_sample(h, w_in, o_prev, stc2, stp2, hre2, him2, DB, T, l, small, prep, cr, ci):
    W = small["sgu_ln_g"].shape[-1]
    GP = prep["lamr"].shape[-1]
    D = h.shape[1]
    D4 = 4 * W
    blk = (T - DB) // DB
    ins = [h, w_in, stc2, stp2, hre2, him2,
           small["sgu_ln_g"], small["sgu_ln_b"], small["sgu_w00"], small["sgu_b0"],
           small["conv_w"], small["conv_b"], small["conv_ln_g"], small["conv_ln_b"],
           prep["lamr"], prep["lami"], prep["bbr"], prep["bbi"], cr, ci,
           small["ssm_d"], prep["wglu"], small["ssm_b_glu"], prep["poolw"], small["pool_scale"], o_prev]
    in_specs = [pl.BlockSpec((DB, D), lambda i: (blk, 0)), pl.BlockSpec(memory_space=pl.ANY)]
    in_specs += [_layer_spec(a, l, single=True) for a in ins[2:-1]]
    in_specs += [pl.BlockSpec(memory_space=pl.ANY)]
    out_shapes = [
        jax.ShapeDtypeStruct((T, D4), BF16),
        jax.ShapeDtypeStruct((DB, W), F32), jax.ShapeDtypeStruct((DB, W), F32), jax.ShapeDtypeStruct((DB, W), F32),
        jax.ShapeDtypeStruct((DB, GP), F32), jax.ShapeDtypeStruct((DB, GP), F32),
    ]
    out_specs = [
        pl.BlockSpec((DB, D4), lambda i: (blk, 0)),
        pl.BlockSpec((DB, W), lambda i: (0, 0)), pl.BlockSpec((DB, W), lambda i: (0, 0)),
        pl.BlockSpec((DB, W), lambda i: (0, 0)),
        pl.BlockSpec((DB, GP), lambda i: (0, 0)), pl.BlockSpec((DB, GP), lambda i: (0, 0)),
    ]
    return pl.pallas_call(
        functools.partial(_mixer_sample_kernel, l=l, W=W),
        out_shape=out_shapes,
        grid=(1,),
        in_specs=in_specs,
        out_specs=out_specs,
        scratch_shapes=[pltpu.VMEM((DB, 6 * W), F32), pltpu.VMEM((2, D, PROJ_CHUNK), F32),
                        pltpu.VMEM((D, PROJ_CHUNK), BF16), pltpu.SemaphoreType.DMA((2,))],
        input_output_aliases={len(ins) - 1: 0},
        compiler_params=_cparams(1),
        name="mixer_sample",
    )(*ins)


def _merge_kernel(h_ref, o_ref, wg0, wg1, wg2, wg3, wb_ref, m_ref, wgbf, wbbf, *, W):
    wgs = (wg0, wg1, wg2, wg3)

    @pl.when(pl.program_id(1) == 0)
    def _():
        for b in range(4):
            _cast_rows(wgs[b], wgbf.at[b], 256)
            wbbf[b] = wb_ref[b].astype(BF16)

    h = h_ref[...]
    acc = None
    for b in range(4):
        gate = _sigmoid(_dot(h, wgbf[b]))
        term = gate * _dot(o_ref[:, b * W:(b + 1) * W], wbbf[b])
        acc = term if acc is None else acc + term
    m_ref[...] = acc.astype(BF16)


def _merge(h, o, w_in, w_branch, l, gate_off):
    T, D = h.shape
    W = w_branch.shape[2]
    tm = _pick(T, (640, 512, 384, 256, 128))
    tn = 256
    nb = D // tn

    def gate_spec(b):
        blk0 = (gate_off + b * D) // tn
        return pl.BlockSpec((None, D, tn), lambda n, m: (l, 0, blk0 + n))

    return pl.pallas_call(
        functools.partial(_merge_kernel, W=W),
        out_shape=jax.ShapeDtypeStruct((T, D), BF16),
        grid=(nb, T // tm),
        in_specs=[pl.BlockSpec((tm, D), lambda n, m: (m, 0)),
                  pl.BlockSpec((tm, 4 * W), lambda n, m: (m, 0)),
                  gate_spec(0), gate_spec(1), gate_spec(2), gate_spec(3),
                  pl.BlockSpec((None, 4, W, tn), lambda n, m: (l, 0, 0, n))],
        out_specs=pl.BlockSpec((tm, tn), lambda n, m: (m, n)),
        scratch_shapes=[pltpu.VMEM((4, D, tn), BF16), pltpu.VMEM((4, W, tn), BF16)],
        compiler_params=_cparams(2),
        name="gated_merge",
    )(h, o, w_in, w_in, w_in, w_in, w_branch)


def _out_kernel(m_ref, x_ref, w_ref, g_ref, wr_ref, x1_ref, hp_ref, lg_ref, wbf, wr2):
    R = wr_ref.shape[-1]

    @pl.when(pl.program_id(0) == 0)
    def _():
        _cast_rows(w_ref, wbf, 256)
        wr = wr_ref[...]
        hi = wr.astype(BF16)
        wr2[:, 0:R] = hi
        wr2[:, R:2 * R] = (wr - hi.astype(F32)).astype(BF16)

    x1 = x_ref[...] + _dot(m_ref[...], wbf[...])
    x1_ref[...] = x1
    h = _rms_scale(x1, g_ref[...])
    hb = h.astype(BF16)
    hl = (h - hb.astype(F32)).astype(BF16)
    both = _dot(hb, wr2[...])
    lg_ref[...] = both[:, 0:R] + both[:, R:2 * R] + _dot(hl, wr2[:, 0:R])
    bits = lax.bitcast_convert_type(hb.astype(F32), U32)
    half = bits.shape[1] // 2
    word = (bits[:, :half] >> 16) | (bits[:, half:] & jnp.uint32(0xFFFF0000))
    tm = word.shape[0]
    for j in range(half // LANES):
        hp_ref[pl.ds(j, tm, stride=half // LANES), :] = word[:, j * LANES:(j + 1) * LANES]


def _out_proj(merged, x, w_out, g_all, wr_all, l):
    T, D = x.shape
    tm = _pick(T, (320, 256, 128))
    R = wr_all.shape[-1]
    return pl.pallas_call(
        _out_kernel,
        out_shape=[jax.ShapeDtypeStruct((T, D), F32), jax.ShapeDtypeStruct((T * (D // 2 // LANES), LANES), U32),
                   jax.ShapeDtypeStruct((T, R), F32)],
        grid=(T // tm,),
        in_specs=[pl.BlockSpec((tm, D), lambda m: (m, 0)),
                  pl.BlockSpec((tm, D), lambda m: (m, 0)),
                  pl.BlockSpec((None, D, D), lambda m: (l, 0, 0), pipeline_mode=pl.Buffered(1)),
                  _layer_spec(g_all, l), _layer_spec(wr_all, l)],
        out_specs=[pl.BlockSpec((tm, D), lambda m: (m, 0)),
                   pl.BlockSpec((tm * (D // 2 // LANES), LANES), lambda m: (m, 0)),
                   pl.BlockSpec((tm, R), lambda m: (m, 0))],
        scratch_shapes=[pltpu.VMEM((D, D), BF16), pltpu.VMEM((D, 2 * R), BF16)],
        compiler_params=_cparams(1),
        name="out_proj",
    )(merged, x, w_out, g_all, wr_all)


def _route_kernel(lg_ref, bias_ref, rt_ref, meta_ref, cnt, off, *, tm, tile, n_tiles_max):
    p = pl.program_id(0)
    m = pl.program_id(1)
    lane = lax.broadcasted_iota(I32, (tm, LANES), 1).astype(F32)
    neg = jnp.float32(-jnp.inf)
    big = jnp.float32(1e9)

    @pl.when((p == 0) & (m == 0))
    def _():
        cnt[...] = jnp.zeros_like(cnt)

    lg = lg_ref[...] + bias_ref[...]
    is_g = lane < N_GROUPS
    gl = jnp.where(is_g, lg, neg)
    gmax = jnp.max(gl, axis=-1, keepdims=True)
    gidx = jnp.min(jnp.where(gl == gmax, lane, big), axis=-1, keepdims=True)
    gsum = jnp.sum(jnp.where(is_g, jnp.exp(gl - gmax), 0.0), axis=-1, keepdims=True)
    g_w = 1.0 / gsum
    lo = N_GROUPS + EXP_PER_GROUP * gidx
    in_grp = (lane >= lo) & (lane < lo + EXP_PER_GROUP)
    el = jnp.where(in_grp, lg, neg)
    v1 = jnp.max(el, axis=-1, keepdims=True)
    i1 = jnp.min(jnp.where(el == v1, lane, big), axis=-1, keepdims=True)
    el2 = jnp.where(lane == i1, neg, el)
    v2 = jnp.max(el2, axis=-1, keepdims=True)
    i2 = jnp.min(jnp.where(el2 == v2, lane, big), axis=-1, keepdims=True)
    e2x = jnp.exp(v2 - v1)
    w1 = g_w / (1.0 + e2x)
    w2 = g_w * e2x / (1.0 + e2x)
    e1 = i1 - N_GROUPS
    e2 = i2 - N_GROUPS
    a1 = jnp.where(lane == e1, 1.0, 0.0)
    a2 = jnp.where(lane == e2, 1.0, 0.0)
    a = a1 + a2

    @pl.when(p == 0)
    def _():
        cnt[0:1, :] = cnt[0:1, :] + jnp.sum(a, axis=0, keepdims=True)

    @pl.when((p == 1) & (m == 0))
    def _():
        counts = cnt[0:1, :]
        tiles = jnp.floor((counts + (tile - 1)) * (1.0 / tile))
        tiles8 = jnp.broadcast_to(tiles, (SUBLANES, LANES)).astype(BF16)
        ri = lax.broadcasted_iota(I32, (LANES, LANES), 0)
        ci = lax.broadcasted_iota(I32, (LANES, LANES), 1)
        upper = jnp.where(ri < ci, 1.0, 0.0).astype(BF16)
        toff = _dot(tiles8, upper)[0:1, :]
        off[0:1, :] = toff * tile
        tend = toff + tiles
        lane1 = lax.broadcasted_iota(I32, (1, LANES), 1).astype(F32)
        n_act = jnp.sum(jnp.where(lane1 == N_EXPERTS - 1, tend, 0.0), axis=-1, keepdims=True)
        texp = jnp.zeros((1, LANES), F32)
        for e in range(N_EXPERTS - 1):
            end_e = jnp.sum(jnp.where(lane1 == e, tend, 0.0), axis=-1, keepdims=True)
            texp = texp + jnp.where(jnp.minimum(lane1, n_act - 1.0) >= end_e, 1.0, 0.0)
        meta_ref[0:1, :] = counts
        meta_ref[1:2, :] = off[0:1, :]
        meta_ref[2:3, :] = texp
        meta_ref[3:4, :] = jnp.broadcast_to(n_act, (1, LANES))
        meta_ref[4:5, :] = tend
        meta_ref[5:8, :] = jnp.zeros((3, LANES), F32)
        cnt[...] = jnp.zeros_like(cnt)

    @pl.when(p == 1)
    def _():
        ri = lax.broadcasted_iota(I32, (tm, tm), 0)
        ci = lax.broadcasted_iota(I32, (tm, tm), 1)
        ltri = jnp.where(ci < ri, 1.0, 0.0).astype(BF16)
        cum = _dot(ltri, a.astype(BF16)) + cnt[0:1, :] + off[0:1, :]
        pos1 = jnp.sum(a1 * cum, axis=-1, keepdims=True)
        pos2 = jnp.sum(a2 * cum, axis=-1, keepdims=True)
        cnt[0:1, :] = cnt[0:1, :] + jnp.sum(a, axis=0, keepdims=True)
        rt = jnp.where(lane == 0, e1, 0.0)
        rt = jnp.where(lane == 1, e2, rt)
        rt = jnp.where(lane == 2, w1, rt)
        rt = jnp.where(lane == 3, w2, rt)
        rt = jnp.where(lane == 4, pos1, rt)
        rt = jnp.where(lane == 5, pos2, rt)
        rt_ref[...] = rt


def _route(logits, bias_all, l, tile, n_tiles_max):
    T, R = logits.shape
    tm = _pick(T, (640, 512, 384, 256, 128))
    return pl.pallas_call(
        functools.partial(_route_kernel, tm=tm, tile=tile, n_tiles_max=n_tiles_max),
        out_shape=[jax.ShapeDtypeStruct((T, R), F32), jax.ShapeDtypeStruct((SUBLANES, LANES), F32)],
        grid=(2, T // tm),
        in_specs=[pl.BlockSpec((tm, R), lambda p, m: (m, 0)), _layer_spec(bias_all, l)],
        out_specs=[pl.BlockSpec((tm, R), lambda p, m: (m * p, 0)),
                   pl.BlockSpec((SUBLANES, LANES), lambda p, m: (0, 0))],
        scratch_shapes=[pltpu.VMEM((SUBLANES, LANES), F32), pltpu.VMEM((SUBLANES, LANES), F32)],
        compiler_params=_cparams(2),
        name="route",
    )(logits, bias_all)


def _dispatch_kernel(pos_ref, tend_ref, hp_ref, hs_ref, stage, zbuf, sem, zsem, *, T, tk, tile, n_tiles, rpt):
    i = pl.program_id(0)
    n = pl.num_programs(0)
    trows = tile * rpt

    @pl.when(i == 0)
    def _():
        zbuf[...] = jnp.zeros_like(zbuf)

        def last_tile_copy(ex):
            end = tend_ref[ex]
            start = tend_ref[ex - 1] if ex > 0 else 0
            row0 = pl.multiple_of((end - 1) * trows, trows)
            return end > start, pltpu.make_async_copy(zbuf, hs_ref.at[pl.ds(row0, trows)], zsem)

        def spare_tile_copy(k):
            idx = tend_ref[N_EXPERTS - 1] + k
            row0 = pl.multiple_of(jnp.minimum(idx, n_tiles - 1) * trows, trows)
            return idx < n_tiles, pltpu.make_async_copy(zbuf, hs_ref.at[pl.ds(row0, trows)], zsem)

        fills = [last_tile_copy(ex) for ex in range(N_EXPERTS)] + [spare_tile_copy(k) for k in range(N_EXPERTS)]
        for go, cp in fills:
            @pl.when(go)
            def _(cp=cp):
                cp.start()

        for go, cp in fills:
            @pl.when(go)
            def _(cp=cp):
                cp.wait()

    def copies(step, slot):
        out = []
        for u in range(tk):
            t = step * tk + u
            for k in range(2):
                p = pl.multiple_of(pos_ref[k * T + t] * rpt, rpt)
                out.append(pltpu.make_async_copy(stage.at[slot, pl.ds(u * rpt, rpt)], hs_ref.at[pl.ds(p, rpt)],
                                                 sem.at[slot]))
        return out

    def wait_slot(s):
        rows = hs_ref.at[pl.ds(0, 2 * tk * rpt)]
        pltpu.make_async_copy(rows, rows, sem.at[s]).wait()

    slot = lax.rem(i, 2)
    for s in range(2):
        @pl.when(slot == s)
        def _(s=s):
            stage[s] = hp_ref[...]
            for cp in copies(i, s):
                cp.start()

        @pl.when((slot == 1 - s) & (i > 0))
        def _(s=s):
            wait_slot(s)

        @pl.when((slot == s) & (i == n - 1))
        def _(s=s):
            wait_slot(s)


def _dispatch(pos_flat, tend, hp, n_pad, tile, rpt):
    T = hp.shape[0] // rpt
    tk = 128
    return pl.pallas_call(
        functools.partial(_dispatch_kernel, T=T, tk=tk, tile=tile, n_tiles=n_pad // tile, rpt=rpt),
        out_shape=jax.ShapeDtypeStruct((n_pad * rpt, LANES), U32),
        grid_spec=pltpu.PrefetchScalarGridSpec(
            num_scalar_prefetch=2,
            grid=(T // tk,),
            in_specs=[pl.BlockSpec((tk * rpt, LANES), lambda i, pos, tend: (i, 0))],
            out_specs=pl.BlockSpec(memory_space=pl.ANY),
            scratch_shapes=[pltpu.VMEM((2, tk * rpt, LANES), U32), pltpu.VMEM((tile * rpt, LANES), U32),
                            pltpu.SemaphoreType.DMA((2,)), pltpu.SemaphoreType.DMA],
        ),
        compiler_params=_cparams(1),
        name="dispatch",
    )(pos_flat, tend, hp)


def _expert_kernel(te_ref, na_ref, hs_ref, wg_ref, wu_ref, wd_ref, ys_ref,
                   wgst, wust, wdst, wgbf, wubf, wdbf, sem, *, l, tile):
    i = pl.program_id(0)
    na = na_ref[0]
    e = te_ref[i]
    first = (i < na) & ((i == 0) | (e != te_ref[jnp.maximum(i - 1, 0)]))

    def weight_copies(ex):
        return (pltpu.make_async_copy(wg_ref.at[l, ex], wgst, sem),
                pltpu.make_async_copy(wu_ref.at[l, ex], wust, sem),
                pltpu.make_async_copy(wd_ref.at[l, ex], wdst, sem))

    @pl.when(i == 0)
    def _():
        for cp in weight_copies(e):
            cp.start()

    @pl.when(first)
    def _():
        for cp in weight_copies(e):
            cp.wait()
        _cast_rows(wgst, wgbf, 256)
        _cast_rows(wust, wubf, 256)
        _cast_rows(wdst, wdbf, 256)
        j = lax.while_loop(lambda j: (j < na) & (te_ref[jnp.minimum(j, na - 1)] == e), lambda j: j + 1, i + 1)

        @pl.when(j < na)
        def _():
            for cp in weight_copies(te_ref[jnp.minimum(j, na - 1)]):
                cp.start()

    @pl.when(i < na)
    def _():
        rin = hs_ref.shape[0] // tile
        w = jnp.concatenate([hs_ref[pl.ds(j, tile, stride=rin), :] for j in range(rin)], axis=1)
        half = w.shape[1]
        lo = lax.bitcast_convert_type(w << 16, F32).astype(BF16)
        hi = lax.bitcast_convert_type(w & jnp.uint32(0xFFFF0000), F32).astype(BF16)
        a = _dot(lo, wgbf[0:half, :]) + _dot(hi, wgbf[half:, :])
        b = _dot(lo, wubf[0:half, :]) + _dot(hi, wubf[half:, :])
        hid = (a * _sigmoid(a) * b).astype(BF16)
        ys_ref[...] = _dot(hid, wdbf[...])

    @pl.when(i >= na)
    def _():
        ys_ref[...] = jnp.zeros_like(ys_ref)


def _experts(te, na, hs, wg, wu, wd, l, tile, n_tiles_max):
    D, F = wg.shape[-2:]
    rin = D // 2 // LANES
    n_pad = hs.shape[0] // rin

    def row_map(i, te_ref, na_ref):
        return (jnp.minimum(i, na_ref[0] - 1), 0)

    any_spec = pl.BlockSpec(memory_space=pl.ANY)
    return pl.pallas_call(
        functools.partial(_expert_kernel, l=l, tile=tile),
        out_shape=jax.ShapeDtypeStruct((n_pad, D), F32),
        grid_spec=pltpu.PrefetchScalarGridSpec(
            num_scalar_prefetch=2,
            grid=(n_tiles_max,),
            in_specs=[pl.BlockSpec((tile * rin, LANES), row_map), any_spec, any_spec, any_spec],
            out_specs=pl.BlockSpec((tile, D), lambda i, te_ref, na_ref: (i, 0)),
            scratch_shapes=[pltpu.VMEM((D, F), F32), pltpu.VMEM((D, F), F32), pltpu.VMEM((F, D), F32),
                            pltpu.VMEM((D, F), BF16), pltpu.VMEM((D, F), BF16), pltpu.VMEM((F, D), BF16),
                            pltpu.SemaphoreType.DMA],
        ),
        compiler_params=_cparams(1),
        name="experts",
    )(te, na, hs, wg, wu, wd)


def _combine_kernel(pos_ref, x1_ref, rt_ref, g_ref, ys_ref, *rest, T, tk, n_prompt):
    final = n_prompt is not None
    out_a, out_b, buf, sem = rest
    i = pl.program_id(0)
    n = pl.num_programs(0)

    def copies(step, slot):
        out = []
        for u in range(tk):
            t = step * tk + u
            for k in range(2):
                p = pos_ref[k * T + t]
                out.append(pltpu.make_async_copy(ys_ref.at[pl.ds(p, 1)], buf.at[slot, k, pl.ds(u, 1)],
                                                 sem.at[slot]))
        return out

    slot = lax.rem(i, 2)

    @pl.when(i == 0)
    def _():
        for cp in copies(0, 0):
            cp.start()

    for s in range(2):
        @pl.when((i + 1 < n) & (slot == 1 - s))
        def _(s=s):
            for cp in copies(i + 1, s):
                cp.start()

    for s in range(2):
        @pl.when(slot == s)
        def _(s=s):
            pltpu.make_async_copy(buf.at[s], buf.at[s], sem.at[s]).wait()
            rt = rt_ref[...]
            w1 = rt[:, 2:3]
            w2 = rt[:, 3:4]
            x2 = x1_ref[...] + w1 * buf[s, 0] + w2 * buf[s, 1]
            hn = _rms_scale(x2, g_ref[...])
            if not final:
                out_a[...] = x2
                out_b[...] = hn.astype(out_b.dtype)
            else:
                @pl.when(i < n_prompt)
                def _():
                    out_a[...] = hn

                @pl.when(i >= n_prompt)
                def _():
                    out_b[...] = hn


def _combine(pos_flat, x1, route, g_all, gl, ys, n_sample=None):
    T, D = x1.shape
    tk = 128
    R = route.shape[1]
    if n_sample is None:
        n_prompt = None
        g_spec = pl.BlockSpec((None, 1, D), lambda i, pos: (gl, 0, 0))
        out_shape = [jax.ShapeDtypeStruct((T, D), F32), jax.ShapeDtypeStruct((T, D), BF16)]
        out_specs = [pl.BlockSpec((tk, D), lambda i, pos: (i, 0)), pl.BlockSpec((tk, D), lambda i, pos: (i, 0))]
    else:
        assert n_sample == tk
        n_prompt = (T - n_sample) // tk
        g_spec = pl.BlockSpec((1, D), lambda i, pos: (0, 0))
        out_shape = [jax.ShapeDtypeStruct((T - n_sample, D), F32), jax.ShapeDtypeStruct((n_sample, D), F32)]
        out_specs = [pl.BlockSpec((tk, D), lambda i, pos: (jnp.minimum(i, n_prompt - 1), 0)),
                     pl.BlockSpec((tk, D), lambda i, pos: (0, 0))]
    return pl.pallas_call(
        functools.partial(_combine_kernel, T=T, tk=tk, n_prompt=n_prompt),
        out_shape=out_shape,
        grid_spec=pltpu.PrefetchScalarGridSpec(
            num_scalar_prefetch=1,
            grid=(T // tk,),
            in_specs=[pl.BlockSpec((tk, D), lambda i, pos: (i, 0)),
                      pl.BlockSpec((tk, R), lambda i, pos: (i, 0)),
                      g_spec,
                      pl.BlockSpec(memory_space=pl.ANY)],
            out_specs=out_specs,
            scratch_shapes=[pltpu.VMEM((2, 2, tk, D), F32), pltpu.SemaphoreType.DMA((2,))],
        ),
        compiler_params=_cparams(1),
        name="combine",
    )(pos_flat, x1, route, g_all, ys)


def kernel(x_prompt, x_sample, state_conv, state_pool, state_ssm_re, state_ssm_im,
           norm_mix_g, norm_ffn_g, w_in,
           sgu_ln_g, sgu_ln_b, sgu_w, sgu_b,
           conv_w, conv_b, conv_ln_g, conv_ln_b,
           ssm_a_re, ssm_a_im, ssm_log_dt, ssm_b_re, ssm_b_im, ssm_c_re, ssm_c_im, ssm_d, ssm_w_glu, ssm_b_glu,
           pool_w, pool_scale,
           w_branch, w_out,
           router_group_w, router_group_b, router_expert_w, router_expert_b,
           expert_w_gate, expert_w_up, expert_w_down,
           final_norm_g):
    B, L, D = x_prompt.shape
    DB = x_sample.shape[0]
    depth = w_in.shape[0]
    W = sgu_ln_g.shape[-1]
    G, P = ssm_a_re.shape[1:]
    GP = G * P
    T = B * L + DB
    gate_off = 6 * W
    assert x_sample.shape[1] == 1 and L % TIME_TILE == 0 and (B * L) % DB == 0

    row = lambda a: a.reshape(depth, 1, a.shape[-1])
    small = {
        "sgu_ln_g": row(sgu_ln_g), "sgu_ln_b": row(sgu_ln_b),
        "sgu_bT": jnp.swapaxes(sgu_b, 1, 2),
        "sgu_w00": row(jnp.repeat(sgu_w[:, :, 0, 0], W // SGU_GROUPS, axis=-1)),
        "sgu_b0": row(jnp.repeat(sgu_b[:, :, 0], W // SGU_GROUPS, axis=-1)),
        "conv_w": conv_w, "conv_w8": jnp.repeat(conv_w, SUBLANES, axis=1), "conv_b": row(conv_b), "conv_ln_g": row(conv_ln_g), "conv_ln_b": row(conv_ln_b),
        "ssm_d": row(ssm_d), "ssm_b_glu": row(ssm_b_glu), "pool_scale": row(pool_scale),
    }
    S = TIME_TILE // SUBLANES
    lamr, lami, pwr, pwi, bbr, bbi, cr_bd, ci_bd, sguw_bf, wglu_bf, poolw_bf = _prep(
        ssm_a_re, ssm_a_im, ssm_log_dt, ssm_b_re, ssm_b_im, ssm_c_re, ssm_c_im, sgu_w, ssm_w_glu, pool_w, S)
    prep = {"lamr": lamr, "lami": lami, "pwr": pwr, "pwi": pwi, "bbr": bbr, "bbi": bbi,
            "sguw": sguw_bf, "wglu": wglu_bf, "poolw": poolw_bf}

    wr = jnp.concatenate([router_group_w,
                          jnp.transpose(router_expert_w, (0, 2, 1, 3)).reshape(depth, D, N_EXPERTS)], axis=-1)
    wr = jnp.pad(wr, ((0, 0), (0, 0), (0, LANES - wr.shape[-1])))
    rb = jnp.concatenate([router_group_b, router_expert_b.reshape(depth, N_EXPERTS)], axis=-1)
    rb = jnp.pad(rb, ((0, 0), (0, LANES - rb.shape[-1]))).reshape(depth, 1, LANES)

    norm_mix3 = row(norm_mix_g)
    norm_ffn3 = row(norm_ffn_g)
    final3 = final_norm_g.reshape(1, D)

    n_tiles_max = -(-2 * T // EXPERT_TILE) + N_EXPERTS
    n_pad = n_tiles_max * EXPERT_TILE

    x = jnp.concatenate([x_prompt.reshape(B * L, D), x_sample.reshape(DB, D)], axis=0)
    h = _rmsnorm(x, norm_mix3, 0, BF16)

    stc2 = state_conv.reshape(depth, DB, (CONV_K - 1) * W)
    stp2 = state_pool.reshape(depth, DB, POOL_HIST * W)
    hre2 = state_ssm_re.reshape(depth, DB, GP)
    him2 = state_ssm_im.reshape(depth, DB, GP)

    conv_p, conv_s, pool_p, pool_s = [], [], [], []
    sre_p, sim_p, sre_s, sim_s, v_s = [], [], [], [], []
    y_p = y_s = None
    for l in range(depth):
        o, cp, pp, rp, ip = _mixer_prompt(h, w_in, B, L, T, l, small, prep, cr_bd, ci_bd)
        o, g_s, vs, d_in_s, xs_re, xs_im = _mixer_sample(h, w_in, o, stc2, stp2, hre2, him2, DB, T, l,
                                                         small, prep, cr_bd, ci_bd)
        merged = _merge(h, o, w_in, w_branch, l, gate_off)
        x1, hp, logits = _out_proj(merged, x, w_out, norm_ffn3, wr, l)
        route, meta = _route(logits, rb, l, EXPERT_TILE, n_tiles_max)
        pos_flat = jnp.transpose(route[:, 4:6]).astype(I32).reshape(2 * T)
        te = meta[2, :n_tiles_max].astype(I32)
        na = meta[3, :1].astype(I32)
        tend = meta[4, :N_EXPERTS].astype(I32)
        hs = _dispatch(pos_flat, tend, hp, n_pad, EXPERT_TILE, D // 2 // LANES)
        ys = _experts(te, na, hs, expert_w_gate, expert_w_up, expert_w_down, l, EXPERT_TILE, n_tiles_max)
        if l + 1 < depth:
            x, h = _combine(pos_flat, x1, route, norm_mix3, l + 1, ys)
        else:
            y_p, y_s = _combine(pos_flat, x1, route, final3, 0, ys, n_sample=DB)

        conv_p.append(cp)
        pool_p.append(pp)
        sre_p.append(rp.reshape(B, G, P))
        sim_p.append(ip.reshape(B, G, P))
        conv_s.append(jnp.concatenate([state_conv[l][:, 1:], g_s[:, None, :]], axis=1))
        pool_s.append(jnp.concatenate([state_pool[l][:, 1:], d_in_s[:, None, :]], axis=1))
        sre_s.append(xs_re.reshape(DB, G, P))
        sim_s.append(xs_im.reshape(DB, G, P))
        v_s.append(vs[:, None, :])

    y_prompt = y_p.reshape(B, L, D)
    y_sample = y_s.reshape(DB, 1, D)
    return (y_prompt, y_sample, jnp.stack(conv_p), jnp.stack(conv_s), jnp.stack(pool_p), jnp.stack(pool_s),
            jnp.stack(sre_p), jnp.stack(sim_p), jnp.stack(sre_s), jnp.stack(sim_s), jnp.stack(v_s))
```

```python
import functools
import math

import jax
import jax.numpy as jnp
from jax import lax
from jax.experimental import pallas as pl
from jax.experimental.pallas import tpu as pltpu

F32 = jnp.float32
BF16 = jnp.bfloat16
I32 = jnp.int32
U32 = jnp.uint32

EPS = 1e-6
CHUNK = 128
SGU_GROUPS = 4
CONV_K = 31
POOL_WINDOWS = (2, 4, 8, 16)
POOL_HIST = 15
SSM_H = 16
SSM_P = 64
N_GROUPS = 4
EXP_PER_GROUP = 4
N_EXPERTS = 16
PAST_LEN = 16384

LANES = 128
SUBLANES = 8
VMEM_LIMIT = 60 * 1024 * 1024

EXPERT_TILE = 256
TIME_TILE = 256
SCAN_COLS = 512


def _cparams(n_axes):
    return pltpu.CompilerParams(dimension_semantics=("arbitrary",) * n_axes,
                                vmem_limit_bytes=VMEM_LIMIT)


def _gelu(x):
    c = math.sqrt(2.0 / math.pi)
    return 0.5 * x * (1.0 + jnp.tanh(c * (x + 0.044715 * (x * x * x))))


def _sigmoid(x):
    return 0.5 * jnp.tanh(0.5 * x) + 0.5


def _layernorm(x, g, b):
    xc = x - jnp.mean(x, axis=-1, keepdims=True)
    var = jnp.mean(xc * xc, axis=-1, keepdims=True)
    return xc * lax.rsqrt(var + EPS) * g + b


def _rms_scale(x, g):
    return x * lax.rsqrt(jnp.mean(x * x, axis=-1, keepdims=True) + EPS) * g


def _dot(a, b):
    return jnp.dot(a, b, preferred_element_type=F32)


def _cast_rows(src_ref, dst_ref, chunk):
    rows = src_ref.shape[0]

    def body(i, c):
        r = pl.multiple_of(i * chunk, chunk)
        dst_ref[pl.ds(r, chunk), :] = src_ref[pl.ds(r, chunk), :].astype(dst_ref.dtype)
        return c

    lax.fori_loop(0, rows // chunk, body, 0)


def _layer_spec(arr, l, single=False):
    nd = arr.ndim
    mode = {"pipeline_mode": pl.Buffered(1)} if single else {}
    return pl.BlockSpec((None,) + tuple(arr.shape[1:]), lambda *_: (l,) + (0,) * (nd - 1), **mode)


def _pick(n, cands):
    for c in cands:
        if n % c == 0:
            return c
    raise ValueError(f"no tile for {n}")


def _rmsnorm_kernel(x_ref, g_ref, o_ref):
    o_ref[...] = _rms_scale(x_ref[...], g_ref[...]).astype(o_ref.dtype)


def _rmsnorm(x, g_all, l, out_dtype):
    T, D = x.shape
    tm = _pick(T, (640, 512, 384, 256, 128))
    return pl.pallas_call(
        _rmsnorm_kernel,
        out_shape=jax.ShapeDtypeStruct((T, D), out_dtype),
        grid=(T // tm,),
        in_specs=[pl.BlockSpec((tm, D), lambda m: (m, 0)), _layer_spec(g_all, l)],
        out_specs=pl.BlockSpec((tm, D), lambda m: (m, 0)),
        compiler_params=_cparams(1),
        name="rmsnorm",
    )(x, g_all)


def _split3(x):
    p1 = x.astype(BF16)
    r1 = x - p1.astype(F32)
    p2 = r1.astype(BF16)
    p3 = (r1 - p2.astype(F32)).astype(BF16)
    return p1, p2, p3


def _prep_kernel(are, aim, ldt, bre, bim, cre, cim, sguw, wglu, poolw,
                 lamr_o, lami_o, pwr_o, pwi_o, bbr_o, bbi_o, cr_o, ci_o, sguw_o, wglu_o, poolw_o, *, S, P, H):
    a_re = are[...]
    a_im = aim[...]
    dt = jnp.exp(ldt[...])
    mag = jnp.exp(a_re * dt)
    lbr = mag * jnp.cos(a_im * dt)
    lbi = mag * jnp.sin(a_im * dt)
    den = a_re * a_re + a_im * a_im
    nr = lbr - 1.0
    kr = (nr * a_re + lbi * a_im) / den
    ki = (lbi * a_re - nr * a_im) / den
    lamr_o[...] = lbr
    lami_o[...] = lbi
    gp = a_re.shape[-1]

    w = bre.shape[0]
    sh_p, sh_h = P.bit_length() - 1, H.bit_length() - 1
    rc = 128
    pe = lax.broadcasted_iota(I32, (P, gp), 0)
    ce = lax.broadcasted_iota(I32, (P, gp), 1)
    rep_p = jnp.where((ce & (P - 1)) == pe, 1.0, 0.0).astype(BF16)
    for i in range(w // rc):
        rows = slice(i * rc, (i + 1) * rc)
        r_i = lax.broadcasted_iota(I32, (rc, gp), 0) + i * rc
        c_i = lax.broadcasted_iota(I32, (rc, gp), 1)
        diag = (r_i >> sh_h) == (c_i >> sh_p)
        br = sum(_dot(piece, rep_p) for piece in _split3(bre[rows, :]))
        bi = sum(_dot(piece, rep_p) for piece in _split3(bim[rows, :]))
        bbr_o[rows, :] = jnp.where(diag, kr * br - ki * bi, 0.0).astype(BF16)
        bbi_o[rows, :] = jnp.where(diag, kr * bi + ki * br, 0.0).astype(BF16)

    he = lax.broadcasted_iota(I32, (H, w), 0)
    ce = lax.broadcasted_iota(I32, (H, w), 1)
    rep_h = jnp.where((ce & (H - 1)) == he, 1.0, 0.0).astype(BF16)
    rc = 256
    for i in range(gp // rc):
        rows = slice(i * rc, (i + 1) * rc)
        r_i = lax.broadcasted_iota(I32, (rc, w), 0) + i * rc
        c_i = lax.broadcasted_iota(I32, (rc, w), 1)
        diag = (r_i >> sh_p) == (c_i >> sh_h)
        cr_o[rows, :] = jnp.where(diag, _dot(cre[rows, :].astype(BF16), rep_h), 0.0).astype(BF16)
        ci_o[rows, :] = jnp.where(diag, _dot(cim[rows, :].astype(BF16), rep_h), 0.0).astype(BF16)

    pr, pi = lbr, lbi
    for s in range(S):
        pwr_o[SUBLANES * s:SUBLANES * (s + 1), :] = jnp.broadcast_to(pr, (SUBLANES, gp))
        pwi_o[SUBLANES * s:SUBLANES * (s + 1), :] = jnp.broadcast_to(pi, (SUBLANES, gp))
        pr, pi = pr * lbr - pi * lbi, pr * lbi + pi * lbr

    t_i = lax.broadcasted_iota(I32, (CHUNK, CHUNK), 0)
    s_i = lax.broadcasted_iota(I32, (CHUNK, CHUNK), 1)
    for g in range(SGU_GROUPS):
        sguw_o[g] = jnp.where(t_i >= s_i, sguw[g], 0.0).astype(BF16)
    wglu_o[...] = wglu[...].astype(BF16)
    for g in range(len(POOL_WINDOWS)):
        poolw_o[g] = poolw[g].astype(BF16)


def _prep(a_re, a_im, log_dt, b_re, b_im, c_re, c_im, sgu_w, w_glu, pool_w, S):
    L, G, P = a_re.shape
    H = b_re.shape[-1]
    GP, W = G * P, G * H
    assert P & (P - 1) == 0 and H & (H - 1) == 0
    bre_t = jnp.swapaxes(b_re, 2, 3).reshape(L, W, P)
    bim_t = jnp.swapaxes(b_im, 2, 3).reshape(L, W, P)
    cre_t = jnp.swapaxes(c_re, 2, 3).reshape(L, GP, H)
    cim_t = jnp.swapaxes(c_im, 2, 3).reshape(L, GP, H)
    are2 = a_re.reshape(L, 1, GP)
    aim2 = a_im.reshape(L, 1, GP)
    ldt2 = jnp.repeat(log_dt, P, axis=-1).reshape(L, 1, GP)

    def lspec(shape):
        nd = len(shape)
        return pl.BlockSpec((None,) + tuple(shape[1:]), lambda l: (l,) + (0,) * (nd - 1))

    ins = [are2, aim2, ldt2, bre_t, bim_t, cre_t, cim_t, sgu_w, w_glu, pool_w]
    out_shapes = [
        jax.ShapeDtypeStruct((L, 1, GP), F32), jax.ShapeDtypeStruct((L, 1, GP), F32),
        jax.ShapeDtypeStruct((L, SUBLANES * S, GP), F32), jax.ShapeDtypeStruct((L, SUBLANES * S, GP), F32),
        jax.ShapeDtypeStruct((L, W, GP), BF16), jax.ShapeDtypeStruct((L, W, GP), BF16),
        jax.ShapeDtypeStruct((L, GP, W), BF16), jax.ShapeDtypeStruct((L, GP, W), BF16),
        jax.ShapeDtypeStruct(sgu_w.shape, BF16), jax.ShapeDtypeStruct(w_glu.shape, BF16),
        jax.ShapeDtypeStruct(pool_w.shape, BF16),
    ]
    return pl.pallas_call(
        functools.partial(_prep_kernel, S=S, P=P, H=H),
        out_shape=out_shapes,
        grid=(L,),
        in_specs=[lspec(a.shape) for a in ins],
        out_specs=[lspec(o.shape) for o in out_shapes],
        compiler_params=_cparams(1),
        name="ssm_prep",
    )(*ins)


PROJ_CHUNK = 256


def _stream_weight_cols(w_hbm, l, n_cols, stage, sem, consume):
    def copy(c, slot):
        return pltpu.make_async_copy(w_hbm.at[l, :, pl.ds(c * PROJ_CHUNK, PROJ_CHUNK)], stage.at[slot],
                                     sem.at[slot])

    n = n_cols // PROJ_CHUNK
    copy(0, 0).start()
    for c in range(n):
        slot = c % 2
        if c + 1 < n:
            copy(c + 1, 1 - slot).start()
        copy(c, slot).wait()
        consume(c, slot)


N_MIXER_INPUTS = 23


def _mixer_prompt_kernel(*refs, n_batch, **kw):
    o_ref = refs[N_MIXER_INPUTS]
    b = pl.program_id(0)

    @pl.when(b < n_batch)
    def _():
        _mixer_prompt_body(*refs, **kw)

    @pl.when((b == n_batch) & (pl.program_id(1) == 0))
    def _():
        o_ref[...] = jnp.zeros_like(o_ref)


def _mixer_prompt_body(h_ref, w_hbm, lng, lnb, sguw, sgub, cw, cb, clg, clb,
                       lamr, lami, pwr, pwi, bbr, bbi, cr, ci, sd, wglu, bglu, poolw, pscale,
                       o_ref, conv_o, pool_o, sre_o, sim_o,
                       w6bf, wstage, wsem, p_ref,
                       gbuf, gsh, pbuf, xr, xi, car_re, car_im, fin_re, fin_im, cm_re, cm_im,
                       *, l, tt, S, W, GP):
    j = pl.program_id(1)
    nt = pl.num_programs(1)
    GH = 32
    PH = 16

    @pl.when((pl.program_id(0) == 0) & (j == 0))
    def _():
        def consume(c, slot):
            _cast_rows(wstage.at[slot], w6bf.at[:, c * PROJ_CHUNK:(c + 1) * PROJ_CHUNK], 256)

        _stream_weight_cols(w_hbm, l, 6 * W, wstage, wsem, consume)

    @pl.when(j == 0)
    def _():
        gbuf[0:GH, :] = jnp.zeros((GH, W), F32)
        pbuf[0:PH, :] = jnp.zeros((PH, W), F32)
        car_re[...] = jnp.zeros_like(car_re)
        car_im[...] = jnp.zeros_like(car_im)

    h = h_ref[...]
    for c in range(3):
        p_ref[:, 2 * c * W:2 * (c + 1) * W] = _dot(h, w6bf[:, 2 * c * W:2 * (c + 1) * W])

    for c in range(tt // CHUNK):
        rows = slice(c * CHUNK, (c + 1) * CHUNK)
        u = _gelu(p_ref[rows, 0:W])
        v = _layernorm(_gelu(p_ref[rows, W:2 * W]), lng[...], lnb[...])
        vb = v.astype(BF16)
        gw = W // SGU_GROUPS
        for g in range(SGU_GROUPS):
            cols = slice(g * gw, (g + 1) * gw)
            mixed = _dot(sguw[g], vb[:, cols]) + sgub[:, g:g + 1]
            o_ref[rows, cols] = (u[:, cols] * mixed).astype(BF16)

    rc = 64
    for c in range(tt // rc):
        rows = slice(c * rc, (c + 1) * rc)
        gbuf[GH + c * rc:GH + (c + 1) * rc, :] = p_ref[rows, 2 * W:3 * W] * _sigmoid(p_ref[rows, 3 * W:4 * W])
    sh_rows = gsh.shape[1]
    for b in range(1, SUBLANES):
        gsh[b] = gbuf[b:b + sh_rows, :]
    rc = 32
    base = GH - (CONV_K - 1)
    for c in range(tt // rc):
        acc = jnp.zeros((rc, W), F32)
        for k in range(CONV_K):
            r0 = base + c * rc + k
            b, a0 = r0 % SUBLANES, r0 - r0 % SUBLANES
            rows = gbuf[a0:a0 + rc, :] if b == 0 else gsh[b, a0:a0 + rc, :]
            wk = jnp.concatenate([cw[SUBLANES * k:SUBLANES * (k + 1), :]] * (rc // SUBLANES), axis=0)
            acc = acc + wk * rows
        y = _layernorm(acc + cb[...], clg[...], clb[...])
        o_ref[c * rc:(c + 1) * rc, W:2 * W] = (y * _sigmoid(y)).astype(BF16)
    gbuf[0:GH, :] = gbuf[tt:tt + GH, :]

    uc = p_ref[:, 4 * W:5 * W]
    r_i = lax.broadcasted_iota(I32, (tt, tt), 0)
    c_i = lax.broadcasted_iota(I32, (tt, tt), 1)
    perm = jnp.where(((r_i & (SUBLANES - 1)) * S + (r_i >> 3)) == c_i, 1.0, 0.0).astype(BF16)
    up = _dot(perm, uc.astype(BF16)).astype(BF16)
    hw, hg = W // 2, GP // 2
    for q in range(2):
        xr[:, q * hg:(q + 1) * hg] = _dot(up[:, q * hw:(q + 1) * hw], bbr[q * hw:(q + 1) * hw, q * hg:(q + 1) * hg])
        xi[:, q * hg:(q + 1) * hg] = _dot(up[:, q * hw:(q + 1) * hw], bbi[q * hw:(q + 1) * hw, q * hg:(q + 1) * hg])

    for cbi in range(GP // SCAN_COLS):
        cols = slice(cbi * SCAN_COLS, (cbi + 1) * SCAN_COLS)
        lr = jnp.broadcast_to(lamr[:, cols], (SUBLANES, SCAN_COLS))
        li = jnp.broadcast_to(lami[:, cols], (SUBLANES, SCAN_COLS))

        def step(s, carry, cols=cols, lr=lr, li=li):
            sr, si = carry
            r0 = pl.multiple_of(s * SUBLANES, SUBLANES)
            nr = lr * sr - li * si + xr[pl.ds(r0, SUBLANES), cols]
            ni = lr * si + li * sr + xi[pl.ds(r0, SUBLANES), cols]
            xr[pl.ds(r0, SUBLANES), cols] = nr
            xi[pl.ds(r0, SUBLANES), cols] = ni
            return nr, ni

        z = jnp.zeros((SUBLANES, SCAN_COLS), F32)
        fr, fi = lax.fori_loop(0, S, step, (z, z), unroll=True)
        fin_re[:, cols] = fr
        fin_im[:, cols] = fi

    lsr = pwr[SUBLANES * (S - 1):SUBLANES * (S - 1) + 1, :]
    lsi = pwi[SUBLANES * (S - 1):SUBLANES * (S - 1) + 1, :]
    c_r = car_re[0:1, :]
    c_im = car_im[0:1, :]
    cm_re[0:1, :] = c_r
    cm_im[0:1, :] = c_im
    for q in range(1, SUBLANES):
        f_r = fin_re[q - 1:q, :]
        f_i = fin_im[q - 1:q, :]
        c_r, c_im = f_r + lsr * c_r - lsi * c_im, f_i + lsr * c_im + lsi * c_r
        cm_re[q:q + 1, :] = c_r
        cm_im[q:q + 1, :] = c_im
    n_r = fin_re[SUBLANES - 1:SUBLANES, :] + lsr * c_r - lsi * c_im
    n_i = fin_im[SUBLANES - 1:SUBLANES, :] + lsr * c_im + lsi * c_r
    car_re[0:1, :] = n_r
    car_im[0:1, :] = n_i

    for cbi in range(GP // SCAN_COLS):
        cols = slice(cbi * SCAN_COLS, (cbi + 1) * SCAN_COLS)
        mr = cm_re[:, cols]
        mi = cm_im[:, cols]

        def fix(s, c, cols=cols, mr=mr, mi=mi):
            r0 = pl.multiple_of(s * SUBLANES, SUBLANES)
            pr = pwr[pl.ds(r0, SUBLANES), cols]
            pi = pwi[pl.ds(r0, SUBLANES), cols]
            xr[pl.ds(r0, SUBLANES), cols] = xr[pl.ds(r0, SUBLANES), cols] + (pr * mr - pi * mi)
            xi[pl.ds(r0, SUBLANES), cols] = xi[pl.ds(r0, SUBLANES), cols] + (pr * mi + pi * mr)
            return c

        lax.fori_loop(0, S, fix, 0, unroll=True)

    yp = jnp.concatenate(
        [_dot(xr[:, q * hg:(q + 1) * hg].astype(BF16), cr[q * hg:(q + 1) * hg, q * hw:(q + 1) * hw])
         - _dot(xi[:, q * hg:(q + 1) * hg].astype(BF16), ci[q * hg:(q + 1) * hg, q * hw:(q + 1) * hw])
         for q in range(2)], axis=1)
    unperm = jnp.where(((c_i & (SUBLANES - 1)) * S + (c_i >> 3)) == r_i, 1.0, 0.0).astype(BF16)
    y1 = yp.astype(BF16)
    r1 = yp - y1.astype(F32)
    y2 = r1.astype(BF16)
    y3 = (r1 - y2.astype(F32)).astype(BF16)
    y = _dot(unperm, y1) + _dot(unperm, y2) + _dot(unperm, y3)
    z = _gelu(y + sd[...] * uc)
    gl = _dot(z.astype(BF16), wglu[...]) + bglu[...]
    o_ref[:, 2 * W:3 * W] = (z * _sigmoid(gl)).astype(BF16)

    xd = p_ref[:, 5 * W:6 * W]
    pbuf[PH:PH + tt, :] = xd
    gw = W // len(POOL_WINDOWS)
    pos = (j * tt + lax.broadcasted_iota(I32, (tt, 1), 0) + 1).astype(F32)
    for gi, win in enumerate(POOL_WINDOWS):
        cols = slice(gi * gw, (gi + 1) * gw)
        s = xd[:, cols]
        for i in range(1, win):
            s = s + pbuf[PH - i:PH - i + tt, cols]
        cnt = jnp.minimum(pos, float(win))
        pooled = s / cnt - xd[:, cols]
        mixed = _dot(pooled.astype(BF16), poolw[gi]) * pscale[:, cols]
        o_ref[:, 3 * W + gi * gw:3 * W + (gi + 1) * gw] = mixed.astype(BF16)
    pbuf[0:PH, :] = pbuf[tt:tt + PH, :]

    @pl.when(j == nt - 1)
    def _():
        conv_o[...] = gbuf[GH - (CONV_K - 1):GH, :]
        pool_o[...] = pbuf[PH - POOL_HIST:PH, :]
        sre_o[...] = car_re[0:1, :]
        sim_o[...] = car_im[0:1, :]


def _mixer_prompt(h, w_in, B, L, T, l, small, prep, cr, ci):
    W = small["sgu_ln_g"].shape[-1]
    GP = prep["lamr"].shape[-1]
    D = h.shape[1]
    D4 = 4 * W
    tt = TIME_TILE
    S = tt // SUBLANES
    nt = L // tt
    ins = [h, w_in,
           small["sgu_ln_g"], small["sgu_ln_b"], prep["sguw"], small["sgu_bT"],
           small["conv_w8"], small["conv_b"], small["conv_ln_g"], small["conv_ln_b"],
           prep["lamr"], prep["lami"], prep["pwr"], prep["pwi"], prep["bbr"], prep["bbi"], cr, ci,
           small["ssm_d"], prep["wglu"], small["ssm_b_glu"], prep["poolw"], small["pool_scale"]]
    assert len(ins) == N_MIXER_INPUTS
    in_specs = [pl.BlockSpec((tt, D), lambda b, j: (jnp.minimum(b * nt + j, B * nt), 0)),
                pl.BlockSpec(memory_space=pl.ANY)]
    in_specs += [_layer_spec(a, l, single=True) for a in ins[2:]]
    out_shapes = [
        jax.ShapeDtypeStruct((T, D4), BF16),
        jax.ShapeDtypeStruct((B, CONV_K - 1, W), F32),
        jax.ShapeDtypeStruct((B, POOL_HIST, W), F32),
        jax.ShapeDtypeStruct((B, 1, GP), F32),
        jax.ShapeDtypeStruct((B, 1, GP), F32),
    ]
    out_specs = [
        pl.BlockSpec((tt, D4), lambda b, j: (jnp.minimum(b * nt + j, B * nt), 0)),
        pl.BlockSpec((None, CONV_K - 1, W), lambda b, j: (jnp.minimum(b, B - 1), 0, 0)),
        pl.BlockSpec((None, POOL_HIST, W), lambda b, j: (jnp.minimum(b, B - 1), 0, 0)),
        pl.BlockSpec((None, 1, GP), lambda b, j: (jnp.minimum(b, B - 1), 0, 0)),
        pl.BlockSpec((None, 1, GP), lambda b, j: (jnp.minimum(b, B - 1), 0, 0)),
    ]
    scratch = [
        pltpu.VMEM((D, 6 * W), BF16), pltpu.VMEM((2, D, PROJ_CHUNK), F32), pltpu.SemaphoreType.DMA((2,)),
        pltpu.VMEM((tt, 6 * W), F32),
        pltpu.VMEM((32 + tt, W), F32), pltpu.VMEM((SUBLANES, 32 + tt - SUBLANES, W), F32),
        pltpu.VMEM((16 + tt, W), F32),
        pltpu.VMEM((tt, GP), F32), pltpu.VMEM((tt, GP), F32),
        pltpu.VMEM((SUBLANES, GP), F32), pltpu.VMEM((SUBLANES, GP), F32),
        pltpu.VMEM((SUBLANES, GP), F32), pltpu.VMEM((SUBLANES, GP), F32),
        pltpu.VMEM((SUBLANES, GP), F32), pltpu.VMEM((SUBLANES, GP), F32),
    ]
    return pl.pallas_call(
        functools.partial(_mixer_prompt_kernel, n_batch=B, l=l, tt=tt, S=S, W=W, GP=GP),
        out_shape=out_shapes,
        grid=(B + 1, nt),
        in_specs=in_specs,
        out_specs=out_specs,
        scratch_shapes=scratch,
        compiler_params=_cparams(2),
        name="mixer_prompt",
    )(*ins)


def _mixer_sample_kernel(h_ref, w_hbm, stc, stp, hre, him, lng, lnb, w00, b0, cw, cb, clg, clb,
                         lamr, lami, bbr, bbi, cr, ci, sd, wglu, bglu, poolw, pscale, o_in,
                         o_ref, g_o, v_o, d_o, xre_o, xim_o,
                         p_ref, wstage, wchunk, wsem, *, l, W):
    del o_in
    h = h_ref[...]

    def consume(c, slot):
        _cast_rows(wstage.at[slot], wchunk, 256)
        p_ref[:, c * PROJ_CHUNK:(c + 1) * PROJ_CHUNK] = _dot(h, wchunk[...])

    _stream_weight_cols(w_hbm, l, 6 * W, wstage, wsem, consume)

    u = _gelu(p_ref[:, 0:W])
    v = _layernorm(_gelu(p_ref[:, W:2 * W]), lng[...], lnb[...])
    v_o[...] = v
    o_ref[:, 0:W] = (u * (w00[...] * v + b0[...])).astype(BF16)

    g = p_ref[:, 2 * W:3 * W] * _sigmoid(p_ref[:, 3 * W:4 * W])
    g_o[...] = g
    acc = cw[CONV_K - 1:CONV_K, :] * g
    for k in range(CONV_K - 1):
        acc = acc + cw[k:k + 1, :] * stc[:, k * W:(k + 1) * W]
    y = _layernorm(acc + cb[...], clg[...], clb[...])
    o_ref[:, W:2 * W] = (y * _sigmoid(y)).astype(BF16)

    uc = p_ref[:, 4 * W:5 * W]
    ub = uc.astype(BF16)
    h_r = hre[...]
    h_i = him[...]
    l_r = lamr[...]
    l_i = lami[...]
    x_r = l_r * h_r - l_i * h_i + _dot(ub, bbr[...])
    x_i = l_r * h_i + l_i * h_r + _dot(ub, bbi[...])
    xre_o[...] = x_r
    xim_o[...] = x_i
    yv = _dot(x_r.astype(BF16), cr[...]) - _dot(x_i.astype(BF16), ci[...]) + sd[...] * uc
    z = _gelu(yv)
    gl = _dot(z.astype(BF16), wglu[...]) + bglu[...]
    o_ref[:, 2 * W:3 * W] = (z * _sigmoid(gl)).astype(BF16)

    xd = p_ref[:, 5 * W:6 * W]
    d_o[...] = xd
    gw = W // len(POOL_WINDOWS)
    for gi, win in enumerate(POOL_WINDOWS):
        cols = slice(gi * gw, (gi + 1) * gw)
        s = xd[:, cols]
        for i in range(1, win):
            r = POOL_HIST - i
            s = s + stp[:, r * W + gi * gw:r * W + (gi + 1) * gw]
        cnt = float(min(PAST_LEN + 1, win))
        pooled = s / cnt - xd[:, cols]
        mixed = _dot(pooled.astype(BF16), poolw[gi]) * pscale[:, cols]
        o_ref[:, 3 * W + gi * gw:3 * W + (gi + 1) * gw] = mixed.astype(BF16)


def _mixer_sample(h, w_in, o_prev, stc2, stp2, hre2, him2, DB, T, l, small, prep, cr, ci):
    W = small["sgu_ln_g"].shape[-1]
    GP = prep["lamr"].shape[-1]
    D = h.shape[1]
    D4 = 4 * W
    blk = (T - DB) // DB
    ins = [h, w_in, stc2, stp2, hre2, him2,
           small["sgu_ln_g"], small["sgu_ln_b"], small["sgu_w00"], small["sgu_b0"],
           small["conv_w"], small["conv_b"], small["conv_ln_g"], small["conv_ln_b"],
           prep["lamr"], prep["lami"], prep["bbr"], prep["bbi"], cr, ci,
           small["ssm_d"], prep["wglu"], small["ssm_b_glu"], prep["poolw"], small["pool_scale"], o_prev]
    in_specs = [pl.BlockSpec((DB, D), lambda i: (blk, 0)), pl.BlockSpec(memory_space=pl.ANY)]
    in_specs += [_layer_spec(a, l, single=True) for a in ins[2:-1]]
    in_specs += [pl.BlockSpec(memory_space=pl.ANY)]
    out_shapes = [
        jax.ShapeDtypeStruct((T, D4), BF16),
        jax.ShapeDtypeStruct((DB, W), F32), jax.ShapeDtypeStruct((DB, W), F32), jax.ShapeDtypeStruct((DB, W), F32),
        jax.ShapeDtypeStruct((DB, GP), F32), jax.ShapeDtypeStruct((DB, GP), F32),
    ]
    out_specs = [
        pl.BlockSpec((DB, D4), lambda i: (blk, 0)),
        pl.BlockSpec((DB, W), lambda i: (0, 0)), pl.BlockSpec((DB, W), lambda i: (0, 0)),
        pl.BlockSpec((DB, W), lambda i: (0, 0)),
        pl.BlockSpec((DB, GP), lambda i: (0, 0)), pl.BlockSpec((DB, GP), lambda i: (0, 0)),
    ]
    return pl.pallas_call(
        functools.partial(_mixer_sample_kernel, l=l, W=W),
        out_shape=out_shapes,
        grid=(1,),
        in_specs=in_specs,
        out_specs=out_specs,
        scratch_shapes=[pltpu.VMEM((DB, 6 * W), F32), pltpu.VMEM((2, D, PROJ_CHUNK), F32),
                        pltpu.VMEM((D, PROJ_CHUNK), BF16), pltpu.SemaphoreType.DMA((2,))],
        input_output_aliases={len(ins) - 1: 0},
        compiler_params=_cparams(1),
        name="mixer_sample",
    )(*ins)


def _merge_kernel(h_ref, o_ref, wg0, wg1, wg2, wg3, wb_ref, m_ref, wgbf, wbbf, *, W):
    wgs = (wg0, wg1, wg2, wg3)

    @pl.when(pl.program_id(1) == 0)
    def _():
        for b in range(4):
            _cast_rows(wgs[b], wgbf.at[b], 256)
            wbbf[b] = wb_ref[b].astype(BF16)

    h = h_ref[...]
    acc = None
    for b in range(4):
        gate = _sigmoid(_dot(h, wgbf[b]))
        term = gate * _dot(o_ref[:, b * W:(b + 1) * W], wbbf[b])
        acc = term if acc is None else acc + term
    m_ref[...] = acc.astype(BF16)


def _merge(h, o, w_in, w_branch, l, gate_off):
    T, D = h.shape
    W = w_branch.shape[2]
    tm = _pick(T, (640, 512, 384, 256, 128))
    tn = 256
    nb = D // tn

    def gate_spec(b):
        blk0 = (gate_off + b * D) // tn
        return pl.BlockSpec((None, D, tn), lambda n, m: (l, 0, blk0 + n))

    return pl.pallas_call(
        functools.partial(_merge_kernel, W=W),
        out_shape=jax.ShapeDtypeStruct((T, D), BF16),
        grid=(nb, T // tm),
        in_specs=[pl.BlockSpec((tm, D), lambda n, m: (m, 0)),
                  pl.BlockSpec((tm, 4 * W), lambda n, m: (m, 0)),
                  gate_spec(0), gate_spec(1), gate_spec(2), gate_spec(3),
                  pl.BlockSpec((None, 4, W, tn), lambda n, m: (l, 0, 0, n))],
        out_specs=pl.BlockSpec((tm, tn), lambda n, m: (m, n)),
        scratch_shapes=[pltpu.VMEM((4, D, tn), BF16), pltpu.VMEM((4, W, tn), BF16)],
        compiler_params=_cparams(2),
        name="gated_merge",
    )(h, o, w_in, w_in, w_in, w_in, w_branch)


def _out_kernel(m_ref, x_ref, w_ref, g_ref, wr_ref, x1_ref, hp_ref, lg_ref, wbf, wr2):
    R = wr_ref.shape[-1]

    @pl.when(pl.program_id(0) == 0)
    def _():
        _cast_rows(w_ref, wbf, 256)
        wr = wr_ref[...]
        hi = wr.astype(BF16)
        wr2[:, 0:R] = hi
        wr2[:, R:2 * R] = (wr - hi.astype(F32)).astype(BF16)

    x1 = x_ref[...] + _dot(m_ref[...], wbf[...])
    x1_ref[...] = x1
    h = _rms_scale(x1, g_ref[...])
    hb = h.astype(BF16)
    hl = (h - hb.astype(F32)).astype(BF16)
    both = _dot(hb, wr2[...])
    lg_ref[...] = both[:, 0:R] + both[:, R:2 * R] + _dot(hl, wr2[:, 0:R])
    bits = lax.bitcast_convert_type(hb.astype(F32), U32)
    half = bits.shape[1] // 2
    word = (bits[:, :half] >> 16) | (bits[:, half:] & jnp.uint32(0xFFFF0000))
    tm = word.shape[0]
    for j in range(half // LANES):
        hp_ref[pl.ds(j, tm, stride=half // LANES), :] = word[:, j * LANES:(j + 1) * LANES]


def _out_proj(merged, x, w_out, g_all, wr_all, l):
    T, D = x.shape
    tm = _pick(T, (320, 256, 128))
    R = wr_all.shape[-1]
    return pl.pallas_call(
        _out_kernel,
        out_shape=[jax.ShapeDtypeStruct((T, D), F32), jax.ShapeDtypeStruct((T * (D // 2 // LANES), LANES), U32),
                   jax.ShapeDtypeStruct((T, R), F32)],
        grid=(T // tm,),
        in_specs=[pl.BlockSpec((tm, D), lambda m: (m, 0)),
                  pl.BlockSpec((tm, D), lambda m: (m, 0)),
                  pl.BlockSpec((None, D, D), lambda m: (l, 0, 0), pipeline_mode=pl.Buffered(1)),
                  _layer_spec(g_all, l), _layer_spec(wr_all, l)],
        out_specs=[pl.BlockSpec((tm, D), lambda m: (m, 0)),
                   pl.BlockSpec((tm * (D // 2 // LANES), LANES), lambda m: (m, 0)),
                   pl.BlockSpec((tm, R), lambda m: (m, 0))],
        scratch_shapes=[pltpu.VMEM((D, D), BF16), pltpu.VMEM((D, 2 * R), BF16)],
        compiler_params=_cparams(1),
        name="out_proj",
    )(merged, x, w_out, g_all, wr_all)


def _route_kernel(lg_ref, bias_ref, rt_ref, meta_ref, cnt, off, *, tm, tile, n_tiles_max):
    p = pl.program_id(0)
    m = pl.program_id(1)
    lane = lax.broadcasted_iota(I32, (tm, LANES), 1).astype(F32)
    neg = jnp.float32(-jnp.inf)
    big = jnp.float32(1e9)

    @pl.when((p == 0) & (m == 0))
    def _():
        cnt[...] = jnp.zeros_like(cnt)

    lg = lg_ref[...] + bias_ref[...]
    is_g = lane < N_GROUPS
    gl = jnp.where(is_g, lg, neg)
    gmax = jnp.max(gl, axis=-1, keepdims=True)
    gidx = jnp.min(jnp.where(gl == gmax, lane, big), axis=-1, keepdims=True)
    gsum = jnp.sum(jnp.where(is_g, jnp.exp(gl - gmax), 0.0), axis=-1, keepdims=True)
    g_w = 1.0 / gsum
    lo = N_GROUPS + EXP_PER_GROUP * gidx
    in_grp = (lane >= lo) & (lane < lo + EXP_PER_GROUP)
    el = jnp.where(in_grp, lg, neg)
    v1 = jnp.max(el, axis=-1, keepdims=True)
    i1 = jnp.min(jnp.where(el == v1, lane, big), axis=-1, keepdims=True)
    el2 = jnp.where(lane == i1, neg, el)
    v2 = jnp.max(el2, axis=-1, keepdims=True)
    i2 = jnp.min(jnp.where(el2 == v2, lane, big), axis=-1, keepdims=True)
    e2x = jnp.exp(v2 - v1)
    w1 = g_w / (1.0 + e2x)
    w2 = g_w * e2x / (1.0 + e2x)
    e1 = i1 - N_GROUPS
    e2 = i2 - N_GROUPS
    a1 = jnp.where(lane == e1, 1.0, 0.0)
    a2 = jnp.where(lane == e2, 1.0, 0.0)
    a = a1 + a2

    @pl.when(p == 0)
    def _():
        cnt[0:1, :] = cnt[0:1, :] + jnp.sum(a, axis=0, keepdims=True)

    @pl.when((p == 1) & (m == 0))
    def _():
        counts = cnt[0:1, :]
        tiles = jnp.floor((counts + (tile - 1)) * (1.0 / tile))
        tiles8 = jnp.broadcast_to(tiles, (SUBLANES, LANES)).astype(BF16)
        ri = lax.broadcasted_iota(I32, (LANES, LANES), 0)
        ci = lax.broadcasted_iota(I32, (LANES, LANES), 1)
        upper = jnp.where(ri < ci, 1.0, 0.0).astype(BF16)
        toff = _dot(tiles8, upper)[0:1, :]
        off[0:1, :] = toff * tile
        tend = toff + tiles
        lane1 = lax.broadcasted_iota(I32, (1, LANES), 1).astype(F32)
        n_act = jnp.sum(jnp.where(lane1 == N_EXPERTS - 1, tend, 0.0), axis=-1, keepdims=True)
        texp = jnp.zeros((1, LANES), F32)
        for e in range(N_EXPERTS - 1):
            end_e = jnp.sum(jnp.where(lane1 == e, tend, 0.0), axis=-1, keepdims=True)
            texp = texp + jnp.where(jnp.minimum(lane1, n_act - 1.0) >= end_e, 1.0, 0.0)
        meta_ref[0:1, :] = counts
        meta_ref[1:2, :] = off[0:1, :]
        meta_ref[2:3, :] = texp
        meta_ref[3:4, :] = jnp.broadcast_to(n_act, (1, LANES))
        meta_ref[4:5, :] = tend
        meta_ref[5:8, :] = jnp.zeros((3, LANES), F32)
        cnt[...] = jnp.zeros_like(cnt)

    @pl.when(p == 1)
    def _():
        ri = lax.broadcasted_iota(I32, (tm, tm), 0)
        ci = lax.broadcasted_iota(I32, (tm, tm), 1)
        ltri = jnp.where(ci < ri, 1.0, 0.0).astype(BF16)
        cum = _dot(ltri, a.astype(BF16)) + cnt[0:1, :] + off[0:1, :]
        pos1 = jnp.sum(a1 * cum, axis=-1, keepdims=True)
        pos2 = jnp.sum(a2 * cum, axis=-1, keepdims=True)
        cnt[0:1, :] = cnt[0:1, :] + jnp.sum(a, axis=0, keepdims=True)
        rt = jnp.where(lane == 0, e1, 0.0)
        rt = jnp.where(lane == 1, e2, rt)
        rt = jnp.where(lane == 2, w1, rt)
        rt = jnp.where(lane == 3, w2, rt)
        rt = jnp.where(lane == 4, pos1, rt)
        rt = jnp.where(lane == 5, pos2, rt)
        rt_ref[...] = rt


def _route(logits, bias_all, l, tile, n_tiles_max):
    T, R = logits.shape
    tm = _pick(T, (640, 512, 384, 256, 128))
    return pl.pallas_call(
        functools.partial(_route_kernel, tm=tm, tile=tile, n_tiles_max=n_tiles_max),
        out_shape=[jax.ShapeDtypeStruct((T, R), F32), jax.ShapeDtypeStruct((SUBLANES, LANES), F32)],
        grid=(2, T // tm),
        in_specs=[pl.BlockSpec((tm, R), lambda p, m: (m, 0)), _layer_spec(bias_all, l)],
        out_specs=[pl.BlockSpec((tm, R), lambda p, m: (m * p, 0)),
                   pl.BlockSpec((SUBLANES, LANES), lambda p, m: (0, 0))],
        scratch_shapes=[pltpu.VMEM((SUBLANES, LANES), F32), pltpu.VMEM((SUBLANES, LANES), F32)],
        compiler_params=_cparams(2),
        name="route",
    )(logits, bias_all)


def _dispatch_kernel(pos_ref, tend_ref, hp_ref, hs_ref, stage, zbuf, sem, zsem, *, T, tk, tile, n_tiles, rpt):
    i = pl.program_id(0)
    n = pl.num_programs(0)
    trows = tile * rpt

    @pl.when(i == 0)
    def _():
        zbuf[...] = jnp.zeros_like(zbuf)

        def last_tile_copy(ex):
            end = tend_ref[ex]
            start = tend_ref[ex - 1] if ex > 0 else 0
            row0 = pl.multiple_of((end - 1) * trows, trows)
            return end > start, pltpu.make_async_copy(zbuf, hs_ref.at[pl.ds(row0, trows)], zsem)

        def spare_tile_copy(k):
            idx = tend_ref[N_EXPERTS - 1] + k
            row0 = pl.multiple_of(jnp.minimum(idx, n_tiles - 1) * trows, trows)
            return idx < n_tiles, pltpu.make_async_copy(zbuf, hs_ref.at[pl.ds(row0, trows)], zsem)

        fills = [last_tile_copy(ex) for ex in range(N_EXPERTS)] + [spare_tile_copy(k) for k in range(N_EXPERTS)]
        for go, cp in fills:
            @pl.when(go)
            def _(cp=cp):
                cp.start()

        for go, cp in fills:
            @pl.when(go)
            def _(cp=cp):
                cp.wait()

    def copies(step, slot):
        out = []
        for u in range(tk):
            t = step * tk + u
            for k in range(2):
                p = pl.multiple_of(pos_ref[k * T + t] * rpt, rpt)
                out.append(pltpu.make_async_copy(stage.at[slot, pl.ds(u * rpt, rpt)], hs_ref.at[pl.ds(p, rpt)],
                                                 sem.at[slot]))
        return out

    def wait_slot(s):
        rows = hs_ref.at[pl.ds(0, 2 * tk * rpt)]
        pltpu.make_async_copy(rows, rows, sem.at[s]).wait()

    slot = lax.rem(i, 2)
    for s in range(2):
        @pl.when(slot == s)
        def _(s=s):
            stage[s] = hp_ref[...]
            for idx, cp in enumerate(copies(i, s)):
                cp.start(priority=idx % 2)

        @pl.when((slot == 1 - s) & (i > 0))
        def _(s=s):
            wait_slot(s)

        @pl.when((slot == s) & (i == n - 1))
        def _(s=s):
            wait_slot(s)


def _dispatch(pos_flat, tend, hp, n_pad, tile, rpt):
    T = hp.shape[0] // rpt
    tk = 128
    return pl.pallas_call(
        functools.partial(_dispatch_kernel, T=T, tk=tk, tile=tile, n_tiles=n_pad // tile, rpt=rpt),
        out_shape=jax.ShapeDtypeStruct((n_pad * rpt, LANES), U32),
        grid_spec=pltpu.PrefetchScalarGridSpec(
            num_scalar_prefetch=2,
            grid=(T // tk,),
            in_specs=[pl.BlockSpec((tk * rpt, LANES), lambda i, pos, tend: (i, 0))],
            out_specs=pl.BlockSpec(memory_space=pl.ANY),
            scratch_shapes=[pltpu.VMEM((2, tk * rpt, LANES), U32), pltpu.VMEM((tile * rpt, LANES), U32),
                            pltpu.SemaphoreType.DMA((2,)), pltpu.SemaphoreType.DMA],
        ),
        compiler_params=_cparams(1),
        name="dispatch",
    )(pos_flat, tend, hp)


def _expert_kernel(te_ref, na_ref, hs_ref, wg_ref, wu_ref, wd_ref, ys_ref,
                   wgst, wust, wdst, wgbf, wubf, wdbf, sem, *, l, tile):
    i = pl.program_id(0)
    na = na_ref[0]
    e = te_ref[i]
    first = (i < na) & ((i == 0) | (e != te_ref[jnp.maximum(i - 1, 0)]))

    def weight_copies(ex):
        return (pltpu.make_async_copy(wg_ref.at[l, ex], wgst, sem),
                pltpu.make_async_copy(wu_ref.at[l, ex], wust, sem),
                pltpu.make_async_copy(wd_ref.at[l, ex], wdst, sem))

    @pl.when(i == 0)
    def _():
        for cp in weight_copies(e):
            cp.start()

    @pl.when(first)
    def _():
        for cp in weight_copies(e):
            cp.wait()
        _cast_rows(wgst, wgbf, 256)
        _cast_rows(wust, wubf, 256)
        _cast_rows(wdst, wdbf, 256)
        j = lax.while_loop(lambda j: (j < na) & (te_ref[jnp.minimum(j, na - 1)] == e), lambda j: j + 1, i + 1)

        @pl.when(j < na)
        def _():
            for cp in weight_copies(te_ref[jnp.minimum(j, na - 1)]):
                cp.start()

    @pl.when(i < na)
    def _():
        rin = hs_ref.shape[0] // tile
        w = jnp.concatenate([hs_ref[pl.ds(j, tile, stride=rin), :] for j in range(rin)], axis=1)
        half = w.shape[1]
        lo = lax.bitcast_convert_type(w << 16, F32).astype(BF16)
        hi = lax.bitcast_convert_type(w & jnp.uint32(0xFFFF0000), F32).astype(BF16)
        a = _dot(lo, wgbf[0:half, :]) + _dot(hi, wgbf[half:, :])
        b = _dot(lo, wubf[0:half, :]) + _dot(hi, wubf[half:, :])
        hid = (a * _sigmoid(a) * b).astype(BF16)
        ys_ref[...] = _dot(hid, wdbf[...])

    @pl.when(i >= na)
    def _():
        ys_ref[...] = jnp.zeros_like(ys_ref)


def _experts(te, na, hs, wg, wu, wd, l, tile, n_tiles_max):
    D, F = wg.shape[-2:]
    rin = D // 2 // LANES
    n_pad = hs.shape[0] // rin

    def row_map(i, te_ref, na_ref):
        return (jnp.minimum(i, na_ref[0] - 1), 0)

    any_spec = pl.BlockSpec(memory_space=pl.ANY)
    return pl.pallas_call(
        functools.partial(_expert_kernel, l=l, tile=tile),
        out_shape=jax.ShapeDtypeStruct((n_pad, D), F32),
        grid_spec=pltpu.PrefetchScalarGridSpec(
            num_scalar_prefetch=2,
            grid=(n_tiles_max,),
            in_specs=[pl.BlockSpec((tile * rin, LANES), row_map), any_spec, any_spec, any_spec],
            out_specs=pl.BlockSpec((tile, D), lambda i, te_ref, na_ref: (i, 0)),
            scratch_shapes=[pltpu.VMEM((D, F), F32), pltpu.VMEM((D, F), F32), pltpu.VMEM((F, D), F32),
                            pltpu.VMEM((D, F), BF16), pltpu.VMEM((D, F), BF16), pltpu.VMEM((F, D), BF16),
                            pltpu.SemaphoreType.DMA],
        ),
        compiler_params=_cparams(1),
        name="experts",
    )(te, na, hs, wg, wu, wd)


def _combine_kernel(pos_ref, x1_ref, rt_ref, g_ref, ys_ref, *rest, T, tk, n_prompt):
    final = n_prompt is not None
    out_a, out_b, buf, sem = rest
    i = pl.program_id(0)
    n = pl.num_programs(0)

    def copies(step, slot):
        out = []
        for u in range(tk):
            t = step * tk + u
            for k in range(2):
                p = pos_ref[k * T + t]
                out.append(pltpu.make_async_copy(ys_ref.at[pl.ds(p, 1)], buf.at[slot, k, pl.ds(u, 1)],
                                                 sem.at[slot]))
        return out

    slot = lax.rem(i, 2)

    @pl.when(i == 0)
    def _():
        for idx, cp in enumerate(copies(0, 0)):
            cp.start(priority=idx % 2)

    for s in range(2):
        @pl.when((i + 1 < n) & (slot == 1 - s))
        def _(s=s):
            for idx, cp in enumerate(copies(i + 1, s)):
                cp.start(priority=idx % 2)

    for s in range(2):
        @pl.when(slot == s)
        def _(s=s):
            pltpu.make_async_copy(buf.at[s], buf.at[s], sem.at[s]).wait()
            rt = rt_ref[...]
            w1 = rt[:, 2:3]
            w2 = rt[:, 3:4]
            x2 = x1_ref[...] + w1 * buf[s, 0] + w2 * buf[s, 1]
            hn = _rms_scale(x2, g_ref[...])
            if not final:
                out_a[...] = x2
                out_b[...] = hn.astype(out_b.dtype)
            else:
                @pl.when(i < n_prompt)
                def _():
                    out_a[...] = hn

                @pl.when(i >= n_prompt)
                def _():
                    out_b[...] = hn


def _combine(pos_flat, x1, route, g_all, gl, ys, n_sample=None):
    T, D = x1.shape
    tk = 128
    R = route.shape[1]
    if n_sample is None:
        n_prompt = None
        g_spec = pl.BlockSpec((None, 1, D), lambda i, pos: (gl, 0, 0))
        out_shape = [jax.ShapeDtypeStruct((T, D), F32), jax.ShapeDtypeStruct((T, D), BF16)]
        out_specs = [pl.BlockSpec((tk, D), lambda i, pos: (i, 0)), pl.BlockSpec((tk, D), lambda i, pos: (i, 0))]
    else:
        assert n_sample == tk
        n_prompt = (T - n_sample) // tk
        g_spec = pl.BlockSpec((1, D), lambda i, pos: (0, 0))
        out_shape = [jax.ShapeDtypeStruct((T - n_sample, D), F32), jax.ShapeDtypeStruct((n_sample, D), F32)]
        out_specs = [pl.BlockSpec((tk, D), lambda i, pos: (jnp.minimum(i, n_prompt - 1), 0)),
                     pl.BlockSpec((tk, D), lambda i, pos: (0, 0))]
    return pl.pallas_call(
        functools.partial(_combine_kernel, T=T, tk=tk, n_prompt=n_prompt),
        out_shape=out_shape,
        grid_spec=pltpu.PrefetchScalarGridSpec(
            num_scalar_prefetch=1,
            grid=(T // tk,),
            in_specs=[pl.BlockSpec((tk, D), lambda i, pos: (i, 0)),
                      pl.BlockSpec((tk, R), lambda i, pos: (i, 0)),
                      g_spec,
                      pl.BlockSpec(memory_space=pl.ANY)],
            out_specs=out_specs,
            scratch_shapes=[pltpu.VMEM((2, 2, tk, D), F32), pltpu.SemaphoreType.DMA((2,))],
        ),
        compiler_params=_cparams(1),
        name="combine",
    )(pos_flat, x1, route, g_all, ys)


def kernel(x_prompt, x_sample, state_conv, state_pool, state_ssm_re, state_ssm_im,
           norm_mix_g, norm_ffn_g, w_in,
           sgu_ln_g, sgu_ln_b, sgu_w, sgu_b,
           conv_w, conv_b, conv_ln_g, conv_ln_b,
           ssm_a_re, ssm_a_im, ssm_log_dt, ssm_b_re, ssm_b_im, ssm_c_re, ssm_c_im, ssm_d, ssm_w_glu, ssm_b_glu,
           pool_w, pool_scale,
           w_branch, w_out,
           router_group_w, router_group_b, router_expert_w, router_expert_b,
           expert_w_gate, expert_w_up, expert_w_down,
           final_norm_g):
    B, L, D = x_prompt.shape
    DB = x_sample.shape[0]
    depth = w_in.shape[0]
    W = sgu_ln_g.shape[-1]
    G, P = ssm_a_re.shape[1:]
    GP = G * P
    T = B * L + DB
    gate_off = 6 * W
    assert x_sample.shape[1] == 1 and L % TIME_TILE == 0 and (B * L) % DB == 0

    row = lambda a: a.reshape(depth, 1, a.shape[-1])
    small = {
        "sgu_ln_g": row(sgu_ln_g), "sgu_ln_b": row(sgu_ln_b),
        "sgu_bT": jnp.swapaxes(sgu_b, 1, 2),
        "sgu_w00": row(jnp.repeat(sgu_w[:, :, 0, 0], W // SGU_GROUPS, axis=-1)),
        "sgu_b0": row(jnp.repeat(sgu_b[:, :, 0], W // SGU_GROUPS, axis=-1)),
        "conv_w": conv_w, "conv_w8": jnp.repeat(conv_w, SUBLANES, axis=1), "conv_b": row(conv_b), "conv_ln_g": row(conv_ln_g), "conv_ln_b": row(conv_ln_b),
        "ssm_d": row(ssm_d), "ssm_b_glu": row(ssm_b_glu), "pool_scale": row(pool_scale),
    }
    S = TIME_TILE // SUBLANES
    lamr, lami, pwr, pwi, bbr, bbi, cr_bd, ci_bd, sguw_bf, wglu_bf, poolw_bf = _prep(
        ssm_a_re, ssm_a_im, ssm_log_dt, ssm_b_re, ssm_b_im, ssm_c_re, ssm_c_im, sgu_w, ssm_w_glu, pool_w, S)
    prep = {"lamr": lamr, "lami": lami, "pwr": pwr, "pwi": pwi, "bbr": bbr, "bbi": bbi,
            "sguw": sguw_bf, "wglu": wglu_bf, "poolw": poolw_bf}

    wr = jnp.concatenate([router_group_w,
                          jnp.transpose(router_expert_w, (0, 2, 1, 3)).reshape(depth, D, N_EXPERTS)], axis=-1)
    wr = jnp.pad(wr, ((0, 0), (0, 0), (0, LANES - wr.shape[-1])))
    rb = jnp.concatenate([router_group_b, router_expert_b.reshape(depth, N_EXPERTS)], axis=-1)
    rb = jnp.pad(rb, ((0, 0), (0, LANES - rb.shape[-1]))).reshape(depth, 1, LANES)

    norm_mix3 = row(norm_mix_g)
    norm_ffn3 = row(norm_ffn_g)
    final3 = final_norm_g.reshape(1, D)

    n_tiles_max = -(-2 * T // EXPERT_TILE) + N_EXPERTS
    n_pad = n_tiles_max * EXPERT_TILE

    x = jnp.concatenate([x_prompt.reshape(B * L, D), x_sample.reshape(DB, D)], axis=0)
    h = _rmsnorm(x, norm_mix3, 0, BF16)

    stc2 = state_conv.reshape(depth, DB, (CONV_K - 1) * W)
    stp2 = state_pool.reshape(depth, DB, POOL_HIST * W)
    hre2 = state_ssm_re.reshape(depth, DB, GP)
    him2 = state_ssm_im.reshape(depth, DB, GP)

    conv_p, conv_s, pool_p, pool_s = [], [], [], []
    sre_p, sim_p, sre_s, sim_s, v_s = [], [], [], [], []
    y_p = y_s = None
    for l in range(depth):
        o, cp, pp, rp, ip = _mixer_prompt(h, w_in, B, L, T, l, small, prep, cr_bd, ci_bd)
        o, g_s, vs, d_in_s, xs_re, xs_im = _mixer_sample(h, w_in, o, stc2, stp2, hre2, him2, DB, T, l,
                                                         small, prep, cr_bd, ci_bd)
        merged = _merge(h, o, w_in, w_branch, l, gate_off)
        x1, hp, logits = _out_proj(merged, x, w_out, norm_ffn3, wr, l)
        route, meta = _route(logits, rb, l, EXPERT_TILE, n_tiles_max)
        pos_flat = jnp.transpose(route[:, 4:6]).astype(I32).reshape(2 * T)
        te = meta[2, :n_tiles_max].astype(I32)
        na = meta[3, :1].astype(I32)
        tend = meta[4, :N_EXPERTS].astype(I32)
        hs = _dispatch(pos_flat, tend, hp, n_pad, EXPERT_TILE, D // 2 // LANES)
        ys = _experts(te, na, hs, expert_w_gate, expert_w_up, expert_w_down, l, EXPERT_TILE, n_tiles_max)
        if l + 1 < depth:
            x, h = _combine(pos_flat, x1, route, norm_mix3, l + 1, ys)
        else:
            y_p, y_s = _combine(pos_flat, x1, route, final3, 0, ys, n_sample=DB)

        conv_p.append(cp)
        pool_p.append(pp)
        sre_p.append(rp.reshape(B, G, P))
        sim_p.append(ip.reshape(B, G, P))
        conv_s.append(jnp.concatenate([state_conv[l][:, 1:], g_s[:, None, :]], axis=1))
        pool_s.append(jnp.concatenate([state_pool[l][:, 1:], d_in_s[:, None, :]], axis=1))
        sre_s.append(xs_re.reshape(DB, G, P))
        sim_s.append(xs_im.reshape(DB, G, P))
        v_s.append(vs[:, None, :])

    y_prompt = y_p.reshape(B, L, D)
    y_sample = y_s.reshape(DB, 1, D)
    return (y_prompt, y_sample, jnp.stack(conv_p), jnp.stack(conv_s), jnp.stack(pool_p), jnp.stack(pool_s),
            jnp.stack(sre_p), jnp.stack(sim_p), jnp.stack(sre_s), jnp.stack(sim_s), jnp.stack(v_s))
```

```python
import functools
import math

import jax
import jax.numpy as jnp
from jax import lax
from jax.experimental import pallas as pl
from jax.experimental.pallas import tpu as pltpu

F32 = jnp.float32
BF16 = jnp.bfloat16
I32 = jnp.int32
U32 = jnp.uint32

EPS = 1e-6
CHUNK = 128
SGU_GROUPS = 4
CONV_K = 31
POOL_WINDOWS = (2, 4, 8, 16)
POOL_HIST = 15
SSM_H = 16
SSM_P = 64
N_GROUPS = 4
EXP_PER_GROUP = 4
N_EXPERTS = 16
PAST_LEN = 16384

LANES = 128
SUBLANES = 8
VMEM_LIMIT = 60 * 1024 * 1024

EXPERT_TILE = 256
TIME_TILE = 256
SCAN_COLS = 512


def _cparams(n_axes):
    return pltpu.CompilerParams(dimension_semantics=("arbitrary",) * n_axes,
                                vmem_limit_bytes=VMEM_LIMIT)


def _gelu(x):
    c = math.sqrt(2.0 / math.pi)
    return 0.5 * x * (1.0 + jnp.tanh(c * (x + 0.044715 * (x * x * x))))


def _sigmoid(x):
    return 0.5 * jnp.tanh(0.5 * x) + 0.5


def _layernorm(x, g, b):
    xc = x - jnp.mean(x, axis=-1, keepdims=True)
    var = jnp.mean(xc * xc, axis=-1, keepdims=True)
    return xc * lax.rsqrt(var + EPS) * g + b


def _rms_scale(x, g):
    return x * lax.rsqrt(jnp.mean(x * x, axis=-1, keepdims=True) + EPS) * g


def _dot(a, b):
    return jnp.dot(a, b, preferred_element_type=F32)


def _cast_rows(src_ref, dst_ref, chunk):
    rows = src_ref.shape[0]

    def body(i, c):
        r = pl.multiple_of(i * chunk, chunk)
        dst_ref[pl.ds(r, chunk), :] = src_ref[pl.ds(r, chunk), :].astype(dst_ref.dtype)
        return c

    lax.fori_loop(0, rows // chunk, body, 0)


def _layer_spec(arr, l, single=False):
    nd = arr.ndim
    mode = {"pipeline_mode": pl.Buffered(1)} if single else {}
    return pl.BlockSpec((None,) + tuple(arr.shape[1:]), lambda *_: (l,) + (0,) * (nd - 1), **mode)


def _pick(n, cands):
    for c in cands:
        if n % c == 0:
            return c
    raise ValueError(f"no tile for {n}")


def _rmsnorm_kernel(x_ref, g_ref, o_ref):
    o_ref[...] = _rms_scale(x_ref[...], g_ref[...]).astype(o_ref.dtype)


def _rmsnorm(x, g_all, l, out_dtype):
    T, D = x.shape
    tm = _pick(T, (640, 512, 384, 256, 128))
    return pl.pallas_call(
        _rmsnorm_kernel,
        out_shape=jax.ShapeDtypeStruct((T, D), out_dtype),
        grid=(T // tm,),
        in_specs=[pl.BlockSpec((tm, D), lambda m: (m, 0)), _layer_spec(g_all, l)],
        out_specs=pl.BlockSpec((tm, D), lambda m: (m, 0)),
        compiler_params=_cparams(1),
        name="rmsnorm",
    )(x, g_all)


def _split3(x):
    p1 = x.astype(BF16)
    r1 = x - p1.astype(F32)
    p2 = r1.astype(BF16)
    p3 = (r1 - p2.astype(F32)).astype(BF16)
    return p1, p2, p3


def _prep_kernel(are, aim, ldt, bre, bim, cre, cim, sguw, wglu, poolw,
                 lamr_o, lami_o, pwr_o, pwi_o, bbr_o, bbi_o, cr_o, ci_o, sguw_o, wglu_o, poolw_o, *, S, P, H):
    a_re = are[...]
    a_im = aim[...]
    dt = jnp.exp(ldt[...])
    mag = jnp.exp(a_re * dt)
    lbr = mag * jnp.cos(a_im * dt)
    lbi = mag * jnp.sin(a_im * dt)
    den = a_re * a_re + a_im * a_im
    nr = lbr - 1.0
    kr = (nr * a_re + lbi * a_im) / den
    ki = (lbi * a_re - nr * a_im) / den
    lamr_o[...] = lbr
    lami_o[...] = lbi
    gp = a_re.shape[-1]

    w = bre.shape[0]
    sh_p, sh_h = P.bit_length() - 1, H.bit_length() - 1
    rc = 128
    pe = lax.broadcasted_iota(I32, (P, gp), 0)
    ce = lax.broadcasted_iota(I32, (P, gp), 1)
    rep_p = jnp.where((ce & (P - 1)) == pe, 1.0, 0.0).astype(BF16)
    for i in range(w // rc):
        rows = slice(i * rc, (i + 1) * rc)
        r_i = lax.broadcasted_iota(I32, (rc, gp), 0) + i * rc
        c_i = lax.broadcasted_iota(I32, (rc, gp), 1)
        diag = (r_i >> sh_h) == (c_i >> sh_p)
        br = sum(_dot(piece, rep_p) for piece in _split3(bre[rows, :]))
        bi = sum(_dot(piece, rep_p) for piece in _split3(bim[rows, :]))
        bbr_o[rows, :] = jnp.where(diag, kr * br - ki * bi, 0.0).astype(BF16)
        bbi_o[rows, :] = jnp.where(diag, kr * bi + ki * br, 0.0).astype(BF16)

    he = lax.broadcasted_iota(I32, (H, w), 0)
    ce = lax.broadcasted_iota(I32, (H, w), 1)
    rep_h = jnp.where((ce & (H - 1)) == he, 1.0, 0.0).astype(BF16)
    rc = 256
    for i in range(gp // rc):
        rows = slice(i * rc, (i + 1) * rc)
        r_i = lax.broadcasted_iota(I32, (rc, w), 0) + i * rc
        c_i = lax.broadcasted_iota(I32, (rc, w), 1)
        diag = (r_i >> sh_p) == (c_i >> sh_h)
        cr_o[rows, :] = jnp.where(diag, _dot(cre[rows, :].astype(BF16), rep_h), 0.0).astype(BF16)
        ci_o[rows, :] = jnp.where(diag, _dot(cim[rows, :].astype(BF16), rep_h), 0.0).astype(BF16)

    pr, pi = lbr, lbi
    for s in range(S):
        pwr_o[SUBLANES * s:SUBLANES * (s + 1), :] = jnp.broadcast_to(pr, (SUBLANES, gp))
        pwi_o[SUBLANES * s:SUBLANES * (s + 1), :] = jnp.broadcast_to(pi, (SUBLANES, gp))
        pr, pi = pr * lbr - pi * lbi, pr * lbi + pi * lbr

    t_i = lax.broadcasted_iota(I32, (CHUNK, CHUNK), 0)
    s_i = lax.broadcasted_iota(I32, (CHUNK, CHUNK), 1)
    for g in range(SGU_GROUPS):
        sguw_o[g] = jnp.where(t_i >= s_i, sguw[g], 0.0).astype(BF16)
    wglu_o[...] = wglu[...].astype(BF16)
    for g in range(len(POOL_WINDOWS)):
        poolw_o[g] = poolw[g].astype(BF16)


def _prep(a_re, a_im, log_dt, b_re, b_im, c_re, c_im, sgu_w, w_glu, pool_w, S):
    L, G, P = a_re.shape
    H = b_re.shape[-1]
    GP, W = G * P, G * H
    assert P & (P - 1) == 0 and H & (H - 1) == 0
    bre_t = jnp.swapaxes(b_re, 2, 3).reshape(L, W, P)
    bim_t = jnp.swapaxes(b_im, 2, 3).reshape(L, W, P)
    cre_t = jnp.swapaxes(c_re, 2, 3).reshape(L, GP, H)
    cim_t = jnp.swapaxes(c_im, 2, 3).reshape(L, GP, H)
    are2 = a_re.reshape(L, 1, GP)
    aim2 = a_im.reshape(L, 1, GP)
    ldt2 = jnp.repeat(log_dt, P, axis=-1).reshape(L, 1, GP)

    def lspec(shape):
        nd = len(shape)
        return pl.BlockSpec((None,) + tuple(shape[1:]), lambda l: (l,) + (0,) * (nd - 1))

    ins = [are2, aim2, ldt2, bre_t, bim_t, cre_t, cim_t, sgu_w, w_glu, pool_w]
    out_shapes = [
        jax.ShapeDtypeStruct((L, 1, GP), F32), jax.ShapeDtypeStruct((L, 1, GP), F32),
        jax.ShapeDtypeStruct((L, SUBLANES * S, GP), F32), jax.ShapeDtypeStruct((L, SUBLANES * S, GP), F32),
        jax.ShapeDtypeStruct((L, W, GP), BF16), jax.ShapeDtypeStruct((L, W, GP), BF16),
        jax.ShapeDtypeStruct((L, GP, W), BF16), jax.ShapeDtypeStruct((L, GP, W), BF16),
        jax.ShapeDtypeStruct(sgu_w.shape, BF16), jax.ShapeDtypeStruct(w_glu.shape, BF16),
        jax.ShapeDtypeStruct(pool_w.shape, BF16),
    ]
    return pl.pallas_call(
        functools.partial(_prep_kernel, S=S, P=P, H=H),
        out_shape=out_shapes,
        grid=(L,),
        in_specs=[lspec(a.shape) for a in ins],
        out_specs=[lspec(o.shape) for o in out_shapes],
        compiler_params=_cparams(1),
        name="ssm_prep",
    )(*ins)


PROJ_CHUNK = 256


def _stream_weight_cols(w_hbm, l, n_cols, stage, sem, consume):
    def copy(c, slot):
        return pltpu.make_async_copy(w_hbm.at[l, :, pl.ds(c * PROJ_CHUNK, PROJ_CHUNK)], stage.at[slot],
                                     sem.at[slot])

    n = n_cols // PROJ_CHUNK
    copy(0, 0).start()
    for c in range(n):
        slot = c % 2
        if c + 1 < n:
            copy(c + 1, 1 - slot).start()
        copy(c, slot).wait()
        consume(c, slot)


N_MIXER_INPUTS = 23


def _mixer_prompt_kernel(*refs, n_batch, **kw):
    o_ref = refs[N_MIXER_INPUTS]
    b = pl.program_id(0)

    @pl.when(b < n_batch)
    def _():
        _mixer_prompt_body(*refs, **kw)

    @pl.when((b == n_batch) & (pl.program_id(1) == 0))
    def _():
        o_ref[...] = jnp.zeros_like(o_ref)


def _mixer_prompt_body(h_ref, w_hbm, lng, lnb, sguw, sgub, cw, cb, clg, clb,
                       lamr, lami, pwr, pwi, bbr, bbi, cr, ci, sd, wglu, bglu, poolw, pscale,
                       o_ref, conv_o, pool_o, sre_o, sim_o,
                       w6bf, wstage, wsem, p_ref,
                       gbuf, gsh, pbuf, plvl, xr, xi, car_re, car_im, fin_re, fin_im, cm_re, cm_im,
                       *, l, tt, S, W, GP):
    j = pl.program_id(1)
    nt = pl.num_programs(1)
    GH = 32
    PH = 32

    @pl.when((pl.program_id(0) == 0) & (j == 0))
    def _():
        def consume(c, slot):
            _cast_rows(wstage.at[slot], w6bf.at[:, c * PROJ_CHUNK:(c + 1) * PROJ_CHUNK], 256)

        _stream_weight_cols(w_hbm, l, 6 * W, wstage, wsem, consume)

    @pl.when(j == 0)
    def _():
        gbuf[0:GH, :] = jnp.zeros((GH, W), F32)
        pbuf[0:PH, :] = jnp.zeros((PH, W), F32)
        car_re[...] = jnp.zeros_like(car_re)
        car_im[...] = jnp.zeros_like(car_im)

    h = h_ref[...]
    for c in range(3):
        p_ref[:, 2 * c * W:2 * (c + 1) * W] = _dot(h, w6bf[:, 2 * c * W:2 * (c + 1) * W])

    for c in range(tt // CHUNK):
        rows = slice(c * CHUNK, (c + 1) * CHUNK)
        u = _gelu(p_ref[rows, 0:W])
        v = _layernorm(_gelu(p_ref[rows, W:2 * W]), lng[...], lnb[...])
        vb = v.astype(BF16)
        gw = W // SGU_GROUPS
        for g in range(SGU_GROUPS):
            cols = slice(g * gw, (g + 1) * gw)
            mixed = _dot(sguw[g], vb[:, cols]) + sgub[:, g:g + 1]
            o_ref[rows, cols] = (u[:, cols] * mixed).astype(BF16)

    rc = 64
    for c in range(tt // rc):
        rows = slice(c * rc, (c + 1) * rc)
        gbuf[GH + c * rc:GH + (c + 1) * rc, :] = p_ref[rows, 2 * W:3 * W] * _sigmoid(p_ref[rows, 3 * W:4 * W])
    sh_rows = gsh.shape[1]
    for b in range(1, SUBLANES):
        gsh[b] = gbuf[b:b + sh_rows, :]
    rc = 32
    base = GH - (CONV_K - 1)
    for c in range(tt // rc):
        acc = jnp.zeros((rc, W), F32)
        for k in range(CONV_K):
            r0 = base + c * rc + k
            b, a0 = r0 % SUBLANES, r0 - r0 % SUBLANES
            rows = gbuf[a0:a0 + rc, :] if b == 0 else gsh[b, a0:a0 + rc, :]
            wk = jnp.concatenate([cw[SUBLANES * k:SUBLANES * (k + 1), :]] * (rc // SUBLANES), axis=0)
            acc = acc + wk * rows
        y = _layernorm(acc + cb[...], clg[...], clb[...])
        o_ref[c * rc:(c + 1) * rc, W:2 * W] = (y * _sigmoid(y)).astype(BF16)
    gbuf[0:GH, :] = gbuf[tt:tt + GH, :]

    uc = p_ref[:, 4 * W:5 * W]
    r_i = lax.broadcasted_iota(I32, (tt, tt), 0)
    c_i = lax.broadcasted_iota(I32, (tt, tt), 1)
    perm = jnp.where(((r_i & (SUBLANES - 1)) * S + (r_i >> 3)) == c_i, 1.0, 0.0).astype(BF16)
    up = _dot(perm, uc.astype(BF16)).astype(BF16)
    hw, hg = W // 2, GP // 2
    for q in range(2):
        xr[:, q * hg:(q + 1) * hg] = _dot(up[:, q * hw:(q + 1) * hw], bbr[q * hw:(q + 1) * hw, q * hg:(q + 1) * hg])
        xi[:, q * hg:(q + 1) * hg] = _dot(up[:, q * hw:(q + 1) * hw], bbi[q * hw:(q + 1) * hw, q * hg:(q + 1) * hg])

    for cbi in range(GP // SCAN_COLS):
        cols = slice(cbi * SCAN_COLS, (cbi + 1) * SCAN_COLS)
        lr = jnp.broadcast_to(lamr[:, cols], (SUBLANES, SCAN_COLS))
        li = jnp.broadcast_to(lami[:, cols], (SUBLANES, SCAN_COLS))

        def step(s, carry, cols=cols, lr=lr, li=li):
            sr, si = carry
            r0 = pl.multiple_of(s * SUBLANES, SUBLANES)
            nr = lr * sr - li * si + xr[pl.ds(r0, SUBLANES), cols]
            ni = lr * si + li * sr + xi[pl.ds(r0, SUBLANES), cols]
            xr[pl.ds(r0, SUBLANES), cols] = nr
            xi[pl.ds(r0, SUBLANES), cols] = ni
            return nr, ni

        z = jnp.zeros((SUBLANES, SCAN_COLS), F32)
        fr, fi = lax.fori_loop(0, S, step, (z, z), unroll=True)
        fin_re[:, cols] = fr
        fin_im[:, cols] = fi

    lsr = pwr[SUBLANES * (S - 1):SUBLANES * (S - 1) + 1, :]
    lsi = pwi[SUBLANES * (S - 1):SUBLANES * (S - 1) + 1, :]
    c_r = car_re[0:1, :]
    c_im = car_im[0:1, :]
    cm_re[0:1, :] = c_r
    cm_im[0:1, :] = c_im
    for q in range(1, SUBLANES):
        f_r = fin_re[q - 1:q, :]
        f_i = fin_im[q - 1:q, :]
        c_r, c_im = f_r + lsr * c_r - lsi * c_im, f_i + lsr * c_im + lsi * c_r
        cm_re[q:q + 1, :] = c_r
        cm_im[q:q + 1, :] = c_im
    n_r = fin_re[SUBLANES - 1:SUBLANES, :] + lsr * c_r - lsi * c_im
    n_i = fin_im[SUBLANES - 1:SUBLANES, :] + lsr * c_im + lsi * c_r
    car_re[0:1, :] = n_r
    car_im[0:1, :] = n_i

    for cbi in range(GP // SCAN_COLS):
        cols = slice(cbi * SCAN_COLS, (cbi + 1) * SCAN_COLS)
        mr = cm_re[:, cols]
        mi = cm_im[:, cols]

        def fix(s, c, cols=cols, mr=mr, mi=mi):
            r0 = pl.multiple_of(s * SUBLANES, SUBLANES)
            pr = pwr[pl.ds(r0, SUBLANES), cols]
            pi = pwi[pl.ds(r0, SUBLANES), cols]
            xr[pl.ds(r0, SUBLANES), cols] = xr[pl.ds(r0, SUBLANES), cols] + (pr * mr - pi * mi)
            xi[pl.ds(r0, SUBLANES), cols] = xi[pl.ds(r0, SUBLANES), cols] + (pr * mi + pi * mr)
            return c

        lax.fori_loop(0, S, fix, 0, unroll=True)

    yp = jnp.concatenate(
        [_dot(xr[:, q * hg:(q + 1) * hg].astype(BF16), cr[q * hg:(q + 1) * hg, q * hw:(q + 1) * hw])
         - _dot(xi[:, q * hg:(q + 1) * hg].astype(BF16), ci[q * hg:(q + 1) * hg, q * hw:(q + 1) * hw])
         for q in range(2)], axis=1)
    unperm = jnp.where(((c_i & (SUBLANES - 1)) * S + (c_i >> 3)) == r_i, 1.0, 0.0).astype(BF16)
    y1 = yp.astype(BF16)
    r1 = yp - y1.astype(F32)
    y2 = r1.astype(BF16)
    y3 = (r1 - y2.astype(F32)).astype(BF16)
    y = _dot(unperm, y1) + _dot(unperm, y2) + _dot(unperm, y3)
    z = _gelu(y + sd[...] * uc)
    gl = _dot(z.astype(BF16), wglu[...]) + bglu[...]
    o_ref[:, 2 * W:3 * W] = (z * _sigmoid(gl)).astype(BF16)

    xd = p_ref[:, 5 * W:6 * W]
    pbuf[PH:PH + tt, :] = xd
    gw = W // len(POOL_WINDOWS)
    pos = (j * tt + lax.broadcasted_iota(I32, (tt, 1), 0) + 1).astype(F32)
    li = 0
    for gi, win in enumerate(POOL_WINDOWS):
        cols = slice(gi * gw, (gi + 1) * gw)
        nlev = win.bit_length() - 1
        load = lambda r0, n, cols=cols: pbuf[r0:r0 + n, cols]
        for k in range(1, nlev):
            lo, sh = SUBLANES * k, 1 << (k - 1)
            n = PH + tt - lo
            plvl[li, lo:lo + n, :] = load(lo, n) + load(lo - sh, n)
            load = lambda r0, n, li=li: plvl[li, r0:r0 + n, :]
            li += 1
        s = load(PH, tt) + load(PH - (win >> 1), tt)
        cnt = jnp.minimum(pos, float(win))
        pooled = s / cnt - xd[:, cols]
        mixed = _dot(pooled.astype(BF16), poolw[gi]) * pscale[:, cols]
        o_ref[:, 3 * W + gi * gw:3 * W + (gi + 1) * gw] = mixed.astype(BF16)
    pbuf[0:PH, :] = pbuf[tt:tt + PH, :]

    @pl.when(j == nt - 1)
    def _():
        conv_o[...] = gbuf[GH - (CONV_K - 1):GH, :]
        pool_o[...] = pbuf[PH - POOL_HIST:PH, :]
        sre_o[...] = car_re[0:1, :]
        sim_o[...] = car_im[0:1, :]


def _mixer_prompt(h, w_in, B, L, T, l, small, prep, cr, ci):
    W = small["sgu_ln_g"].shape[-1]
    GP = prep["lamr"].shape[-1]
    D = h.shape[1]
    D4 = 4 * W
    tt = TIME_TILE
    S = tt // SUBLANES
    nt = L // tt
    ins = [h, w_in,
           small["sgu_ln_g"], small["sgu_ln_b"], prep["sguw"], small["sgu_bT"],
           small["conv_w8"], small["conv_b"], small["conv_ln_g"], small["conv_ln_b"],
           prep["lamr"], prep["lami"], prep["pwr"], prep["pwi"], prep["bbr"], prep["bbi"], cr, ci,
           small["ssm_d"], prep["wglu"], small["ssm_b_glu"], prep["poolw"], small["pool_scale"]]
    assert len(ins) == N_MIXER_INPUTS
    in_specs = [pl.BlockSpec((tt, D), lambda b, j: (jnp.minimum(b * nt + j, B * nt), 0)),
                pl.BlockSpec(memory_space=pl.ANY)]
    in_specs += [_layer_spec(a, l, single=True) for a in ins[2:]]
    out_shapes = [
        jax.ShapeDtypeStruct((T, D4), BF16),
        jax.ShapeDtypeStruct((B, CONV_K - 1, W), F32),
        jax.ShapeDtypeStruct((B, POOL_HIST, W), F32),
        jax.ShapeDtypeStruct((B, 1, GP), F32),
        jax.ShapeDtypeStruct((B, 1, GP), F32),
    ]
    out_specs = [
        pl.BlockSpec((tt, D4), lambda b, j: (jnp.minimum(b * nt + j, B * nt), 0)),
        pl.BlockSpec((None, CONV_K - 1, W), lambda b, j: (jnp.minimum(b, B - 1), 0, 0)),
        pl.BlockSpec((None, POOL_HIST, W), lambda b, j: (jnp.minimum(b, B - 1), 0, 0)),
        pl.BlockSpec((None, 1, GP), lambda b, j: (jnp.minimum(b, B - 1), 0, 0)),
        pl.BlockSpec((None, 1, GP), lambda b, j: (jnp.minimum(b, B - 1), 0, 0)),
    ]
    scratch = [
        pltpu.VMEM((D, 6 * W), BF16), pltpu.VMEM((2, D, PROJ_CHUNK), F32), pltpu.SemaphoreType.DMA((2,)),
        pltpu.VMEM((tt, 6 * W), F32),
        pltpu.VMEM((32 + tt, W), F32), pltpu.VMEM((SUBLANES, 32 + tt - SUBLANES, W), F32),
        pltpu.VMEM((32 + tt, W), F32), pltpu.VMEM((6, 32 + tt, W // len(POOL_WINDOWS)), F32),
        pltpu.VMEM((tt, GP), F32), pltpu.VMEM((tt, GP), F32),
        pltpu.VMEM((SUBLANES, GP), F32), pltpu.VMEM((SUBLANES, GP), F32),
        pltpu.VMEM((SUBLANES, GP), F32), pltpu.VMEM((SUBLANES, GP), F32),
        pltpu.VMEM((SUBLANES, GP), F32), pltpu.VMEM((SUBLANES, GP), F32),
    ]
    return pl.pallas_call(
        functools.partial(_mixer_prompt_kernel, n_batch=B, l=l, tt=tt, S=S, W=W, GP=GP),
        out_shape=out_shapes,
        grid=(B + 1, nt),
        in_specs=in_specs,
        out_specs=out_specs,
        scratch_shapes=scratch,
        compiler_params=_cparams(2),
        name="mixer_prompt",
    )(*ins)


def _mixer_sample_kernel(h_ref, w_hbm, stc, stp, hre, him, lng, lnb, w00, b0, cw, cb, clg, clb,
                         lamr, lami, bbr, bbi, cr, ci, sd, wglu, bglu, poolw, pscale, o_in,
                         o_ref, g_o, v_o, d_o, xre_o, xim_o,
                         p_ref, wstage, wchunk, wsem, *, l, W):
    del o_in
    h = h_ref[...]

    def consume(c, slot):
        _cast_rows(wstage.at[slot], wchunk, 256)
        p_ref[:, c * PROJ_CHUNK:(c + 1) * PROJ_CHUNK] = _dot(h, wchunk[...])

    _stream_weight_cols(w_hbm, l, 6 * W, wstage, wsem, consume)

    u = _gelu(p_ref[:, 0:W])
    v = _layernorm(_gelu(p_ref[:, W:2 * W]), lng[...], lnb[...])
    v_o[...] = v
    o_ref[:, 0:W] = (u * (w00[...] * v + b0[...])).astype(BF16)

    g = p_ref[:, 2 * W:3 * W] * _sigmoid(p_ref[:, 3 * W:4 * W])
    g_o[...] = g
    acc = cw[CONV_K - 1:CONV_K, :] * g
    for k in range(CONV_K - 1):
        acc = acc + cw[k:k + 1, :] * stc[:, k * W:(k + 1) * W]
    y = _layernorm(acc + cb[...], clg[...], clb[...])
    o_ref[:, W:2 * W] = (y * _sigmoid(y)).astype(BF16)

    uc = p_ref[:, 4 * W:5 * W]
    ub = uc.astype(BF16)
    h_r = hre[...]
    h_i = him[...]
    l_r = lamr[...]
    l_i = lami[...]
    x_r = l_r * h_r - l_i * h_i + _dot(ub, bbr[...])
    x_i = l_r * h_i + l_i * h_r + _dot(ub, bbi[...])
    xre_o[...] = x_r
    xim_o[...] = x_i
    yv = _dot(x_r.astype(BF16), cr[...]) - _dot(x_i.astype(BF16), ci[...]) + sd[...] * uc
    z = _gelu(yv)
    gl = _dot(z.astype(BF16), wglu[...]) + bglu[...]
    o_ref[:, 2 * W:3 * W] = (z * _sigmoid(gl)).astype(BF16)

    xd = p_ref[:, 5 * W:6 * W]
    d_o[...] = xd
    gw = W // len(POOL_WINDOWS)
    for gi, win in enumerate(POOL_WINDOWS):
        cols = slice(gi * gw, (gi + 1) * gw)
        s = xd[:, cols]
        for i in range(1, win):
            r = POOL_HIST - i
            s = s + stp[:, r * W + gi * gw:r * W + (gi + 1) * gw]
        cnt = float(min(PAST_LEN + 1, win))
        pooled = s / cnt - xd[:, cols]
        mixed = _dot(pooled.astype(BF16), poolw[gi]) * pscale[:, cols]
        o_ref[:, 3 * W + gi * gw:3 * W + (gi + 1) * gw] = mixed.astype(BF16)


def _mixer_sample(h, w_in, o_prev, stc2, stp2, hre2, him2, DB, T, l, small, prep, cr, ci):
    W = small["sgu_ln_g"].shape[-1]
    GP = prep["lamr"].shape[-1]
    D = h.shape[1]
    D4 = 4 * W
    blk = (T - DB) // DB
    ins = [h, w_in, stc2, stp2, hre2, him2,
           small["sgu_ln_g"], small["sgu_ln_b"], small["sgu_w00"], small["sgu_b0"],
           small["conv_w"], small["conv_b"], small["conv_ln_g"], small["conv_ln_b"],
           prep["lamr"], prep["lami"], prep["bbr"], prep["bbi"], cr, ci,
           small["ssm_d"], prep["wglu"], small["ssm_b_glu"], prep["poolw"], small["pool_scale"], o_prev]
    in_specs = [pl.BlockSpec((DB, D), lambda i: (blk, 0)), pl.BlockSpec(memory_space=pl.ANY)]
    in_specs += [_layer_spec(a, l, single=True) for a in ins[2:-1]]
    in_specs += [pl.BlockSpec(memory_space=pl.ANY)]
    out_shapes = [
        jax.ShapeDtypeStruct((T, D4), BF16),
        jax.ShapeDtypeStruct((DB, W), F32), jax.ShapeDtypeStruct((DB, W), F32), jax.ShapeDtypeStruct((DB, W), F32),
        jax.ShapeDtypeStruct((DB, GP), F32), jax.ShapeDtypeStruct((DB, GP), F32),
    ]
    out_specs = [
        pl.BlockSpec((DB, D4), lambda i: (blk, 0)),
        pl.BlockSpec((DB, W), lambda i: (0, 0)), pl.BlockSpec((DB, W), lambda i: (0, 0)),
        pl.BlockSpec((DB, W), lambda i: (0, 0)),
        pl.BlockSpec((DB, GP), lambda i: (0, 0)), pl.BlockSpec((DB, GP), lambda i: (0, 0)),
    ]
    return pl.pallas_call(
        functools.partial(_mixer_sample_kernel, l=l, W=W),
        out_shape=out_shapes,
        grid=(1,),
        in_specs=in_specs,
        out_specs=out_specs,
        scratch_shapes=[pltpu.VMEM((DB, 6 * W), F32), pltpu.VMEM((2, D, PROJ_CHUNK), F32),
                        pltpu.VMEM((D, PROJ_CHUNK), BF16), pltpu.SemaphoreType.DMA((2,))],
        input_output_aliases={len(ins) - 1: 0},
        compiler_params=_cparams(1),
        name="mixer_sample",
    )(*ins)


def _merge_kernel(h_ref, o_ref, wg0, wg1, wg2, wg3, wb_ref, m_ref, wgbf, wbbf, *, W):
    wgs = (wg0, wg1, wg2, wg3)

    @pl.when(pl.program_id(1) == 0)
    def _():
        for b in range(4):
            _cast_rows(wgs[b], wgbf.at[b], 256)
            wbbf[b] = wb_ref[b].astype(BF16)

    h = h_ref[...]
    acc = None
    for b in range(4):
        gate = _sigmoid(_dot(h, wgbf[b]))
        term = gate * _dot(o_ref[:, b * W:(b + 1) * W], wbbf[b])
        acc = term if acc is None else acc + term
    m_ref[...] = acc.astype(BF16)


def _merge(h, o, w_in, w_branch, l, gate_off):
    T, D = h.shape
    W = w_branch.shape[2]
    tm = _pick(T, (640, 512, 384, 256, 128))
    tn = 256
    nb = D // tn

    def gate_spec(b):
        blk0 = (gate_off + b * D) // tn
        return pl.BlockSpec((None, D, tn), lambda n, m: (l, 0, blk0 + n))

    return pl.pallas_call(
        functools.partial(_merge_kernel, W=W),
        out_shape=jax.ShapeDtypeStruct((T, D), BF16),
        grid=(nb, T // tm),
        in_specs=[pl.BlockSpec((tm, D), lambda n, m: (m, 0)),
                  pl.BlockSpec((tm, 4 * W), lambda n, m: (m, 0)),
                  gate_spec(0), gate_spec(1), gate_spec(2), gate_spec(3),
                  pl.BlockSpec((None, 4, W, tn), lambda n, m: (l, 0, 0, n))],
        out_specs=pl.BlockSpec((tm, tn), lambda n, m: (m, n)),
        scratch_shapes=[pltpu.VMEM((4, D, tn), BF16), pltpu.VMEM((4, W, tn), BF16)],
        compiler_params=_cparams(2),
        name="gated_merge",
    )(h, o, w_in, w_in, w_in, w_in, w_branch)


def _out_kernel(m_ref, x_ref, w_ref, g_ref, wr_ref, x1_ref, hp_ref, lg_ref, wbf, wr2):
    R = wr_ref.shape[-1]

    @pl.when(pl.program_id(0) == 0)
    def _():
        _cast_rows(w_ref, wbf, 256)
        wr = wr_ref[...]
        hi = wr.astype(BF16)
        wr2[:, 0:R] = hi
        wr2[:, R:2 * R] = (wr - hi.astype(F32)).astype(BF16)

    x1 = x_ref[...] + _dot(m_ref[...], wbf[...])
    x1_ref[...] = x1
    h = _rms_scale(x1, g_ref[...])
    hb = h.astype(BF16)
    hl = (h - hb.astype(F32)).astype(BF16)
    both = _dot(hb, wr2[...])
    lg_ref[...] = both[:, 0:R] + both[:, R:2 * R] + _dot(hl, wr2[:, 0:R])
    bits = lax.bitcast_convert_type(hb.astype(F32), U32)
    half = bits.shape[1] // 2
    word = (bits[:, :half] >> 16) | (bits[:, half:] & jnp.uint32(0xFFFF0000))
    tm = word.shape[0]
    for j in range(half // LANES):
        hp_ref[pl.ds(j, tm, stride=half // LANES), :] = word[:, j * LANES:(j + 1) * LANES]


def _out_proj(merged, x, w_out, g_all, wr_all, l):
    T, D = x.shape
    tm = _pick(T, (416, 320, 256, 128))
    R = wr_all.shape[-1]
    return pl.pallas_call(
        _out_kernel,
        out_shape=[jax.ShapeDtypeStruct((T, D), F32), jax.ShapeDtypeStruct((T * (D // 2 // LANES), LANES), U32),
                   jax.ShapeDtypeStruct((T, R), F32)],
        grid=(T // tm,),
        in_specs=[pl.BlockSpec((tm, D), lambda m: (m, 0)),
                  pl.BlockSpec((tm, D), lambda m: (m, 0)),
                  pl.BlockSpec((None, D, D), lambda m: (l, 0, 0), pipeline_mode=pl.Buffered(1)),
                  _layer_spec(g_all, l), _layer_spec(wr_all, l)],
        out_specs=[pl.BlockSpec((tm, D), lambda m: (m, 0)),
                   pl.BlockSpec((tm * (D // 2 // LANES), LANES), lambda m: (m, 0)),
                   pl.BlockSpec((tm, R), lambda m: (m, 0))],
        scratch_shapes=[pltpu.VMEM((D, D), BF16), pltpu.VMEM((D, 2 * R), BF16)],
        compiler_params=_cparams(1),
        name="out_proj",
    )(merged, x, w_out, g_all, wr_all)


def _route_kernel(lg_ref, bias_ref, rt_ref, meta_ref, cnt, off, *, tm, tile):
    p = pl.program_id(0)
    m = pl.program_id(1)
    lane = lax.broadcasted_iota(I32, (tm, LANES), 1).astype(F32)
    neg = jnp.float32(-jnp.inf)
    big = jnp.float32(1e9)

    @pl.when((p == 0) & (m == 0))
    def _():
        cnt[...] = jnp.zeros_like(cnt)

    lg = lg_ref[...] + bias_ref[...]
    is_g = lane < N_GROUPS
    gl = jnp.where(is_g, lg, neg)
    gmax = jnp.max(gl, axis=-1, keepdims=True)
    gidx = jnp.min(jnp.where(gl == gmax, lane, big), axis=-1, keepdims=True)
    gsum = jnp.sum(jnp.where(is_g, jnp.exp(gl - gmax), 0.0), axis=-1, keepdims=True)
    g_w = 1.0 / gsum
    lo = N_GROUPS + EXP_PER_GROUP * gidx
    in_grp = (lane >= lo) & (lane < lo + EXP_PER_GROUP)
    el = jnp.where(in_grp, lg, neg)
    v1 = jnp.max(el, axis=-1, keepdims=True)
    i1 = jnp.min(jnp.where(el == v1, lane, big), axis=-1, keepdims=True)
    el2 = jnp.where(lane == i1, neg, el)
    v2 = jnp.max(el2, axis=-1, keepdims=True)
    i2 = jnp.min(jnp.where(el2 == v2, lane, big), axis=-1, keepdims=True)
    e2x = jnp.exp(v2 - v1)
    w1 = g_w / (1.0 + e2x)
    w2 = g_w * e2x / (1.0 + e2x)
    e1 = i1 - N_GROUPS
    e2 = i2 - N_GROUPS
    a1 = jnp.where(lane == e1, 1.0, 0.0)
    a2 = jnp.where(lane == e2, 1.0, 0.0)
    a = a1 + a2

    @pl.when(p == 0)
    def _():
        cnt[0:1, :] = cnt[0:1, :] + jnp.sum(a, axis=0, keepdims=True)

    @pl.when((p == 1) & (m == 0))
    def _():
        counts = cnt[0:1, :]
        tiles = jnp.floor((counts + (tile - 1)) * (1.0 / tile))
        tiles8 = jnp.broadcast_to(tiles, (SUBLANES, LANES)).astype(BF16)
        ri = lax.broadcasted_iota(I32, (LANES, LANES), 0)
        ci = lax.broadcasted_iota(I32, (LANES, LANES), 1)
        upper = jnp.where(ri < ci, 1.0, 0.0).astype(BF16)
        toff = _dot(tiles8, upper)[0:1, :]
        off[0:1, :] = toff * tile
        tend = toff + tiles
        lane1 = lax.broadcasted_iota(I32, (1, LANES), 1).astype(F32)
        n_act = jnp.sum(jnp.where(lane1 == N_EXPERTS - 1, tend, 0.0), axis=-1, keepdims=True)
        texp = jnp.zeros((1, LANES), F32)
        for e in range(N_EXPERTS - 1):
            end_e = jnp.sum(jnp.where(lane1 == e, tend, 0.0), axis=-1, keepdims=True)
            texp = texp + jnp.where(jnp.minimum(lane1, n_act - 1.0) >= end_e, 1.0, 0.0)
        meta_ref[0:1, :] = counts
        meta_ref[1:2, :] = off[0:1, :]
        meta_ref[2:3, :] = texp
        meta_ref[3:4, :] = jnp.broadcast_to(n_act, (1, LANES))
        meta_ref[4:5, :] = tend
        meta_ref[5:8, :] = jnp.zeros((3, LANES), F32)
        cnt[...] = jnp.zeros_like(cnt)

    @pl.when(p == 1)
    def _():
        ri = lax.broadcasted_iota(I32, (tm, tm), 0)
        ci = lax.broadcasted_iota(I32, (tm, tm), 1)
        ltri = jnp.where(ci < ri, 1.0, 0.0).astype(BF16)
        cum = _dot(ltri, a.astype(BF16)) + cnt[0:1, :] + off[0:1, :]
        pos1 = jnp.sum(a1 * cum, axis=-1, keepdims=True)
        pos2 = jnp.sum(a2 * cum, axis=-1, keepdims=True)
        cnt[0:1, :] = cnt[0:1, :] + jnp.sum(a, axis=0, keepdims=True)
        rt = jnp.where(lane == 0, e1, 0.0)
        rt = jnp.where(lane == 1, e2, rt)
        rt = jnp.where(lane == 2, w1, rt)
        rt = jnp.where(lane == 3, w2, rt)
        rt = jnp.where(lane == 4, pos1, rt)
        rt = jnp.where(lane == 5, pos2, rt)
        rt_ref[...] = rt


def _route(logits, bias_all, l, tile):
    T, R = logits.shape
    tm = _pick(T, (640, 512, 384, 256, 128))
    return pl.pallas_call(
        functools.partial(_route_kernel, tm=tm, tile=tile),
        out_shape=[jax.ShapeDtypeStruct((T, R), F32), jax.ShapeDtypeStruct((SUBLANES, LANES), F32)],
        grid=(2, T // tm),
        in_specs=[pl.BlockSpec((tm, R), lambda p, m: (m, 0)), _layer_spec(bias_all, l)],
        out_specs=[pl.BlockSpec((tm, R), lambda p, m: (m * p, 0)),
                   pl.BlockSpec((SUBLANES, LANES), lambda p, m: (0, 0))],
        scratch_shapes=[pltpu.VMEM((SUBLANES, LANES), F32), pltpu.VMEM((SUBLANES, LANES), F32)],
        compiler_params=_cparams(2),
        name="route",
    )(logits, bias_all)


def _dispatch_kernel(pos_ref, tend_ref, hp_ref, hs_ref, stage, zbuf, sem, zsem, *, T, tk, tile, n_tiles, rpt):
    i = pl.program_id(0)
    n = pl.num_programs(0)
    trows = tile * rpt

    @pl.when(i == 0)
    def _():
        zbuf[...] = jnp.zeros_like(zbuf)

        def last_tile_copy(ex):
            end = tend_ref[ex]
            start = tend_ref[ex - 1] if ex > 0 else 0
            row0 = pl.multiple_of((end - 1) * trows, trows)
            return end > start, pltpu.make_async_copy(zbuf, hs_ref.at[pl.ds(row0, trows)], zsem)

        def spare_tile_copy(k):
            idx = tend_ref[N_EXPERTS - 1] + k
            row0 = pl.multiple_of(jnp.minimum(idx, n_tiles - 1) * trows, trows)
            return idx < n_tiles, pltpu.make_async_copy(zbuf, hs_ref.at[pl.ds(row0, trows)], zsem)

        fills = [last_tile_copy(ex) for ex in range(N_EXPERTS)] + [spare_tile_copy(k) for k in range(N_EXPERTS)]
        for go, cp in fills:
            @pl.when(go)
            def _(cp=cp):
                cp.start()

        for go, cp in fills:
            @pl.when(go)
            def _(cp=cp):
                cp.wait()

    def copies(step, slot):
        out = []
        for u in range(tk):
            t = step * tk + u
            for k in range(2):
                p = pl.multiple_of(pos_ref[k * T + t] * rpt, rpt)
                out.append(pltpu.make_async_copy(stage.at[slot, pl.ds(u * rpt, rpt)], hs_ref.at[pl.ds(p, rpt)],
                                                 sem.at[slot]))
        return out

    def wait_slot(s):
        rows = hs_ref.at[pl.ds(0, 2 * tk * rpt)]
        pltpu.make_async_copy(rows, rows, sem.at[s]).wait()

    slot = lax.rem(i, 2)
    for s in range(2):
        @pl.when(slot == s)
        def _(s=s):
            stage[s] = hp_ref[...]
            for idx, cp in enumerate(copies(i, s)):
                cp.start(priority=idx % 2)

        @pl.when((slot == 1 - s) & (i > 0))
        def _(s=s):
            wait_slot(s)

        @pl.when((slot == s) & (i == n - 1))
        def _(s=s):
            wait_slot(s)


def _dispatch(pos_flat, tend, hp, n_pad, tile, rpt):
    T = hp.shape[0] // rpt
    tk = 128
    return pl.pallas_call(
        functools.partial(_dispatch_kernel, T=T, tk=tk, tile=tile, n_tiles=n_pad // tile, rpt=rpt),
        out_shape=jax.ShapeDtypeStruct((n_pad * rpt, LANES), U32),
        grid_spec=pltpu.PrefetchScalarGridSpec(
            num_scalar_prefetch=2,
            grid=(T // tk,),
            in_specs=[pl.BlockSpec((tk * rpt, LANES), lambda i, pos, tend: (i, 0))],
            out_specs=pl.BlockSpec(memory_space=pl.ANY),
            scratch_shapes=[pltpu.VMEM((2, tk * rpt, LANES), U32), pltpu.VMEM((tile * rpt, LANES), U32),
                            pltpu.SemaphoreType.DMA((2,)), pltpu.SemaphoreType.DMA],
        ),
        compiler_params=_cparams(1),
        name="dispatch",
    )(pos_flat, tend, hp)


def _expert_kernel(te_ref, na_ref, hs_ref, wg_ref, wu_ref, wd_ref, ys_ref,
                   wgst, wust, wdst, wgbf, wubf, wdbf, sem, *, l, tile):
    i = pl.program_id(0)
    na = na_ref[0]
    e = te_ref[i]
    first = (i < na) & ((i == 0) | (e != te_ref[jnp.maximum(i - 1, 0)]))

    def weight_copies(ex):
        return (pltpu.make_async_copy(wg_ref.at[l, ex], wgst, sem),
                pltpu.make_async_copy(wu_ref.at[l, ex], wust, sem),
                pltpu.make_async_copy(wd_ref.at[l, ex], wdst, sem))

    @pl.when(i == 0)
    def _():
        for cp in weight_copies(e):
            cp.start()

    @pl.when(first)
    def _():
        for cp in weight_copies(e):
            cp.wait()
        _cast_rows(wgst, wgbf, 256)
        _cast_rows(wust, wubf, 256)
        _cast_rows(wdst, wdbf, 256)
        j = lax.while_loop(lambda j: (j < na) & (te_ref[jnp.minimum(j, na - 1)] == e), lambda j: j + 1, i + 1)

        @pl.when(j < na)
        def _():
            for cp in weight_copies(te_ref[jnp.minimum(j, na - 1)]):
                cp.start()

    @pl.when(i < na)
    def _():
        rin = hs_ref.shape[0] // tile
        w = jnp.concatenate([hs_ref[pl.ds(j, tile, stride=rin), :] for j in range(rin)], axis=1)
        half = w.shape[1]
        lo = lax.bitcast_convert_type(w << 16, F32).astype(BF16)
        hi = lax.bitcast_convert_type(w & jnp.uint32(0xFFFF0000), F32).astype(BF16)
        a = _dot(lo, wgbf[0:half, :]) + _dot(hi, wgbf[half:, :])
        b = _dot(lo, wubf[0:half, :]) + _dot(hi, wubf[half:, :])
        hid = (a * _sigmoid(a) * b).astype(BF16)
        ys_ref[...] = _dot(hid, wdbf[...])

    @pl.when(i >= na)
    def _():
        ys_ref[...] = jnp.zeros_like(ys_ref)


def _experts(te, na, hs, wg, wu, wd, l, tile, n_tiles_max):
    D, F = wg.shape[-2:]
    rin = D // 2 // LANES
    n_pad = hs.shape[0] // rin

    def row_map(i, te_ref, na_ref):
        return (jnp.minimum(i, na_ref[0] - 1), 0)

    any_spec = pl.BlockSpec(memory_space=pl.ANY)
    return pl.pallas_call(
        functools.partial(_expert_kernel, l=l, tile=tile),
        out_shape=jax.ShapeDtypeStruct((n_pad, D), F32),
        grid_spec=pltpu.PrefetchScalarGridSpec(
            num_scalar_prefetch=2,
            grid=(n_tiles_max,),
            in_specs=[pl.BlockSpec((tile * rin, LANES), row_map), any_spec, any_spec, any_spec],
            out_specs=pl.BlockSpec((tile, D), lambda i, te_ref, na_ref: (i, 0)),
            scratch_shapes=[pltpu.VMEM((D, F), F32), pltpu.VMEM((D, F), F32), pltpu.VMEM((F, D), F32),
                            pltpu.VMEM((D, F), BF16), pltpu.VMEM((D, F), BF16), pltpu.VMEM((F, D), BF16),
                            pltpu.SemaphoreType.DMA],
        ),
        compiler_params=_cparams(1),
        name="experts",
    )(te, na, hs, wg, wu, wd)


def _combine_kernel(pos_ref, x1_ref, rt_ref, g_ref, ys_ref, *rest, T, tk, n_prompt):
    final = n_prompt is not None
    out_a, out_b, buf, sem = rest
    i = pl.program_id(0)
    n = pl.num_programs(0)

    def copies(step, slot):
        out = []
        for u in range(tk):
            t = step * tk + u
            for k in range(2):
                p = pos_ref[k * T + t]
                out.append(pltpu.make_async_copy(ys_ref.at[pl.ds(p, 1)], buf.at[slot, k, pl.ds(u, 1)],
                                                 sem.at[slot]))
        return out

    slot = lax.rem(i, 2)

    @pl.when(i == 0)
    def _():
        for idx, cp in enumerate(copies(0, 0)):
            cp.start(priority=idx % 2)

    for s in range(2):
        @pl.when((i + 1 < n) & (slot == 1 - s))
        def _(s=s):
            for idx, cp in enumerate(copies(i + 1, s)):
                cp.start(priority=idx % 2)

    for s in range(2):
        @pl.when(slot == s)
        def _(s=s):
            pltpu.make_async_copy(buf.at[s], buf.at[s], sem.at[s]).wait()
            rt = rt_ref[...]
            w1 = rt[:, 2:3]
            w2 = rt[:, 3:4]
            x2 = x1_ref[...] + w1 * buf[s, 0] + w2 * buf[s, 1]
            hn = _rms_scale(x2, g_ref[...])
            if not final:
                out_a[...] = x2
                out_b[...] = hn.astype(out_b.dtype)
            else:
                @pl.when(i < n_prompt)
                def _():
                    out_a[...] = hn

                @pl.when(i >= n_prompt)
                def _():
                    out_b[...] = hn


def _combine(pos_flat, x1, route, g_all, gl, ys, n_sample=None):
    T, D = x1.shape
    tk = 128
    R = route.shape[1]
    if n_sample is None:
        n_prompt = None
        g_spec = pl.BlockSpec((None, 1, D), lambda i, pos: (gl, 0, 0))
        out_shape = [jax.ShapeDtypeStruct((T, D), F32), jax.ShapeDtypeStruct((T, D), BF16)]
        out_specs = [pl.BlockSpec((tk, D), lambda i, pos: (i, 0)), pl.BlockSpec((tk, D), lambda i, pos: (i, 0))]
    else:
        assert n_sample == tk
        n_prompt = (T - n_sample) // tk
        g_spec = pl.BlockSpec((1, D), lambda i, pos: (0, 0))
        out_shape = [jax.ShapeDtypeStruct((T - n_sample, D), F32), jax.ShapeDtypeStruct((n_sample, D), F32)]
        out_specs = [pl.BlockSpec((tk, D), lambda i, pos: (jnp.minimum(i, n_prompt - 1), 0)),
                     pl.BlockSpec((tk, D), lambda i, pos: (0, 0))]
    return pl.pallas_call(
        functools.partial(_combine_kernel, T=T, tk=tk, n_prompt=n_prompt),
        out_shape=out_shape,
        grid_spec=pltpu.PrefetchScalarGridSpec(
            num_scalar_prefetch=1,
            grid=(T // tk,),
            in_specs=[pl.BlockSpec((tk, D), lambda i, pos: (i, 0)),
                      pl.BlockSpec((tk, R), lambda i, pos: (i, 0)),
                      g_spec,
                      pl.BlockSpec(memory_space=pl.ANY)],
            out_specs=out_specs,
            scratch_shapes=[pltpu.VMEM((2, 2, tk, D), F32), pltpu.SemaphoreType.DMA((2,))],
        ),
        compiler_params=_cparams(1),
        name="combine",
    )(pos_flat, x1, route, g_all, ys)


def kernel(x_prompt, x_sample, state_conv, state_pool, state_ssm_re, state_ssm_im,
           norm_mix_g, norm_ffn_g, w_in,
           sgu_ln_g, sgu_ln_b, sgu_w, sgu_b,
           conv_w, conv_b, conv_ln_g, conv_ln_b,
           ssm_a_re, ssm_a_im, ssm_log_dt, ssm_b_re, ssm_b_im, ssm_c_re, ssm_c_im, ssm_d, ssm_w_glu, ssm_b_glu,
           pool_w, pool_scale,
           w_branch, w_out,
           router_group_w, router_group_b, router_expert_w, router_expert_b,
           expert_w_gate, expert_w_up, expert_w_down,
           final_norm_g):
    B, L, D = x_prompt.shape
    DB = x_sample.shape[0]
    depth = w_in.shape[0]
    W = sgu_ln_g.shape[-1]
    G, P = ssm_a_re.shape[1:]
    GP = G * P
    T = B * L + DB
    gate_off = 6 * W
    assert x_sample.shape[1] == 1 and L % TIME_TILE == 0 and (B * L) % DB == 0
    assert all(w & (w - 1) == 0 and w <= 16 for w in POOL_WINDOWS)

    row = lambda a: a.reshape(depth, 1, a.shape[-1])
    small = {
        "sgu_ln_g": row(sgu_ln_g), "sgu_ln_b": row(sgu_ln_b),
        "sgu_bT": jnp.swapaxes(sgu_b, 1, 2),
        "sgu_w00": row(jnp.repeat(sgu_w[:, :, 0, 0], W // SGU_GROUPS, axis=-1)),
        "sgu_b0": row(jnp.repeat(sgu_b[:, :, 0], W // SGU_GROUPS, axis=-1)),
        "conv_w": conv_w, "conv_w8": jnp.repeat(conv_w, SUBLANES, axis=1), "conv_b": row(conv_b), "conv_ln_g": row(conv_ln_g), "conv_ln_b": row(conv_ln_b),
        "ssm_d": row(ssm_d), "ssm_b_glu": row(ssm_b_glu), "pool_scale": row(pool_scale),
    }
    S = TIME_TILE // SUBLANES
    lamr, lami, pwr, pwi, bbr, bbi, cr_bd, ci_bd, sguw_bf, wglu_bf, poolw_bf = _prep(
        ssm_a_re, ssm_a_im, ssm_log_dt, ssm_b_re, ssm_b_im, ssm_c_re, ssm_c_im, sgu_w, ssm_w_glu, pool_w, S)
    prep = {"lamr": lamr, "lami": lami, "pwr": pwr, "pwi": pwi, "bbr": bbr, "bbi": bbi,
            "sguw": sguw_bf, "wglu": wglu_bf, "poolw": poolw_bf}

    wr = jnp.concatenate([router_group_w,
                          jnp.transpose(router_expert_w, (0, 2, 1, 3)).reshape(depth, D, N_EXPERTS)], axis=-1)
    wr = jnp.pad(wr, ((0, 0), (0, 0), (0, LANES - wr.shape[-1])))
    rb = jnp.concatenate([router_group_b, router_expert_b.reshape(depth, N_EXPERTS)], axis=-1)
    rb = jnp.pad(rb, ((0, 0), (0, LANES - rb.shape[-1]))).reshape(depth, 1, LANES)

    norm_mix3 = row(norm_mix_g)
    norm_ffn3 = row(norm_ffn_g)
    final3 = final_norm_g.reshape(1, D)

    n_tiles_max = -(-2 * T // EXPERT_TILE) + N_EXPERTS
    n_pad = n_tiles_max * EXPERT_TILE

    x = jnp.concatenate([x_prompt.reshape(B * L, D), x_sample.reshape(DB, D)], axis=0)
    h = _rmsnorm(x, norm_mix3, 0, BF16)

    stc2 = state_conv.reshape(depth, DB, (CONV_K - 1) * W)
    stp2 = state_pool.reshape(depth, DB, POOL_HIST * W)
    hre2 = state_ssm_re.reshape(depth, DB, GP)
    him2 = state_ssm_im.reshape(depth, DB, GP)

    conv_p, conv_s, pool_p, pool_s = [], [], [], []
    sre_p, sim_p, sre_s, sim_s, v_s = [], [], [], [], []
    y_p = y_s = None
    for l in range(depth):
        o, cp, pp, rp, ip = _mixer_prompt(h, w_in, B, L, T, l, small, prep, cr_bd, ci_bd)
        o, g_s, vs, d_in_s, xs_re, xs_im = _mixer_sample(h, w_in, o, stc2, stp2, hre2, him2, DB, T, l,
                                                         small, prep, cr_bd, ci_bd)
        merged = _merge(h, o, w_in, w_branch, l, gate_off)
        x1, hp, logits = _out_proj(merged, x, w_out, norm_ffn3, wr, l)
        route, meta = _route(logits, rb, l, EXPERT_TILE)
        pos_flat = jnp.transpose(route[:, 4:6]).astype(I32).reshape(2 * T)
        te = meta[2, :n_tiles_max].astype(I32)
        na = meta[3, :1].astype(I32)
        tend = meta[4, :N_EXPERTS].astype(I32)
        hs = _dispatch(pos_flat, tend, hp, n_pad, EXPERT_TILE, D // 2 // LANES)
        ys = _experts(te, na, hs, expert_w_gate, expert_w_up, expert_w_down, l, EXPERT_TILE, n_tiles_max)
        if l + 1 < depth:
            x, h = _combine(pos_flat, x1, route, norm_mix3, l + 1, ys)
        else:
            y_p, y_s = _combine(pos_flat, x1, route, final3, 0, ys, n_sample=DB)

        conv_p.append(cp)
        pool_p.append(pp)
        sre_p.append(rp.reshape(B, G, P))
        sim_p.append(ip.reshape(B, G, P))
        conv_s.append(jnp.concatenate([state_conv[l][:, 1:], g_s[:, None, :]], axis=1))
        pool_s.append(jnp.concatenate([state_pool[l][:, 1:], d_in_s[:, None, :]], axis=1))
        sre_s.append(xs_re.reshape(DB, G, P))
        sim_s.append(xs_im.reshape(DB, G, P))
        v_s.append(vs[:, None, :])

    y_prompt = y_p.reshape(B, L, D)
    y_sample = y_s.reshape(DB, 1, D)
    return (y_prompt, y_sample, jnp.stack(conv_p), jnp.stack(conv_s), jnp.stack(pool_p), jnp.stack(pool_s),
            jnp.stack(sre_p), jnp.stack(sim_p), jnp.stack(sre_s), jnp.stack(sim_s), jnp.stack(v_s))
```

```python
import functools
import math

import jax
import jax.numpy as jnp
from jax import lax
from jax.experimental import pallas as pl
from jax.experimental.pallas import tpu as pltpu

F32 = jnp.float32
BF16 = jnp.bfloat16
I32 = jnp.int32
U32 = jnp.uint32

EPS = 1e-6
CHUNK = 128
SGU_GROUPS = 4
CONV_K = 31
POOL_WINDOWS = (2, 4, 8, 16)
POOL_HIST = 15
SSM_H = 16
SSM_P = 64
N_GROUPS = 4
EXP_PER_GROUP = 4
N_EXPERTS = 16
PAST_LEN = 16384

LANES = 128
SUBLANES = 8
VMEM_LIMIT = 60 * 1024 * 1024

EXPERT_TILE = 256
TIME_TILE = 256
SCAN_COLS = 512


def _cparams(n_axes):
    return pltpu.CompilerParams(dimension_semantics=("arbitrary",) * n_axes,
                                vmem_limit_bytes=VMEM_LIMIT)


def _gelu(x):
    c = math.sqrt(2.0 / math.pi)
    return 0.5 * x * (1.0 + jnp.tanh(c * (x + 0.044715 * (x * x * x))))


def _sigmoid(x):
    return 0.5 * jnp.tanh(0.5 * x) + 0.5


def _layernorm(x, g, b):
    xc = x - jnp.mean(x, axis=-1, keepdims=True)
    var = jnp.mean(xc * xc, axis=-1, keepdims=True)
    return xc * lax.rsqrt(var + EPS) * g + b


def _rms_scale(x, g):
    return x * lax.rsqrt(jnp.mean(x * x, axis=-1, keepdims=True) + EPS) * g


def _dot(a, b):
    return jnp.dot(a, b, preferred_element_type=F32)


def _cast_rows(src_ref, dst_ref, chunk):
    rows = src_ref.shape[0]

    def body(i, c):
        r = pl.multiple_of(i * chunk, chunk)
        dst_ref[pl.ds(r, chunk), :] = src_ref[pl.ds(r, chunk), :].astype(dst_ref.dtype)
        return c

    lax.fori_loop(0, rows // chunk, body, 0)


def _layer_spec(arr, l, single=False):
    nd = arr.ndim
    mode = {"pipeline_mode": pl.Buffered(1)} if single else {}
    return pl.BlockSpec((None,) + tuple(arr.shape[1:]), lambda *_: (l,) + (0,) * (nd - 1), **mode)


def _pick(n, cands):
    for c in cands:
        if n % c == 0:
            return c
    raise ValueError(f"no tile for {n}")


def _rmsnorm_kernel(x_ref, g_ref, o_ref):
    o_ref[...] = _rms_scale(x_ref[...], g_ref[...]).astype(o_ref.dtype)


def _rmsnorm(x, g_all, l, out_dtype):
    T, D = x.shape
    tm = _pick(T, (640, 512, 384, 256, 128))
    return pl.pallas_call(
        _rmsnorm_kernel,
        out_shape=jax.ShapeDtypeStruct((T, D), out_dtype),
        grid=(T // tm,),
        in_specs=[pl.BlockSpec((tm, D), lambda m: (m, 0)), _layer_spec(g_all, l)],
        out_specs=pl.BlockSpec((tm, D), lambda m: (m, 0)),
        compiler_params=_cparams(1),
        name="rmsnorm",
    )(x, g_all)


def _split3(x):
    p1 = x.astype(BF16)
    r1 = x - p1.astype(F32)
    p2 = r1.astype(BF16)
    p3 = (r1 - p2.astype(F32)).astype(BF16)
    return p1, p2, p3


def _prep_kernel(are, aim, ldt, bre, bim, cre, cim, sguw, wglu, poolw,
                 lamr_o, lami_o, pwr_o, pwi_o, bbr_o, bbi_o, cr_o, ci_o, sguw_o, wglu_o, poolw_o, *, S, P, H):
    a_re = are[...]
    a_im = aim[...]
    dt = jnp.exp(ldt[...])
    mag = jnp.exp(a_re * dt)
    lbr = mag * jnp.cos(a_im * dt)
    lbi = mag * jnp.sin(a_im * dt)
    den = a_re * a_re + a_im * a_im
    nr = lbr - 1.0
    kr = (nr * a_re + lbi * a_im) / den
    ki = (lbi * a_re - nr * a_im) / den
    lamr_o[...] = lbr
    lami_o[...] = lbi
    gp = a_re.shape[-1]

    w = bre.shape[0]
    sh_p, sh_h = P.bit_length() - 1, H.bit_length() - 1
    rc = 128
    pe = lax.broadcasted_iota(I32, (P, gp), 0)
    ce = lax.broadcasted_iota(I32, (P, gp), 1)
    rep_p = jnp.where((ce & (P - 1)) == pe, 1.0, 0.0).astype(BF16)
    for i in range(w // rc):
        rows = slice(i * rc, (i + 1) * rc)
        r_i = lax.broadcasted_iota(I32, (rc, gp), 0) + i * rc
        c_i = lax.broadcasted_iota(I32, (rc, gp), 1)
        diag = (r_i >> sh_h) == (c_i >> sh_p)
        br = sum(_dot(piece, rep_p) for piece in _split3(bre[rows, :]))
        bi = sum(_dot(piece, rep_p) for piece in _split3(bim[rows, :]))
        bbr_o[rows, :] = jnp.where(diag, kr * br - ki * bi, 0.0).astype(BF16)
        bbi_o[rows, :] = jnp.where(diag, kr * bi + ki * br, 0.0).astype(BF16)

    he = lax.broadcasted_iota(I32, (H, w), 0)
    ce = lax.broadcasted_iota(I32, (H, w), 1)
    rep_h = jnp.where((ce & (H - 1)) == he, 1.0, 0.0).astype(BF16)
    rc = 256
    for i in range(gp // rc):
        rows = slice(i * rc, (i + 1) * rc)
        r_i = lax.broadcasted_iota(I32, (rc, w), 0) + i * rc
        c_i = lax.broadcasted_iota(I32, (rc, w), 1)
        diag = (r_i >> sh_p) == (c_i >> sh_h)
        cr_o[rows, :] = jnp.where(diag, _dot(cre[rows, :].astype(BF16), rep_h), 0.0).astype(BF16)
        ci_o[rows, :] = jnp.where(diag, _dot(cim[rows, :].astype(BF16), rep_h), 0.0).astype(BF16)

    pr, pi = lbr, lbi
    for s in range(S):
        pwr_o[SUBLANES * s:SUBLANES * (s + 1), :] = jnp.broadcast_to(pr, (SUBLANES, gp))
        pwi_o[SUBLANES * s:SUBLANES * (s + 1), :] = jnp.broadcast_to(pi, (SUBLANES, gp))
        pr, pi = pr * lbr - pi * lbi, pr * lbi + pi * lbr

    t_i = lax.broadcasted_iota(I32, (CHUNK, CHUNK), 0)
    s_i = lax.broadcasted_iota(I32, (CHUNK, CHUNK), 1)
    for g in range(SGU_GROUPS):
        sguw_o[g] = jnp.where(t_i >= s_i, sguw[g], 0.0).astype(BF16)
    wglu_o[...] = wglu[...].astype(BF16)
    for g in range(len(POOL_WINDOWS)):
        poolw_o[g] = poolw[g].astype(BF16)


def _prep(a_re, a_im, log_dt, b_re, b_im, c_re, c_im, sgu_w, w_glu, pool_w, S):
    L, G, P = a_re.shape
    H = b_re.shape[-1]
    GP, W = G * P, G * H
    assert P & (P - 1) == 0 and H & (H - 1) == 0
    bre_t = jnp.swapaxes(b_re, 2, 3).reshape(L, W, P)
    bim_t = jnp.swapaxes(b_im, 2, 3).reshape(L, W, P)
    cre_t = jnp.swapaxes(c_re, 2, 3).reshape(L, GP, H)
    cim_t = jnp.swapaxes(c_im, 2, 3).reshape(L, GP, H)
    are2 = a_re.reshape(L, 1, GP)
    aim2 = a_im.reshape(L, 1, GP)
    ldt2 = jnp.repeat(log_dt, P, axis=-1).reshape(L, 1, GP)

    def lspec(shape):
        nd = len(shape)
        return pl.BlockSpec((None,) + tuple(shape[1:]), lambda l: (l,) + (0,) * (nd - 1))

    ins = [are2, aim2, ldt2, bre_t, bim_t, cre_t, cim_t, sgu_w, w_glu, pool_w]
    out_shapes = [
        jax.ShapeDtypeStruct((L, 1, GP), F32), jax.ShapeDtypeStruct((L, 1, GP), F32),
        jax.ShapeDtypeStruct((L, SUBLANES * S, GP), F32), jax.ShapeDtypeStruct((L, SUBLANES * S, GP), F32),
        jax.ShapeDtypeStruct((L, W, GP), BF16), jax.ShapeDtypeStruct((L, W, GP), BF16),
        jax.ShapeDtypeStruct((L, GP, W), BF16), jax.ShapeDtypeStruct((L, GP, W), BF16),
        jax.ShapeDtypeStruct(sgu_w.shape, BF16), jax.ShapeDtypeStruct(w_glu.shape, BF16),
        jax.ShapeDtypeStruct(pool_w.shape, BF16),
    ]
    return pl.pallas_call(
        functools.partial(_prep_kernel, S=S, P=P, H=H),
        out_shape=out_shapes,
        grid=(L,),
        in_specs=[lspec(a.shape) for a in ins],
        out_specs=[lspec(o.shape) for o in out_shapes],
        compiler_params=_cparams(1),
        name="ssm_prep",
    )(*ins)


PROJ_CHUNK = 256


def _stream_weight_cols(w_hbm, l, n_cols, stage, sem, consume):
    def copy(c, slot):
        return pltpu.make_async_copy(w_hbm.at[l, :, pl.ds(c * PROJ_CHUNK, PROJ_CHUNK)], stage.at[slot],
                                     sem.at[slot])

    n = n_cols // PROJ_CHUNK
    copy(0, 0).start()
    for c in range(n):
        slot = c % 2
        if c + 1 < n:
            copy(c + 1, 1 - slot).start()
        copy(c, slot).wait()
        consume(c, slot)


N_MIXER_INPUTS = 23


def _mixer_prompt_kernel(*refs, n_batch, **kw):
    o_ref = refs[N_MIXER_INPUTS]
    b = pl.program_id(0)

    @pl.when(b < n_batch)
    def _():
        _mixer_prompt_body(*refs, **kw)

    @pl.when((b == n_batch) & (pl.program_id(1) == 0))
    def _():
        o_ref[...] = jnp.zeros_like(o_ref)


def _mixer_prompt_body(h_ref, w_hbm, lng, lnb, sguw, sgub, cw, cb, clg, clb,
                       lamr, lami, pwr, pwi, bbr, bbi, cr, ci, sd, wglu, bglu, poolw, pscale,
                       o_ref, conv_o, pool_o, sre_o, sim_o,
                       w6bf, wstage, wsem, p_ref,
                       gbuf, gsh, pbuf, plvl, xr, xi, car_re, car_im, fin_re, fin_im, cm_re, cm_im,
                       *, l, tt, S, W, GP):
    j = pl.program_id(1)
    nt = pl.num_programs(1)
    GH = 32
    PH = 32

    @pl.when((pl.program_id(0) == 0) & (j == 0))
    def _():
        def consume(c, slot):
            _cast_rows(wstage.at[slot], w6bf.at[:, c * PROJ_CHUNK:(c + 1) * PROJ_CHUNK], 256)

        _stream_weight_cols(w_hbm, l, 6 * W, wstage, wsem, consume)

    @pl.when(j == 0)
    def _():
        gbuf[0:GH, :] = jnp.zeros((GH, W), F32)
        pbuf[0:PH, :] = jnp.zeros((PH, W), F32)
        car_re[...] = jnp.zeros_like(car_re)
        car_im[...] = jnp.zeros_like(car_im)

    h = h_ref[...]
    for c in range(3):
        p_ref[:, 2 * c * W:2 * (c + 1) * W] = _dot(h, w6bf[:, 2 * c * W:2 * (c + 1) * W])

    for c in range(tt // CHUNK):
        rows = slice(c * CHUNK, (c + 1) * CHUNK)
        u = _gelu(p_ref[rows, 0:W])
        v = _layernorm(_gelu(p_ref[rows, W:2 * W]), lng[...], lnb[...])
        vb = v.astype(BF16)
        gw = W // SGU_GROUPS
        for g in range(SGU_GROUPS):
            cols = slice(g * gw, (g + 1) * gw)
            mixed = _dot(sguw[g], vb[:, cols]) + sgub[:, g:g + 1]
            o_ref[rows, cols] = (u[:, cols] * mixed).astype(BF16)

    rc = 64
    for c in range(tt // rc):
        rows = slice(c * rc, (c + 1) * rc)
        gbuf[GH + c * rc:GH + (c + 1) * rc, :] = p_ref[rows, 2 * W:3 * W] * _sigmoid(p_ref[rows, 3 * W:4 * W])
    sh_rows = gsh.shape[1]
    for b in range(1, SUBLANES):
        gsh[b] = gbuf[b:b + sh_rows, :]
    rc = 32
    base = GH - (CONV_K - 1)
    for c in range(tt // rc):
        acc = jnp.zeros((rc, W), F32)
        for k in range(CONV_K):
            r0 = base + c * rc + k
            b, a0 = r0 % SUBLANES, r0 - r0 % SUBLANES
            rows = gbuf[a0:a0 + rc, :] if b == 0 else gsh[b, a0:a0 + rc, :]
            wk = jnp.concatenate([cw[SUBLANES * k:SUBLANES * (k + 1), :]] * (rc // SUBLANES), axis=0)
            acc = acc + wk * rows
        y = _layernorm(acc + cb[...], clg[...], clb[...])
        o_ref[c * rc:(c + 1) * rc, W:2 * W] = (y * _sigmoid(y)).astype(BF16)
    gbuf[0:GH, :] = gbuf[tt:tt + GH, :]

    uc = p_ref[:, 4 * W:5 * W]
    r_i = lax.broadcasted_iota(I32, (tt, tt), 0)
    c_i = lax.broadcasted_iota(I32, (tt, tt), 1)
    perm = jnp.where(((r_i & (SUBLANES - 1)) * S + (r_i >> 3)) == c_i, 1.0, 0.0).astype(BF16)
    up = _dot(perm, uc.astype(BF16)).astype(BF16)
    hw, hg = W // 2, GP // 2
    for q in range(2):
        xr[:, q * hg:(q + 1) * hg] = _dot(up[:, q * hw:(q + 1) * hw], bbr[q * hw:(q + 1) * hw, q * hg:(q + 1) * hg])
        xi[:, q * hg:(q + 1) * hg] = _dot(up[:, q * hw:(q + 1) * hw], bbi[q * hw:(q + 1) * hw, q * hg:(q + 1) * hg])

    for cbi in range(GP // SCAN_COLS):
        cols = slice(cbi * SCAN_COLS, (cbi + 1) * SCAN_COLS)
        lr = jnp.broadcast_to(lamr[:, cols], (SUBLANES, SCAN_COLS))
        li = jnp.broadcast_to(lami[:, cols], (SUBLANES, SCAN_COLS))

        def step(s, carry, cols=cols, lr=lr, li=li):
            sr, si = carry
            r0 = pl.multiple_of(s * SUBLANES, SUBLANES)
            nr = lr * sr - li * si + xr[pl.ds(r0, SUBLANES), cols]
            ni = lr * si + li * sr + xi[pl.ds(r0, SUBLANES), cols]
            xr[pl.ds(r0, SUBLANES), cols] = nr
            xi[pl.ds(r0, SUBLANES), cols] = ni
            return nr, ni

        z = jnp.zeros((SUBLANES, SCAN_COLS), F32)
        fr, fi = lax.fori_loop(0, S, step, (z, z), unroll=True)
        fin_re[:, cols] = fr
        fin_im[:, cols] = fi

    lsr = pwr[SUBLANES * (S - 1):SUBLANES * (S - 1) + 1, :]
    lsi = pwi[SUBLANES * (S - 1):SUBLANES * (S - 1) + 1, :]
    c_r = car_re[0:1, :]
    c_im = car_im[0:1, :]
    cm_re[0:1, :] = c_r
    cm_im[0:1, :] = c_im
    for q in range(1, SUBLANES):
        f_r = fin_re[q - 1:q, :]
        f_i = fin_im[q - 1:q, :]
        c_r, c_im = f_r + lsr * c_r - lsi * c_im, f_i + lsr * c_im + lsi * c_r
        cm_re[q:q + 1, :] = c_r
        cm_im[q:q + 1, :] = c_im
    n_r = fin_re[SUBLANES - 1:SUBLANES, :] + lsr * c_r - lsi * c_im
    n_i = fin_im[SUBLANES - 1:SUBLANES, :] + lsr * c_im + lsi * c_r
    car_re[0:1, :] = n_r
    car_im[0:1, :] = n_i

    for cbi in range(GP // SCAN_COLS):
        cols = slice(cbi * SCAN_COLS, (cbi + 1) * SCAN_COLS)
        mr = cm_re[:, cols]
        mi = cm_im[:, cols]

        def fix(s, c, cols=cols, mr=mr, mi=mi):
            r0 = pl.multiple_of(s * SUBLANES, SUBLANES)
            pr = pwr[pl.ds(r0, SUBLANES), cols]
            pi = pwi[pl.ds(r0, SUBLANES), cols]
            xr[pl.ds(r0, SUBLANES), cols] = xr[pl.ds(r0, SUBLANES), cols] + (pr * mr - pi * mi)
            xi[pl.ds(r0, SUBLANES), cols] = xi[pl.ds(r0, SUBLANES), cols] + (pr * mi + pi * mr)
            return c

        lax.fori_loop(0, S, fix, 0, unroll=True)

    yp = jnp.concatenate(
        [_dot(xr[:, q * hg:(q + 1) * hg].astype(BF16), cr[q * hg:(q + 1) * hg, q * hw:(q + 1) * hw])
         - _dot(xi[:, q * hg:(q + 1) * hg].astype(BF16), ci[q * hg:(q + 1) * hg, q * hw:(q + 1) * hw])
         for q in range(2)], axis=1)
    unperm = jnp.where(((c_i & (SUBLANES - 1)) * S + (c_i >> 3)) == r_i, 1.0, 0.0).astype(BF16)
    y1 = yp.astype(BF16)
    r1 = yp - y1.astype(F32)
    y2 = r1.astype(BF16)
    y3 = (r1 - y2.astype(F32)).astype(BF16)
    y = _dot(unperm, y1) + _dot(unperm, y2) + _dot(unperm, y3)
    z = _gelu(y + sd[...] * uc)
    gl = _dot(z.astype(BF16), wglu[...]) + bglu[...]
    o_ref[:, 2 * W:3 * W] = (z * _sigmoid(gl)).astype(BF16)

    xd = p_ref[:, 5 * W:6 * W]
    pbuf[PH:PH + tt, :] = xd
    gw = W // len(POOL_WINDOWS)
    pos = (j * tt + lax.broadcasted_iota(I32, (tt, 1), 0) + 1).astype(F32)
    li = 0
    for gi, win in enumerate(POOL_WINDOWS):
        cols = slice(gi * gw, (gi + 1) * gw)
        nlev = win.bit_length() - 1
        load = lambda r0, n, cols=cols: pbuf[r0:r0 + n, cols]
        for k in range(1, nlev):
            lo, sh = SUBLANES * k, 1 << (k - 1)
            n = PH + tt - lo
            plvl[li, lo:lo + n, :] = load(lo, n) + load(lo - sh, n)
            load = lambda r0, n, li=li: plvl[li, r0:r0 + n, :]
            li += 1
        s = load(PH, tt) + load(PH - (win >> 1), tt)
        cnt = jnp.minimum(pos, float(win))
        pooled = s / cnt - xd[:, cols]
        mixed = _dot(pooled.astype(BF16), poolw[gi]) * pscale[:, cols]
        o_ref[:, 3 * W + gi * gw:3 * W + (gi + 1) * gw] = mixed.astype(BF16)
    pbuf[0:PH, :] = pbuf[tt:tt + PH, :]

    @pl.when(j == nt - 1)
    def _():
        conv_o[...] = gbuf[GH - (CONV_K - 1):GH, :]
        pool_o[...] = pbuf[PH - POOL_HIST:PH, :]
        sre_o[...] = car_re[0:1, :]
        sim_o[...] = car_im[0:1, :]


def _mixer_prompt(h, w_in, B, L, T, l, small, prep, cr, ci):
    W = small["sgu_ln_g"].shape[-1]
    GP = prep["lamr"].shape[-1]
    D = h.shape[1]
    D4 = 4 * W
    tt = TIME_TILE
    S = tt // SUBLANES
    nt = L // tt
    ins = [h, w_in,
           small["sgu_ln_g"], small["sgu_ln_b"], prep["sguw"], small["sgu_bT"],
           small["conv_w8"], small["conv_b"], small["conv_ln_g"], small["conv_ln_b"],
           prep["lamr"], prep["lami"], prep["pwr"], prep["pwi"], prep["bbr"], prep["bbi"], cr, ci,
           small["ssm_d"], prep["wglu"], small["ssm_b_glu"], prep["poolw"], small["pool_scale"]]
    assert len(ins) == N_MIXER_INPUTS
    in_specs = [pl.BlockSpec((tt, D), lambda b, j: (jnp.minimum(b * nt + j, B * nt), 0)),
                pl.BlockSpec(memory_space=pl.ANY)]
    in_specs += [_layer_spec(a, l, single=True) for a in ins[2:]]
    out_shapes = [
        jax.ShapeDtypeStruct((T, D4), BF16),
        jax.ShapeDtypeStruct((B, CONV_K - 1, W), F32),
        jax.ShapeDtypeStruct((B, POOL_HIST, W), F32),
        jax.ShapeDtypeStruct((B, 1, GP), F32),
        jax.ShapeDtypeStruct((B, 1, GP), F32),
    ]
    out_specs = [
        pl.BlockSpec((tt, D4), lambda b, j: (jnp.minimum(b * nt + j, B * nt), 0)),
        pl.BlockSpec((None, CONV_K - 1, W), lambda b, j: (jnp.minimum(b, B - 1), 0, 0)),
        pl.BlockSpec((None, POOL_HIST, W), lambda b, j: (jnp.minimum(b, B - 1), 0, 0)),
        pl.BlockSpec((None, 1, GP), lambda b, j: (jnp.minimum(b, B - 1), 0, 0)),
        pl.BlockSpec((None, 1, GP), lambda b, j: (jnp.minimum(b, B - 1), 0, 0)),
    ]
    scratch = [
        pltpu.VMEM((D, 6 * W), BF16), pltpu.VMEM((2, D, PROJ_CHUNK), F32), pltpu.SemaphoreType.DMA((2,)),
        pltpu.VMEM((tt, 6 * W), F32),
        pltpu.VMEM((32 + tt, W), F32), pltpu.VMEM((SUBLANES, 32 + tt - SUBLANES, W), F32),
        pltpu.VMEM((32 + tt, W), F32), pltpu.VMEM((6, 32 + tt, W // len(POOL_WINDOWS)), F32),
        pltpu.VMEM((tt, GP), F32), pltpu.VMEM((tt, GP), F32),
        pltpu.VMEM((SUBLANES, GP), F32), pltpu.VMEM((SUBLANES, GP), F32),
        pltpu.VMEM((SUBLANES, GP), F32), pltpu.VMEM((SUBLANES, GP), F32),
        pltpu.VMEM((SUBLANES, GP), F32), pltpu.VMEM((SUBLANES, GP), F32),
    ]
    return pl.pallas_call(
        functools.partial(_mixer_prompt_kernel, n_batch=B, l=l, tt=tt, S=S, W=W, GP=GP),
        out_shape=out_shapes,
        grid=(B + 1, nt),
        in_specs=in_specs,
        out_specs=out_specs,
        scratch_shapes=scratch,
        compiler_params=_cparams(2),
        name="mixer_prompt",
    )(*ins)


def _mixer_sample_kernel(h_ref, w_hbm, stc, stp, hre, him, lng, lnb, w00, b0, cw, cb, clg, clb,
                         lamr, lami, bbr, bbi, cr, ci, sd, wglu, bglu, poolw, pscale, o_in,
                         o_ref, g_o, v_o, d_o, xre_o, xim_o,
                         p_ref, wstage, wchunk, wsem, *, l, W):
    del o_in
    h = h_ref[...]

    def consume(c, slot):
        _cast_rows(wstage.at[slot], wchunk, 256)
        p_ref[:, c * PROJ_CHUNK:(c + 1) * PROJ_CHUNK] = _dot(h, wchunk[...])

    _stream_weight_cols(w_hbm, l, 6 * W, wstage, wsem, consume)

    u = _gelu(p_ref[:, 0:W])
    v = _layernorm(_gelu(p_ref[:, W:2 * W]), lng[...], lnb[...])
    v_o[...] = v
    o_ref[:, 0:W] = (u * (w00[...] * v + b0[...])).astype(BF16)

    g = p_ref[:, 2 * W:3 * W] * _sigmoid(p_ref[:, 3 * W:4 * W])
    g_o[...] = g
    acc = cw[CONV_K - 1:CONV_K, :] * g
    for k in range(CONV_K - 1):
        acc = acc + cw[k:k + 1, :] * stc[:, k * W:(k + 1) * W]
    y = _layernorm(acc + cb[...], clg[...], clb[...])
    o_ref[:, W:2 * W] = (y * _sigmoid(y)).astype(BF16)

    uc = p_ref[:, 4 * W:5 * W]
    ub = uc.astype(BF16)
    h_r = hre[...]
    h_i = him[...]
    l_r = lamr[...]
    l_i = lami[...]
    x_r = l_r * h_r - l_i * h_i + _dot(ub, bbr[...])
    x_i = l_r * h_i + l_i * h_r + _dot(ub, bbi[...])
    xre_o[...] = x_r
    xim_o[...] = x_i
    yv = _dot(x_r.astype(BF16), cr[...]) - _dot(x_i.astype(BF16), ci[...]) + sd[...] * uc
    z = _gelu(yv)
    gl = _dot(z.astype(BF16), wglu[...]) + bglu[...]
    o_ref[:, 2 * W:3 * W] = (z * _sigmoid(gl)).astype(BF16)

    xd = p_ref[:, 5 * W:6 * W]
    d_o[...] = xd
    gw = W // len(POOL_WINDOWS)
    for gi, win in enumerate(POOL_WINDOWS):
        cols = slice(gi * gw, (gi + 1) * gw)
        s = xd[:, cols]
        for i in range(1, win):
            r = POOL_HIST - i
            s = s + stp[:, r * W + gi * gw:r * W + (gi + 1) * gw]
        cnt = float(min(PAST_LEN + 1, win))
        pooled = s / cnt - xd[:, cols]
        mixed = _dot(pooled.astype(BF16), poolw[gi]) * pscale[:, cols]
        o_ref[:, 3 * W + gi * gw:3 * W + (gi + 1) * gw] = mixed.astype(BF16)


def _mixer_sample(h, w_in, o_prev, stc2, stp2, hre2, him2, DB, T, l, small, prep, cr, ci):
    W = small["sgu_ln_g"].shape[-1]
    GP = prep["lamr"].shape[-1]
    D = h.shape[1]
    D4 = 4 * W
    blk = (T - DB) // DB
    ins = [h, w_in, stc2, stp2, hre2, him2,
           small["sgu_ln_g"], small["sgu_ln_b"], small["sgu_w00"], small["sgu_b0"],
           small["conv_w"], small["conv_b"], small["conv_ln_g"], small["conv_ln_b"],
           prep["lamr"], prep["lami"], prep["bbr"], prep["bbi"], cr, ci,
           small["ssm_d"], prep["wglu"], small["ssm_b_glu"], prep["poolw"], small["pool_scale"], o_prev]
    in_specs = [pl.BlockSpec((DB, D), lambda i: (blk, 0)), pl.BlockSpec(memory_space=pl.ANY)]
    in_specs += [_layer_spec(a, l, single=True) for a in ins[2:-1]]
    in_specs += [pl.BlockSpec(memory_space=pl.ANY)]
    out_shapes = [
        jax.ShapeDtypeStruct((T, D4), BF16),
        jax.ShapeDtypeStruct((DB, W), F32), jax.ShapeDtypeStruct((DB, W), F32), jax.ShapeDtypeStruct((DB, W), F32),
        jax.ShapeDtypeStruct((DB, GP), F32), jax.ShapeDtypeStruct((DB, GP), F32),
    ]
    out_specs = [
        pl.BlockSpec((DB, D4), lambda i: (blk, 0)),
        pl.BlockSpec((DB, W), lambda i: (0, 0)), pl.BlockSpec((DB, W), lambda i: (0, 0)),
        pl.BlockSpec((DB, W), lambda i: (0, 0)),
        pl.BlockSpec((DB, GP), lambda i: (0, 0)), pl.BlockSpec((DB, GP), lambda i: (0, 0)),
    ]
    return pl.pallas_call(
        functools.partial(_mixer_sample_kernel, l=l, W=W),
        out_shape=out_shapes,
        grid=(1,),
        in_specs=in_specs,
        out_specs=out_specs,
        scratch_shapes=[pltpu.VMEM((DB, 6 * W), F32), pltpu.VMEM((2, D, PROJ_CHUNK), F32),
                        pltpu.VMEM((D, PROJ_CHUNK), BF16), pltpu.SemaphoreType.DMA((2,))],
        input_output_aliases={len(ins) - 1: 0},
        compiler_params=_cparams(1),
        name="mixer_sample",
    )(*ins)


def _merge_kernel(h_ref, o_ref, wg0, wg1, wg2, wg3, wb_ref, m_ref, wgbf, wbbf, *, W):
    wgs = (wg0, wg1, wg2, wg3)

    @pl.when(pl.program_id(1) == 0)
    def _():
        for b in range(4):
            _cast_rows(wgs[b], wgbf.at[b], 256)
            wbbf[b] = wb_ref[b].astype(BF16)

    h = h_ref[...]
    acc = None
    for b in range(4):
        gate = _sigmoid(_dot(h, wgbf[b]))
        term = gate * _dot(o_ref[:, b * W:(b + 1) * W], wbbf[b])
        acc = term if acc is None else acc + term
    m_ref[...] = acc.astype(BF16)


def _merge(h, o, w_in, w_branch, l, gate_off):
    T, D = h.shape
    W = w_branch.shape[2]
    tm = _pick(T, (1040, 640, 512, 384, 256, 128))
    tn = 256
    nb = D // tn

    def gate_spec(b):
        blk0 = (gate_off + b * D) // tn
        return pl.BlockSpec((None, D, tn), lambda n, m: (l, 0, blk0 + n))

    return pl.pallas_call(
        functools.partial(_merge_kernel, W=W),
        out_shape=jax.ShapeDtypeStruct((T, D), BF16),
        grid=(nb, T // tm),
        in_specs=[pl.BlockSpec((tm, D), lambda n, m: (m, 0)),
                  pl.BlockSpec((tm, 4 * W), lambda n, m: (m, 0)),
                  gate_spec(0), gate_spec(1), gate_spec(2), gate_spec(3),
                  pl.BlockSpec((None, 4, W, tn), lambda n, m: (l, 0, 0, n))],
        out_specs=pl.BlockSpec((tm, tn), lambda n, m: (m, n)),
        scratch_shapes=[pltpu.VMEM((4, D, tn), BF16), pltpu.VMEM((4, W, tn), BF16)],
        compiler_params=_cparams(2),
        name="gated_merge",
    )(h, o, w_in, w_in, w_in, w_in, w_branch)


def _out_kernel(m_ref, x_ref, w_ref, g_ref, wr_ref, x1_ref, hp_ref, lg_ref, wbf, wr2):
    R = wr_ref.shape[-1]

    @pl.when(pl.program_id(0) == 0)
    def _():
        _cast_rows(w_ref, wbf, 256)
        wr = wr_ref[...]
        hi = wr.astype(BF16)
        wr2[:, 0:R] = hi
        wr2[:, R:2 * R] = (wr - hi.astype(F32)).astype(BF16)

    x1 = x_ref[...] + _dot(m_ref[...], wbf[...])
    x1_ref[...] = x1
    h = _rms_scale(x1, g_ref[...])
    hb = h.astype(BF16)
    hl = (h - hb.astype(F32)).astype(BF16)
    both = _dot(hb, wr2[...])
    lg_ref[...] = both[:, 0:R] + both[:, R:2 * R] + _dot(hl, wr2[:, 0:R])
    bits = lax.bitcast_convert_type(hb.astype(F32), U32)
    half = bits.shape[1] // 2
    word = (bits[:, :half] >> 16) | (bits[:, half:] & jnp.uint32(0xFFFF0000))
    tm = word.shape[0]
    for j in range(half // LANES):
        hp_ref[pl.ds(j, tm, stride=half // LANES), :] = word[:, j * LANES:(j + 1) * LANES]


def _out_proj(merged, x, w_out, g_all, wr_all, l):
    T, D = x.shape
    tm = _pick(T, (416, 320, 256, 128))
    R = wr_all.shape[-1]
    return pl.pallas_call(
        _out_kernel,
        out_shape=[jax.ShapeDtypeStruct((T, D), F32), jax.ShapeDtypeStruct((T * (D // 2 // LANES), LANES), U32),
                   jax.ShapeDtypeStruct((T, R), F32)],
        grid=(T // tm,),
        in_specs=[pl.BlockSpec((tm, D), lambda m: (m, 0)),
                  pl.BlockSpec((tm, D), lambda m: (m, 0)),
                  pl.BlockSpec((None, D, D), lambda m: (l, 0, 0), pipeline_mode=pl.Buffered(1)),
                  _layer_spec(g_all, l), _layer_spec(wr_all, l)],
        out_specs=[pl.BlockSpec((tm, D), lambda m: (m, 0)),
                   pl.BlockSpec((tm * (D // 2 // LANES), LANES), lambda m: (m, 0)),
                   pl.BlockSpec((tm, R), lambda m: (m, 0))],
        scratch_shapes=[pltpu.VMEM((D, D), BF16), pltpu.VMEM((D, 2 * R), BF16)],
        compiler_params=_cparams(1),
        name="out_proj",
    )(merged, x, w_out, g_all, wr_all)


def _route_kernel(lg_ref, bias_ref, rt_ref, meta_ref, cnt, off, *, tm, tile):
    p = pl.program_id(0)
    m = pl.program_id(1)
    lane = lax.broadcasted_iota(I32, (tm, LANES), 1).astype(F32)
    neg = jnp.float32(-jnp.inf)
    big = jnp.float32(1e9)

    @pl.when((p == 0) & (m == 0))
    def _():
        cnt[...] = jnp.zeros_like(cnt)

    lg = lg_ref[...] + bias_ref[...]
    is_g = lane < N_GROUPS
    gl = jnp.where(is_g, lg, neg)
    gmax = jnp.max(gl, axis=-1, keepdims=True)
    gidx = jnp.min(jnp.where(gl == gmax, lane, big), axis=-1, keepdims=True)
    gsum = jnp.sum(jnp.where(is_g, jnp.exp(gl - gmax), 0.0), axis=-1, keepdims=True)
    g_w = 1.0 / gsum
    lo = N_GROUPS + EXP_PER_GROUP * gidx
    in_grp = (lane >= lo) & (lane < lo + EXP_PER_GROUP)
    el = jnp.where(in_grp, lg, neg)
    v1 = jnp.max(el, axis=-1, keepdims=True)
    i1 = jnp.min(jnp.where(el == v1, lane, big), axis=-1, keepdims=True)
    el2 = jnp.where(lane == i1, neg, el)
    v2 = jnp.max(el2, axis=-1, keepdims=True)
    i2 = jnp.min(jnp.where(el2 == v2, lane, big), axis=-1, keepdims=True)
    e2x = jnp.exp(v2 - v1)
    w1 = g_w / (1.0 + e2x)
    w2 = g_w * e2x / (1.0 + e2x)
    e1 = i1 - N_GROUPS
    e2 = i2 - N_GROUPS
    a1 = jnp.where(lane == e1, 1.0, 0.0)
    a2 = jnp.where(lane == e2, 1.0, 0.0)
    a = a1 + a2

    @pl.when(p == 0)
    def _():
        cnt[0:1, :] = cnt[0:1, :] + jnp.sum(a, axis=0, keepdims=True)

    @pl.when((p == 1) & (m == 0))
    def _():
        counts = cnt[0:1, :]
        tiles = jnp.floor((counts + (tile - 1)) * (1.0 / tile))
        tiles8 = jnp.broadcast_to(tiles, (SUBLANES, LANES)).astype(BF16)
        ri = lax.broadcasted_iota(I32, (LANES, LANES), 0)
        ci = lax.broadcasted_iota(I32, (LANES, LANES), 1)
        upper = jnp.where(ri < ci, 1.0, 0.0).astype(BF16)
        toff = _dot(tiles8, upper)[0:1, :]
        off[0:1, :] = toff * tile
        tend = toff + tiles
        lane1 = lax.broadcasted_iota(I32, (1, LANES), 1).astype(F32)
        n_act = jnp.sum(jnp.where(lane1 == N_EXPERTS - 1, tend, 0.0), axis=-1, keepdims=True)
        texp = jnp.zeros((1, LANES), F32)
        for e in range(N_EXPERTS - 1):
            end_e = jnp.sum(jnp.where(lane1 == e, tend, 0.0), axis=-1, keepdims=True)
            texp = texp + jnp.where(jnp.minimum(lane1, n_act - 1.0) >= end_e, 1.0, 0.0)
        meta_ref[0:1, :] = counts
        meta_ref[1:2, :] = off[0:1, :]
        meta_ref[2:3, :] = texp
        meta_ref[3:4, :] = jnp.broadcast_to(n_act, (1, LANES))
        meta_ref[4:5, :] = tend
        meta_ref[5:8, :] = jnp.zeros((3, LANES), F32)
        cnt[...] = jnp.zeros_like(cnt)

    @pl.when(p == 1)
    def _():
        ri = lax.broadcasted_iota(I32, (tm, tm), 0)
        ci = lax.broadcasted_iota(I32, (tm, tm), 1)
        ltri = jnp.where(ci < ri, 1.0, 0.0).astype(BF16)
        cum = _dot(ltri, a.astype(BF16)) + cnt[0:1, :] + off[0:1, :]
        pos1 = jnp.sum(a1 * cum, axis=-1, keepdims=True)
        pos2 = jnp.sum(a2 * cum, axis=-1, keepdims=True)
        cnt[0:1, :] = cnt[0:1, :] + jnp.sum(a, axis=0, keepdims=True)
        rt = jnp.where(lane == 0, e1, 0.0)
        rt = jnp.where(lane == 1, e2, rt)
        rt = jnp.where(lane == 2, w1, rt)
        rt = jnp.where(lane == 3, w2, rt)
        rt = jnp.where(lane == 4, pos1, rt)
        rt = jnp.where(lane == 5, pos2, rt)
        rt_ref[...] = rt


def _route(logits, bias_all, l, tile):
    T, R = logits.shape
    tm = _pick(T, (640, 512, 384, 256, 128))
    return pl.pallas_call(
        functools.partial(_route_kernel, tm=tm, tile=tile),
        out_shape=[jax.ShapeDtypeStruct((T, R), F32), jax.ShapeDtypeStruct((SUBLANES, LANES), F32)],
        grid=(2, T // tm),
        in_specs=[pl.BlockSpec((tm, R), lambda p, m: (m, 0)), _layer_spec(bias_all, l)],
        out_specs=[pl.BlockSpec((tm, R), lambda p, m: (m * p, 0)),
                   pl.BlockSpec((SUBLANES, LANES), lambda p, m: (0, 0))],
        scratch_shapes=[pltpu.VMEM((SUBLANES, LANES), F32), pltpu.VMEM((SUBLANES, LANES), F32)],
        compiler_params=_cparams(2),
        name="route",
    )(logits, bias_all)


def _dispatch_kernel(pos_ref, tend_ref, hp_ref, hs_ref, stage, zbuf, sem, zsem, *, T, tk, tile, n_tiles, rpt):
    i = pl.program_id(0)
    n = pl.num_programs(0)
    trows = tile * rpt

    @pl.when(i == 0)
    def _():
        zbuf[...] = jnp.zeros_like(zbuf)

        def last_tile_copy(ex):
            end = tend_ref[ex]
            start = tend_ref[ex - 1] if ex > 0 else 0
            row0 = pl.multiple_of((end - 1) * trows, trows)
            return end > start, pltpu.make_async_copy(zbuf, hs_ref.at[pl.ds(row0, trows)], zsem)

        def spare_tile_copy(k):
            idx = tend_ref[N_EXPERTS - 1] + k
            row0 = pl.multiple_of(jnp.minimum(idx, n_tiles - 1) * trows, trows)
            return idx < n_tiles, pltpu.make_async_copy(zbuf, hs_ref.at[pl.ds(row0, trows)], zsem)

        fills = [last_tile_copy(ex) for ex in range(N_EXPERTS)] + [spare_tile_copy(k) for k in range(N_EXPERTS)]
        for go, cp in fills:
            @pl.when(go)
            def _(cp=cp):
                cp.start()

        for go, cp in fills:
            @pl.when(go)
            def _(cp=cp):
                cp.wait()

    def copies(step, slot):
        out = []
        for u in range(tk):
            t = step * tk + u
            for k in range(2):
                p = pl.multiple_of(pos_ref[k * T + t] * rpt, rpt)
                out.append(pltpu.make_async_copy(stage.at[slot, pl.ds(u * rpt, rpt)], hs_ref.at[pl.ds(p, rpt)],
                                                 sem.at[slot]))
        return out

    def wait_slot(s):
        rows = hs_ref.at[pl.ds(0, 2 * tk * rpt)]
        pltpu.make_async_copy(rows, rows, sem.at[s]).wait()

    slot = lax.rem(i, 2)
    for s in range(2):
        @pl.when(slot == s)
        def _(s=s):
            stage[s] = hp_ref[...]
            for idx, cp in enumerate(copies(i, s)):
                cp.start(priority=idx % 2)

        @pl.when((slot == 1 - s) & (i > 0))
        def _(s=s):
            wait_slot(s)

        @pl.when((slot == s) & (i == n - 1))
        def _(s=s):
            wait_slot(s)


def _dispatch(pos_flat, tend, hp, n_pad, tile, rpt):
    T = hp.shape[0] // rpt
    tk = 128
    return pl.pallas_call(
        functools.partial(_dispatch_kernel, T=T, tk=tk, tile=tile, n_tiles=n_pad // tile, rpt=rpt),
        out_shape=jax.ShapeDtypeStruct((n_pad * rpt, LANES), U32),
        grid_spec=pltpu.PrefetchScalarGridSpec(
            num_scalar_prefetch=2,
            grid=(T // tk,),
            in_specs=[pl.BlockSpec((tk * rpt, LANES), lambda i, pos, tend: (i, 0))],
            out_specs=pl.BlockSpec(memory_space=pl.ANY),
            scratch_shapes=[pltpu.VMEM((2, tk * rpt, LANES), U32), pltpu.VMEM((tile * rpt, LANES), U32),
                            pltpu.SemaphoreType.DMA((2,)), pltpu.SemaphoreType.DMA],
        ),
        compiler_params=_cparams(1),
        name="dispatch",
    )(pos_flat, tend, hp)


def _expert_kernel(te_ref, na_ref, hs_ref, wg_ref, wu_ref, wd_ref, ys_ref,
                   wgst, wust, wdst, wgbf, wubf, wdbf, sem, *, l, tile):
    i = pl.program_id(0)
    na = na_ref[0]
    e = te_ref[i]
    first = (i < na) & ((i == 0) | (e != te_ref[jnp.maximum(i - 1, 0)]))

    def weight_copies(ex):
        return (pltpu.make_async_copy(wg_ref.at[l, ex], wgst, sem),
                pltpu.make_async_copy(wu_ref.at[l, ex], wust, sem),
                pltpu.make_async_copy(wd_ref.at[l, ex], wdst, sem))

    @pl.when(i == 0)
    def _():
        for cp in weight_copies(e):
            cp.start()

    @pl.when(first)
    def _():
        for cp in weight_copies(e):
            cp.wait()
        _cast_rows(wgst, wgbf, 256)
        _cast_rows(wust, wubf, 256)
        _cast_rows(wdst, wdbf, 256)
        j = lax.while_loop(lambda j: (j < na) & (te_ref[jnp.minimum(j, na - 1)] == e), lambda j: j + 1, i + 1)

        @pl.when(j < na)
        def _():
            for cp in weight_copies(te_ref[jnp.minimum(j, na - 1)]):
                cp.start()

    @pl.when(i < na)
    def _():
        rin = hs_ref.shape[0] // tile
        w = jnp.concatenate([hs_ref[pl.ds(j, tile, stride=rin), :] for j in range(rin)], axis=1)
        half = w.shape[1]
        lo = lax.bitcast_convert_type(w << 16, F32).astype(BF16)
        hi = lax.bitcast_convert_type(w & jnp.uint32(0xFFFF0000), F32).astype(BF16)
        a = _dot(lo, wgbf[0:half, :]) + _dot(hi, wgbf[half:, :])
        b = _dot(lo, wubf[0:half, :]) + _dot(hi, wubf[half:, :])
        hid = (a * _sigmoid(a) * b).astype(BF16)
        ys_ref[...] = _dot(hid, wdbf[...])

    @pl.when(i >= na)
    def _():
        ys_ref[...] = jnp.zeros_like(ys_ref)


def _experts(te, na, hs, wg, wu, wd, l, tile, n_tiles_max):
    D, F = wg.shape[-2:]
    rin = D // 2 // LANES
    n_pad = hs.shape[0] // rin

    def row_map(i, te_ref, na_ref):
        return (jnp.minimum(i, na_ref[0] - 1), 0)

    any_spec = pl.BlockSpec(memory_space=pl.ANY)
    return pl.pallas_call(
        functools.partial(_expert_kernel, l=l, tile=tile),
        out_shape=jax.ShapeDtypeStruct((n_pad, D), F32),
        grid_spec=pltpu.PrefetchScalarGridSpec(
            num_scalar_prefetch=2,
            grid=(n_tiles_max,),
            in_specs=[pl.BlockSpec((tile * rin, LANES), row_map), any_spec, any_spec, any_spec],
            out_specs=pl.BlockSpec((tile, D), lambda i, te_ref, na_ref: (i, 0)),
            scratch_shapes=[pltpu.VMEM((D, F), F32), pltpu.VMEM((D, F), F32), pltpu.VMEM((F, D), F32),
                            pltpu.VMEM((D, F), BF16), pltpu.VMEM((D, F), BF16), pltpu.VMEM((F, D), BF16),
                            pltpu.SemaphoreType.DMA],
        ),
        compiler_params=_cparams(1),
        name="experts",
    )(te, na, hs, wg, wu, wd)


def _combine_kernel(pos_ref, x1_ref, rt_ref, g_ref, ys_ref, *rest, T, tk, n_prompt):
    final = n_prompt is not None
    out_a, out_b, buf, sem = rest
    i = pl.program_id(0)
    n = pl.num_programs(0)

    def copies(step, slot):
        out = []
        for u in range(tk):
            t = step * tk + u
            for k in range(2):
                p = pos_ref[k * T + t]
                out.append(pltpu.make_async_copy(ys_ref.at[pl.ds(p, 1)], buf.at[slot, k, pl.ds(u, 1)],
                                                 sem.at[slot]))
        return out

    slot = lax.rem(i, 2)

    @pl.when(i == 0)
    def _():
        for idx, cp in enumerate(copies(0, 0)):
            cp.start(priority=idx % 2)

    for s in range(2):
        @pl.when((i + 1 < n) & (slot == 1 - s))
        def _(s=s):
            for idx, cp in enumerate(copies(i + 1, s)):
                cp.start(priority=idx % 2)

    for s in range(2):
        @pl.when(slot == s)
        def _(s=s):
            pltpu.make_async_copy(buf.at[s], buf.at[s], sem.at[s]).wait()
            rt = rt_ref[...]
            w1 = rt[:, 2:3]
            w2 = rt[:, 3:4]
            x2 = x1_ref[...] + w1 * buf[s, 0] + w2 * buf[s, 1]
            hn = _rms_scale(x2, g_ref[...])
            if not final:
                out_a[...] = x2
                out_b[...] = hn.astype(out_b.dtype)
            else:
                @pl.when(i < n_prompt)
                def _():
                    out_a[...] = hn

                @pl.when(i >= n_prompt)
                def _():
                    out_b[...] = hn


def _combine(pos_flat, x1, route, g_all, gl, ys, n_sample=None):
    T, D = x1.shape
    tk = 128
    R = route.shape[1]
    if n_sample is None:
        n_prompt = None
        g_spec = pl.BlockSpec((None, 1, D), lambda i, pos: (gl, 0, 0))
        out_shape = [jax.ShapeDtypeStruct((T, D), F32), jax.ShapeDtypeStruct((T, D), BF16)]
        out_specs = [pl.BlockSpec((tk, D), lambda i, pos: (i, 0)), pl.BlockSpec((tk, D), lambda i, pos: (i, 0))]
    else:
        assert n_sample == tk
        n_prompt = (T - n_sample) // tk
        g_spec = pl.BlockSpec((1, D), lambda i, pos: (0, 0))
        out_shape = [jax.ShapeDtypeStruct((T - n_sample, D), F32), jax.ShapeDtypeStruct((n_sample, D), F32)]
        out_specs = [pl.BlockSpec((tk, D), lambda i, pos: (jnp.minimum(i, n_prompt - 1), 0)),
                     pl.BlockSpec((tk, D), lambda i, pos: (0, 0))]
    return pl.pallas_call(
        functools.partial(_combine_kernel, T=T, tk=tk, n_prompt=n_prompt),
        out_shape=out_shape,
        grid_spec=pltpu.PrefetchScalarGridSpec(
            num_scalar_prefetch=1,
            grid=(T // tk,),
            in_specs=[pl.BlockSpec((tk, D), lambda i, pos: (i, 0)),
                      pl.BlockSpec((tk, R), lambda i, pos: (i, 0)),
                      g_spec,
                      pl.BlockSpec(memory_space=pl.ANY)],
            out_specs=out_specs,
            scratch_shapes=[pltpu.VMEM((2, 2, tk, D), F32), pltpu.SemaphoreType.DMA((2,))],
        ),
        compiler_params=_cparams(1),
        name="combine",
    )(pos_flat, x1, route, g_all, ys)


def kernel(x_prompt, x_sample, state_conv, state_pool, state_ssm_re, state_ssm_im,
           norm_mix_g, norm_ffn_g, w_in,
           sgu_ln_g, sgu_ln_b, sgu_w, sgu_b,
           conv_w, conv_b, conv_ln_g, conv_ln_b,
           ssm_a_re, ssm_a_im, ssm_log_dt, ssm_b_re, ssm_b_im, ssm_c_re, ssm_c_im, ssm_d, ssm_w_glu, ssm_b_glu,
           pool_w, pool_scale,
           w_branch, w_out,
           router_group_w, router_group_b, router_expert_w, router_expert_b,
           expert_w_gate, expert_w_up, expert_w_down,
           final_norm_g):
    B, L, D = x_prompt.shape
    DB = x_sample.shape[0]
    depth = w_in.shape[0]
    W = sgu_ln_g.shape[-1]
    G, P = ssm_a_re.shape[1:]
    GP = G * P
    T = B * L + DB
    gate_off = 6 * W
    assert x_sample.shape[1] == 1 and L % TIME_TILE == 0 and (B * L) % DB == 0
    assert all(w & (w - 1) == 0 and w <= 16 for w in POOL_WINDOWS)

    row = lambda a: a.reshape(depth, 1, a.shape[-1])
    small = {
        "sgu_ln_g": row(sgu_ln_g), "sgu_ln_b": row(sgu_ln_b),
        "sgu_bT": jnp.swapaxes(sgu_b, 1, 2),
        "sgu_w00": row(jnp.repeat(sgu_w[:, :, 0, 0], W // SGU_GROUPS, axis=-1)),
        "sgu_b0": row(jnp.repeat(sgu_b[:, :, 0], W // SGU_GROUPS, axis=-1)),
        "conv_w": conv_w, "conv_w8": jnp.repeat(conv_w, SUBLANES, axis=1), "conv_b": row(conv_b), "conv_ln_g": row(conv_ln_g), "conv_ln_b": row(conv_ln_b),
        "ssm_d": row(ssm_d), "ssm_b_glu": row(ssm_b_glu), "pool_scale": row(pool_scale),
    }
    S = TIME_TILE // SUBLANES
    lamr, lami, pwr, pwi, bbr, bbi, cr_bd, ci_bd, sguw_bf, wglu_bf, poolw_bf = _prep(
        ssm_a_re, ssm_a_im, ssm_log_dt, ssm_b_re, ssm_b_im, ssm_c_re, ssm_c_im, sgu_w, ssm_w_glu, pool_w, S)
    prep = {"lamr": lamr, "lami": lami, "pwr": pwr, "pwi": pwi, "bbr": bbr, "bbi": bbi,
            "sguw": sguw_bf, "wglu": wglu_bf, "poolw": poolw_bf}

    wr = jnp.concatenate([router_group_w,
                          jnp.transpose(router_expert_w, (0, 2, 1, 3)).reshape(depth, D, N_EXPERTS)], axis=-1)
    wr = jnp.pad(wr, ((0, 0), (0, 0), (0, LANES - wr.shape[-1])))
    rb = jnp.concatenate([router_group_b, router_expert_b.reshape(depth, N_EXPERTS)], axis=-1)
    rb = jnp.pad(rb, ((0, 0), (0, LANES - rb.shape[-1]))).reshape(depth, 1, LANES)

    norm_mix3 = row(norm_mix_g)
    norm_ffn3 = row(norm_ffn_g)
    final3 = final_norm_g.reshape(1, D)

    n_tiles_max = -(-2 * T // EXPERT_TILE) + N_EXPERTS
    n_pad = n_tiles_max * EXPERT_TILE

    x = jnp.concatenate([x_prompt.reshape(B * L, D), x_sample.reshape(DB, D)], axis=0)
    h = _rmsnorm(x, norm_mix3, 0, BF16)

    stc2 = state_conv.reshape(depth, DB, (CONV_K - 1) * W)
    stp2 = state_pool.reshape(depth, DB, POOL_HIST * W)
    hre2 = state_ssm_re.reshape(depth, DB, GP)
    him2 = state_ssm_im.reshape(depth, DB, GP)

    conv_p, conv_s, pool_p, pool_s = [], [], [], []
    sre_p, sim_p, sre_s, sim_s, v_s = [], [], [], [], []
    y_p = y_s = None
    for l in range(depth):
        o, cp, pp, rp, ip = _mixer_prompt(h, w_in, B, L, T, l, small, prep, cr_bd, ci_bd)
        o, g_s, vs, d_in_s, xs_re, xs_im = _mixer_sample(h, w_in, o, stc2, stp2, hre2, him2, DB, T, l,
                                                         small, prep, cr_bd, ci_bd)
        merged = _merge(h, o, w_in, w_branch, l, gate_off)
        x1, hp, logits = _out_proj(merged, x, w_out, norm_ffn3, wr, l)
        route, meta = _route(logits, rb, l, EXPERT_TILE)
        pos_flat = jnp.transpose(route[:, 4:6]).astype(I32).reshape(2 * T)
        te = meta[2, :n_tiles_max].astype(I32)
        na = meta[3, :1].astype(I32)
        tend = meta[4, :N_EXPERTS].astype(I32)
        hs = _dispatch(pos_flat, tend, hp, n_pad, EXPERT_TILE, D // 2 // LANES)
        ys = _experts(te, na, hs, expert_w_gate, expert_w_up, expert_w_down, l, EXPERT_TILE, n_tiles_max)
        if l + 1 < depth:
            x, h = _combine(pos_flat, x1, route, norm_mix3, l + 1, ys)
        else:
            y_p, y_s = _combine(pos_flat, x1, route, final3, 0, ys, n_sample=DB)

        conv_p.append(cp)
        pool_p.append(pp)
        sre_p.append(rp.reshape(B, G, P))
        sim_p.append(ip.reshape(B, G, P))
        conv_s.append(jnp.concatenate([state_conv[l][:, 1:], g_s[:, None, :]], axis=1))
        pool_s.append(jnp.concatenate([state_pool[l][:, 1:], d_in_s[:, None, :]], axis=1))
        sre_s.append(xs_re.reshape(DB, G, P))
        sim_s.append(xs_im.reshape(DB, G, P))
        v_s.append(vs[:, None, :])

    y_prompt = y_p.reshape(B, L, D)
    y_sample = y_s.reshape(DB, 1, D)
    return (y_prompt, y_sample, jnp.stack(conv_p), jnp.stack(conv_s), jnp.stack(pool_p), jnp.stack(pool_s),
            jnp.stack(sre_p), jnp.stack(sim_p), jnp.stack(sre_s), jnp.stack(sim_s), jnp.stack(v_s))
```

```python
import functools
import math

import jax
import jax.numpy as jnp
from jax import lax
from jax.experimental import pallas as pl
from jax.experimental.pallas import tpu as pltpu

F32 = jnp.float32
BF16 = jnp.bfloat16
I32 = jnp.int32
U32 = jnp.uint32

EPS = 1e-6
CHUNK = 128
SGU_GROUPS = 4
CONV_K = 31
POOL_WINDOWS = (2, 4, 8, 16)
POOL_HIST = 15
SSM_H = 16
SSM_P = 64
N_GROUPS = 4
EXP_PER_GROUP = 4
N_EXPERTS = 16
PAST_LEN = 16384

LANES = 128
SUBLANES = 8
VMEM_LIMIT = 60 * 1024 * 1024

EXPERT_TILE = 256
TIME_TILE = 256
SCAN_COLS = 512


def _cparams(n_axes):
    return pltpu.CompilerParams(dimension_semantics=("arbitrary",) * n_axes,
                                vmem_limit_bytes=VMEM_LIMIT)


def _gelu(x):
    c = math.sqrt(2.0 / math.pi)
    return 0.5 * x * (1.0 + jnp.tanh(c * (x + 0.044715 * (x * x * x))))


def _sigmoid(x):
    return 0.5 * jnp.tanh(0.5 * x) + 0.5


def _layernorm(x, g, b):
    xc = x - jnp.mean(x, axis=-1, keepdims=True)
    var = jnp.mean(xc * xc, axis=-1, keepdims=True)
    return xc * lax.rsqrt(var + EPS) * g + b


def _rms_scale(x, g):
    return x * lax.rsqrt(jnp.mean(x * x, axis=-1, keepdims=True) + EPS) * g


def _dot(a, b):
    return jnp.dot(a, b, preferred_element_type=F32)


def _cast_rows(src_ref, dst_ref, chunk):
    rows = src_ref.shape[0]

    def body(i, c):
        r = pl.multiple_of(i * chunk, chunk)
        dst_ref[pl.ds(r, chunk), :] = src_ref[pl.ds(r, chunk), :].astype(dst_ref.dtype)
        return c

    lax.fori_loop(0, rows // chunk, body, 0)


def _layer_spec(arr, l, single=False):
    nd = arr.ndim
    mode = {"pipeline_mode": pl.Buffered(1)} if single else {}
    return pl.BlockSpec((None,) + tuple(arr.shape[1:]), lambda *_: (l,) + (0,) * (nd - 1), **mode)


def _pick(n, cands):
    for c in cands:
        if n % c == 0:
            return c
    raise ValueError(f"no tile for {n}")


def _rmsnorm_kernel(x_ref, g_ref, o_ref):
    o_ref[...] = _rms_scale(x_ref[...], g_ref[...]).astype(o_ref.dtype)


def _rmsnorm(x, g_all, l, out_dtype):
    T, D = x.shape
    tm = _pick(T, (640, 512, 384, 256, 128))
    return pl.pallas_call(
        _rmsnorm_kernel,
        out_shape=jax.ShapeDtypeStruct((T, D), out_dtype),
        grid=(T // tm,),
        in_specs=[pl.BlockSpec((tm, D), lambda m: (m, 0)), _layer_spec(g_all, l)],
        out_specs=pl.BlockSpec((tm, D), lambda m: (m, 0)),
        compiler_params=_cparams(1),
        name="rmsnorm",
    )(x, g_all)


def _split3(x):
    p1 = x.astype(BF16)
    r1 = x - p1.astype(F32)
    p2 = r1.astype(BF16)
    p3 = (r1 - p2.astype(F32)).astype(BF16)
    return p1, p2, p3


def _prep_kernel(are, aim, ldt, bre, bim, cre, cim, sguw, wglu, poolw,
                 lamr_o, lami_o, pwr_o, pwi_o, bbr_o, bbi_o, cr_o, ci_o, sguw_o, wglu_o, poolw_o, *, S, P, H):
    a_re = are[...]
    a_im = aim[...]
    dt = jnp.exp(ldt[...])
    mag = jnp.exp(a_re * dt)
    lbr = mag * jnp.cos(a_im * dt)
    lbi = mag * jnp.sin(a_im * dt)
    den = a_re * a_re + a_im * a_im
    nr = lbr - 1.0
    kr = (nr * a_re + lbi * a_im) / den
    ki = (lbi * a_re - nr * a_im) / den
    lamr_o[...] = lbr
    lami_o[...] = lbi
    gp = a_re.shape[-1]

    w = bre.shape[0]
    sh_p, sh_h = P.bit_length() - 1, H.bit_length() - 1
    rc = 128
    pe = lax.broadcasted_iota(I32, (P, gp), 0)
    ce = lax.broadcasted_iota(I32, (P, gp), 1)
    rep_p = jnp.where((ce & (P - 1)) == pe, 1.0, 0.0).astype(BF16)
    for i in range(w // rc):
        rows = slice(i * rc, (i + 1) * rc)
        r_i = lax.broadcasted_iota(I32, (rc, gp), 0) + i * rc
        c_i = lax.broadcasted_iota(I32, (rc, gp), 1)
        diag = (r_i >> sh_h) == (c_i >> sh_p)
        br = sum(_dot(piece, rep_p) for piece in _split3(bre[rows, :]))
        bi = sum(_dot(piece, rep_p) for piece in _split3(bim[rows, :]))
        bbr_o[rows, :] = jnp.where(diag, kr * br - ki * bi, 0.0).astype(BF16)
        bbi_o[rows, :] = jnp.where(diag, kr * bi + ki * br, 0.0).astype(BF16)

    he = lax.broadcasted_iota(I32, (H, w), 0)
    ce = lax.broadcasted_iota(I32, (H, w), 1)
    rep_h = jnp.where((ce & (H - 1)) == he, 1.0, 0.0).astype(BF16)
    rc = 256
    for i in range(gp // rc):
        rows = slice(i * rc, (i + 1) * rc)
        r_i = lax.broadcasted_iota(I32, (rc, w), 0) + i * rc
        c_i = lax.broadcasted_iota(I32, (rc, w), 1)
        diag = (r_i >> sh_p) == (c_i >> sh_h)
        cr_o[rows, :] = jnp.where(diag, _dot(cre[rows, :].astype(BF16), rep_h), 0.0).astype(BF16)
        ci_o[rows, :] = jnp.where(diag, _dot(cim[rows, :].astype(BF16), rep_h), 0.0).astype(BF16)

    pr, pi = lbr, lbi
    for s in range(S):
        pwr_o[SUBLANES * s:SUBLANES * (s + 1), :] = jnp.broadcast_to(pr, (SUBLANES, gp))
        pwi_o[SUBLANES * s:SUBLANES * (s + 1), :] = jnp.broadcast_to(pi, (SUBLANES, gp))
        pr, pi = pr * lbr - pi * lbi, pr * lbi + pi * lbr

    t_i = lax.broadcasted_iota(I32, (CHUNK, CHUNK), 0)
    s_i = lax.broadcasted_iota(I32, (CHUNK, CHUNK), 1)
    for g in range(SGU_GROUPS):
        sguw_o[g] = jnp.where(t_i >= s_i, sguw[g], 0.0).astype(BF16)
    wglu_o[...] = wglu[...].astype(BF16)
    for g in range(len(POOL_WINDOWS)):
        poolw_o[g] = poolw[g].astype(BF16)


def _prep(a_re, a_im, log_dt, b_re, b_im, c_re, c_im, sgu_w, w_glu, pool_w, S):
    L, G, P = a_re.shape
    H = b_re.shape[-1]
    GP, W = G * P, G * H
    assert P & (P - 1) == 0 and H & (H - 1) == 0
    bre_t = jnp.swapaxes(b_re, 2, 3).reshape(L, W, P)
    bim_t = jnp.swapaxes(b_im, 2, 3).reshape(L, W, P)
    cre_t = jnp.swapaxes(c_re, 2, 3).reshape(L, GP, H)
    cim_t = jnp.swapaxes(c_im, 2, 3).reshape(L, GP, H)
    are2 = a_re.reshape(L, 1, GP)
    aim2 = a_im.reshape(L, 1, GP)
    ldt2 = jnp.repeat(log_dt, P, axis=-1).reshape(L, 1, GP)

    def lspec(shape):
        nd = len(shape)
        return pl.BlockSpec((None,) + tuple(shape[1:]), lambda l: (l,) + (0,) * (nd - 1))

    ins = [are2, aim2, ldt2, bre_t, bim_t, cre_t, cim_t, sgu_w, w_glu, pool_w]
    out_shapes = [
        jax.ShapeDtypeStruct((L, 1, GP), F32), jax.ShapeDtypeStruct((L, 1, GP), F32),
        jax.ShapeDtypeStruct((L, SUBLANES * S, GP), F32), jax.ShapeDtypeStruct((L, SUBLANES * S, GP), F32),
        jax.ShapeDtypeStruct((L, W, GP), BF16), jax.ShapeDtypeStruct((L, W, GP), BF16),
        jax.ShapeDtypeStruct((L, GP, W), BF16), jax.ShapeDtypeStruct((L, GP, W), BF16),
        jax.ShapeDtypeStruct(sgu_w.shape, BF16), jax.ShapeDtypeStruct(w_glu.shape, BF16),
        jax.ShapeDtypeStruct(pool_w.shape, BF16),
    ]
    return pl.pallas_call(
        functools.partial(_prep_kernel, S=S, P=P, H=H),
        out_shape=out_shapes,
        grid=(L,),
        in_specs=[lspec(a.shape) for a in ins],
        out_specs=[lspec(o.shape) for o in out_shapes],
        compiler_params=_cparams(1),
        name="ssm_prep",
    )(*ins)


PROJ_CHUNK = 256


def _stream_weight_cols(w_hbm, l, n_cols, stage, sem, consume):
    def copy(c, slot):
        return pltpu.make_async_copy(w_hbm.at[l, :, pl.ds(c * PROJ_CHUNK, PROJ_CHUNK)], stage.at[slot],
                                     sem.at[slot])

    n = n_cols // PROJ_CHUNK
    copy(0, 0).start()
    for c in range(n):
        slot = c % 2
        if c + 1 < n:
            copy(c + 1, 1 - slot).start()
        copy(c, slot).wait()
        consume(c, slot)


N_MIXER_INPUTS = 23


def _mixer_prompt_kernel(*refs, n_batch, **kw):
    o_ref = refs[N_MIXER_INPUTS]
    b = pl.program_id(0)

    @pl.when(b < n_batch)
    def _():
        _mixer_prompt_body(*refs, **kw)

    @pl.when((b == n_batch) & (pl.program_id(1) == 0))
    def _():
        o_ref[...] = jnp.zeros_like(o_ref)


def _mixer_prompt_body(h_ref, w_hbm, lng, lnb, sguw, sgub, cw, cb, clg, clb,
                       lamr, lami, pwr, pwi, bbr, bbi, cr, ci, sd, wglu, bglu, poolw, pscale,
                       o_ref, conv_o, pool_o, sre_o, sim_o,
                       w6bf, wstage, wsem, p_ref,
                       gbuf, gsh, pbuf, plvl, xr, xi, car_re, car_im, fin_re, fin_im, cm_re, cm_im,
                       *, l, tt, S, W, GP):
    j = pl.program_id(1)
    nt = pl.num_programs(1)
    GH = 32
    PH = 32

    @pl.when((pl.program_id(0) == 0) & (j == 0))
    def _():
        def consume(c, slot):
            _cast_rows(wstage.at[slot], w6bf.at[:, c * PROJ_CHUNK:(c + 1) * PROJ_CHUNK], 256)

        _stream_weight_cols(w_hbm, l, 6 * W, wstage, wsem, consume)

    @pl.when(j == 0)
    def _():
        gbuf[0:GH, :] = jnp.zeros((GH, W), F32)
        pbuf[0:PH, :] = jnp.zeros((PH, W), F32)
        car_re[...] = jnp.zeros_like(car_re)
        car_im[...] = jnp.zeros_like(car_im)

    h = h_ref[...]
    for c in range(3):
        p_ref[:, 2 * c * W:2 * (c + 1) * W] = _dot(h, w6bf[:, 2 * c * W:2 * (c + 1) * W])

    for c in range(tt // CHUNK):
        rows = slice(c * CHUNK, (c + 1) * CHUNK)
        u = _gelu(p_ref[rows, 0:W])
        v = _layernorm(_gelu(p_ref[rows, W:2 * W]), lng[...], lnb[...])
        vb = v.astype(BF16)
        gw = W // SGU_GROUPS
        for g in range(SGU_GROUPS):
            cols = slice(g * gw, (g + 1) * gw)
            mixed = _dot(sguw[g], vb[:, cols]) + sgub[:, g:g + 1]
            o_ref[rows, cols] = (u[:, cols] * mixed).astype(BF16)

    rc = 64
    for c in range(tt // rc):
        rows = slice(c * rc, (c + 1) * rc)
        gbuf[GH + c * rc:GH + (c + 1) * rc, :] = p_ref[rows, 2 * W:3 * W] * _sigmoid(p_ref[rows, 3 * W:4 * W])
    sh_rows = gsh.shape[1]
    for b in range(1, SUBLANES):
        gsh[b] = gbuf[b:b + sh_rows, :]
    rc = 32
    base = GH - (CONV_K - 1)
    for c in range(tt // rc):
        acc = jnp.zeros((rc, W), F32)
        for k in range(CONV_K):
            r0 = base + c * rc + k
            b, a0 = r0 % SUBLANES, r0 - r0 % SUBLANES
            rows = gbuf[a0:a0 + rc, :] if b == 0 else gsh[b, a0:a0 + rc, :]
            wk = jnp.concatenate([cw[SUBLANES * k:SUBLANES * (k + 1), :]] * (rc // SUBLANES), axis=0)
            acc = acc + wk * rows
        y = _layernorm(acc + cb[...], clg[...], clb[...])
        o_ref[c * rc:(c + 1) * rc, W:2 * W] = (y * _sigmoid(y)).astype(BF16)
    gbuf[0:GH, :] = gbuf[tt:tt + GH, :]

    uc = p_ref[:, 4 * W:5 * W]
    r_i = lax.broadcasted_iota(I32, (tt, tt), 0)
    c_i = lax.broadcasted_iota(I32, (tt, tt), 1)
    perm = jnp.where(((r_i & (SUBLANES - 1)) * S + (r_i >> 3)) == c_i, 1.0, 0.0).astype(BF16)
    up = _dot(perm, uc.astype(BF16)).astype(BF16)
    hw, hg = W // 2, GP // 2
    for q in range(2):
        xr[:, q * hg:(q + 1) * hg] = _dot(up[:, q * hw:(q + 1) * hw], bbr[q * hw:(q + 1) * hw, q * hg:(q + 1) * hg])
        xi[:, q * hg:(q + 1) * hg] = _dot(up[:, q * hw:(q + 1) * hw], bbi[q * hw:(q + 1) * hw, q * hg:(q + 1) * hg])

    for cbi in range(GP // SCAN_COLS):
        cols = slice(cbi * SCAN_COLS, (cbi + 1) * SCAN_COLS)
        lr = jnp.broadcast_to(lamr[:, cols], (SUBLANES, SCAN_COLS))
        li = jnp.broadcast_to(lami[:, cols], (SUBLANES, SCAN_COLS))

        def step(s, carry, cols=cols, lr=lr, li=li):
            sr, si = carry
            r0 = pl.multiple_of(s * SUBLANES, SUBLANES)
            nr = lr * sr - li * si + xr[pl.ds(r0, SUBLANES), cols]
            ni = lr * si + li * sr + xi[pl.ds(r0, SUBLANES), cols]
            xr[pl.ds(r0, SUBLANES), cols] = nr
            xi[pl.ds(r0, SUBLANES), cols] = ni
            return nr, ni

        z = jnp.zeros((SUBLANES, SCAN_COLS), F32)
        fr, fi = lax.fori_loop(0, S, step, (z, z), unroll=True)
        fin_re[:, cols] = fr
        fin_im[:, cols] = fi

    lsr = pwr[SUBLANES * (S - 1):SUBLANES * (S - 1) + 1, :]
    lsi = pwi[SUBLANES * (S - 1):SUBLANES * (S - 1) + 1, :]
    c_r = car_re[0:1, :]
    c_im = car_im[0:1, :]
    cm_re[0:1, :] = c_r
    cm_im[0:1, :] = c_im
    for q in range(1, SUBLANES):
        f_r = fin_re[q - 1:q, :]
        f_i = fin_im[q - 1:q, :]
        c_r, c_im = f_r + lsr * c_r - lsi * c_im, f_i + lsr * c_im + lsi * c_r
        cm_re[q:q + 1, :] = c_r
        cm_im[q:q + 1, :] = c_im
    n_r = fin_re[SUBLANES - 1:SUBLANES, :] + lsr * c_r - lsi * c_im
    n_i = fin_im[SUBLANES - 1:SUBLANES, :] + lsr * c_im + lsi * c_r
    car_re[0:1, :] = n_r
    car_im[0:1, :] = n_i

    for cbi in range(GP // SCAN_COLS):
        cols = slice(cbi * SCAN_COLS, (cbi + 1) * SCAN_COLS)
        mr = cm_re[:, cols]
        mi = cm_im[:, cols]

        def fix(s, c, cols=cols, mr=mr, mi=mi):
            r0 = pl.multiple_of(s * SUBLANES, SUBLANES)
            pr = pwr[pl.ds(r0, SUBLANES), cols]
            pi = pwi[pl.ds(r0, SUBLANES), cols]
            xr[pl.ds(r0, SUBLANES), cols] = xr[pl.ds(r0, SUBLANES), cols] + (pr * mr - pi * mi)
            xi[pl.ds(r0, SUBLANES), cols] = xi[pl.ds(r0, SUBLANES), cols] + (pr * mi + pi * mr)
            return c

        lax.fori_loop(0, S, fix, 0, unroll=True)

    yp = jnp.concatenate(
        [_dot(xr[:, q * hg:(q + 1) * hg].astype(BF16), cr[q * hg:(q + 1) * hg, q * hw:(q + 1) * hw])
         - _dot(xi[:, q * hg:(q + 1) * hg].astype(BF16), ci[q * hg:(q + 1) * hg, q * hw:(q + 1) * hw])
         for q in range(2)], axis=1)
    unperm = jnp.where(((c_i & (SUBLANES - 1)) * S + (c_i >> 3)) == r_i, 1.0, 0.0).astype(BF16)
    y1 = yp.astype(BF16)
    r1 = yp - y1.astype(F32)
    y2 = r1.astype(BF16)
    y3 = (r1 - y2.astype(F32)).astype(BF16)
    y = _dot(unperm, y1) + _dot(unperm, y2) + _dot(unperm, y3)
    z = _gelu(y + sd[...] * uc)
    gl = _dot(z.astype(BF16), wglu[...]) + bglu[...]
    o_ref[:, 2 * W:3 * W] = (z * _sigmoid(gl)).astype(BF16)

    xd = p_ref[:, 5 * W:6 * W]
    pbuf[PH:PH + tt, :] = xd
    gw = W // len(POOL_WINDOWS)
    pos = (j * tt + lax.broadcasted_iota(I32, (tt, 1), 0) + 1).astype(F32)
    li = 0
    for gi, win in enumerate(POOL_WINDOWS):
        cols = slice(gi * gw, (gi + 1) * gw)
        nlev = win.bit_length() - 1
        load = lambda r0, n, cols=cols: pbuf[r0:r0 + n, cols]
        for k in range(1, nlev):
            lo, sh = SUBLANES * k, 1 << (k - 1)
            n = PH + tt - lo
            plvl[li, lo:lo + n, :] = load(lo, n) + load(lo - sh, n)
            load = lambda r0, n, li=li: plvl[li, r0:r0 + n, :]
            li += 1
        s = load(PH, tt) + load(PH - (win >> 1), tt)
        cnt = jnp.minimum(pos, float(win))
        pooled = s / cnt - xd[:, cols]
        mixed = _dot(pooled.astype(BF16), poolw[gi]) * pscale[:, cols]
        o_ref[:, 3 * W + gi * gw:3 * W + (gi + 1) * gw] = mixed.astype(BF16)
    pbuf[0:PH, :] = pbuf[tt:tt + PH, :]

    @pl.when(j == nt - 1)
    def _():
        conv_o[...] = gbuf[GH - (CONV_K - 1):GH, :]
        pool_o[...] = pbuf[PH - POOL_HIST:PH, :]
        sre_o[...] = car_re[0:1, :]
        sim_o[...] = car_im[0:1, :]


def _mixer_prompt(h, w_in, B, L, T, l, small, prep, cr, ci):
    W = small["sgu_ln_g"].shape[-1]
    GP = prep["lamr"].shape[-1]
    D = h.shape[1]
    D4 = 4 * W
    tt = TIME_TILE
    S = tt // SUBLANES
    nt = L // tt
    ins = [h, w_in,
           small["sgu_ln_g"], small["sgu_ln_b"], prep["sguw"], small["sgu_bT"],
           small["conv_w8"], small["conv_b"], small["conv_ln_g"], small["conv_ln_b"],
           prep["lamr"], prep["lami"], prep["pwr"], prep["pwi"], prep["bbr"], prep["bbi"], cr, ci,
           small["ssm_d"], prep["wglu"], small["ssm_b_glu"], prep["poolw"], small["pool_scale"]]
    assert len(ins) == N_MIXER_INPUTS
    in_specs = [pl.BlockSpec((tt, D), lambda b, j: (jnp.minimum(b * nt + j, B * nt), 0)),
                pl.BlockSpec(memory_space=pl.ANY)]
    in_specs += [_layer_spec(a, l, single=True) for a in ins[2:]]
    out_shapes = [
        jax.ShapeDtypeStruct((T, D4), BF16),
        jax.ShapeDtypeStruct((B, CONV_K - 1, W), F32),
        jax.ShapeDtypeStruct((B, POOL_HIST, W), F32),
        jax.ShapeDtypeStruct((B, 1, GP), F32),
        jax.ShapeDtypeStruct((B, 1, GP), F32),
    ]
    out_specs = [
        pl.BlockSpec((tt, D4), lambda b, j: (jnp.minimum(b * nt + j, B * nt), 0)),
        pl.BlockSpec((None, CONV_K - 1, W), lambda b, j: (jnp.minimum(b, B - 1), 0, 0)),
        pl.BlockSpec((None, POOL_HIST, W), lambda b, j: (jnp.minimum(b, B - 1), 0, 0)),
        pl.BlockSpec((None, 1, GP), lambda b, j: (jnp.minimum(b, B - 1), 0, 0)),
        pl.BlockSpec((None, 1, GP), lambda b, j: (jnp.minimum(b, B - 1), 0, 0)),
    ]
    scratch = [
        pltpu.VMEM((D, 6 * W), BF16), pltpu.VMEM((2, D, PROJ_CHUNK), F32), pltpu.SemaphoreType.DMA((2,)),
        pltpu.VMEM((tt, 6 * W), F32),
        pltpu.VMEM((32 + tt, W), F32), pltpu.VMEM((SUBLANES, 32 + tt - SUBLANES, W), F32),
        pltpu.VMEM((32 + tt, W), F32), pltpu.VMEM((6, 32 + tt, W // len(POOL_WINDOWS)), F32),
        pltpu.VMEM((tt, GP), F32), pltpu.VMEM((tt, GP), F32),
        pltpu.VMEM((SUBLANES, GP), F32), pltpu.VMEM((SUBLANES, GP), F32),
        pltpu.VMEM((SUBLANES, GP), F32), pltpu.VMEM((SUBLANES, GP), F32),
        pltpu.VMEM((SUBLANES, GP), F32), pltpu.VMEM((SUBLANES, GP), F32),
    ]
    return pl.pallas_call(
        functools.partial(_mixer_prompt_kernel, n_batch=B, l=l, tt=tt, S=S, W=W, GP=GP),
        out_shape=out_shapes,
        grid=(B + 1, nt),
        in_specs=in_specs,
        out_specs=out_specs,
        scratch_shapes=scratch,
        compiler_params=_cparams(2),
        name="mixer_prompt",
    )(*ins)


def _mixer_sample_kernel(h_ref, w_hbm, stc, stp, hre, him, lng, lnb, w00, b0, cw, cb, clg, clb,
                         lamr, lami, bbr, bbi, cr, ci, sd, wglu, bglu, poolw, pscale, o_in,
                         o_ref, g_o, v_o, d_o, xre_o, xim_o,
                         p_ref, wstage, wchunk, wsem, *, l, W):
    del o_in
    h = h_ref[...]

    def consume(c, slot):
        _cast_rows(wstage.at[slot], wchunk, 256)
        p_ref[:, c * PROJ_CHUNK:(c + 1) * PROJ_CHUNK] = _dot(h, wchunk[...])

    _stream_weight_cols(w_hbm, l, 6 * W, wstage, wsem, consume)

    u = _gelu(p_ref[:, 0:W])
    v = _layernorm(_gelu(p_ref[:, W:2 * W]), lng[...], lnb[...])
    v_o[...] = v
    o_ref[:, 0:W] = (u * (w00[...] * v + b0[...])).astype(BF16)

    g = p_ref[:, 2 * W:3 * W] * _sigmoid(p_ref[:, 3 * W:4 * W])
    g_o[...] = g
    acc = cw[CONV_K - 1:CONV_K, :] * g
    for k in range(CONV_K - 1):
        acc = acc + cw[k:k + 1, :] * stc[:, k * W:(k + 1) * W]
    y = _layernorm(acc + cb[...], clg[...], clb[...])
    o_ref[:, W:2 * W] = (y * _sigmoid(y)).astype(BF16)

    uc = p_ref[:, 4 * W:5 * W]
    ub = uc.astype(BF16)
    h_r = hre[...]
    h_i = him[...]
    l_r = lamr[...]
    l_i = lami[...]
    x_r = l_r * h_r - l_i * h_i + _dot(ub, bbr[...])
    x_i = l_r * h_i + l_i * h_r + _dot(ub, bbi[...])
    xre_o[...] = x_r
    xim_o[...] = x_i
    yv = _dot(x_r.astype(BF16), cr[...]) - _dot(x_i.astype(BF16), ci[...]) + sd[...] * uc
    z = _gelu(yv)
    gl = _dot(z.astype(BF16), wglu[...]) + bglu[...]
    o_ref[:, 2 * W:3 * W] = (z * _sigmoid(gl)).astype(BF16)

    xd = p_ref[:, 5 * W:6 * W]
    d_o[...] = xd
    gw = W // len(POOL_WINDOWS)
    for gi, win in enumerate(POOL_WINDOWS):
        cols = slice(gi * gw, (gi + 1) * gw)
        s = xd[:, cols]
        for i in range(1, win):
            r = POOL_HIST - i
            s = s + stp[:, r * W + gi * gw:r * W + (gi + 1) * gw]
        cnt = float(min(PAST_LEN + 1, win))
        pooled = s / cnt - xd[:, cols]
        mixed = _dot(pooled.astype(BF16), poolw[gi]) * pscale[:, cols]
        o_ref[:, 3 * W + gi * gw:3 * W + (gi + 1) * gw] = mixed.astype(BF16)


def _mixer_sample(h, w_in, o_prev, stc2, stp2, hre2, him2, DB, T, l, small, prep, cr, ci):
    W = small["sgu_ln_g"].shape[-1]
    GP = prep["lamr"].shape[-1]
    D = h.shape[1]
    D4 = 4 * W
    blk = (T - DB) // DB
    ins = [h, w_in, stc2, stp2, hre2, him2,
           small["sgu_ln_g"], small["sgu_ln_b"], small["sgu_w00"], small["sgu_b0"],
           small["conv_w"], small["conv_b"], small["conv_ln_g"], small["conv_ln_b"],
           prep["lamr"], prep["lami"], prep["bbr"], prep["bbi"], cr, ci,
           small["ssm_d"], prep["wglu"], small["ssm_b_glu"], prep["poolw"], small["pool_scale"], o_prev]
    in_specs = [pl.BlockSpec((DB, D), lambda i: (blk, 0)), pl.BlockSpec(memory_space=pl.ANY)]
    in_specs += [_layer_spec(a, l, single=True) for a in ins[2:-1]]
    in_specs += [pl.BlockSpec(memory_space=pl.ANY)]
    out_shapes = [
        jax.ShapeDtypeStruct((T, D4), BF16),
        jax.ShapeDtypeStruct((DB, W), F32), jax.ShapeDtypeStruct((DB, W), F32), jax.ShapeDtypeStruct((DB, W), F32),
        jax.ShapeDtypeStruct((DB, GP), F32), jax.ShapeDtypeStruct((DB, GP), F32),
    ]
    out_specs = [
        pl.BlockSpec((DB, D4), lambda i: (blk, 0)),
        pl.BlockSpec((DB, W), lambda i: (0, 0)), pl.BlockSpec((DB, W), lambda i: (0, 0)),
        pl.BlockSpec((DB, W), lambda i: (0, 0)),
        pl.BlockSpec((DB, GP), lambda i: (0, 0)), pl.BlockSpec((DB, GP), lambda i: (0, 0)),
    ]
    return pl.pallas_call(
        functools.partial(_mixer_sample_kernel, l=l, W=W),
        out_shape=out_shapes,
        grid=(1,),
        in_specs=in_specs,
        out_specs=out_specs,
        scratch_shapes=[pltpu.VMEM((DB, 6 * W), F32), pltpu.VMEM((2, D, PROJ_CHUNK), F32),
                        pltpu.VMEM((D, PROJ_CHUNK), BF16), pltpu.SemaphoreType.DMA((2,))],
        input_output_aliases={len(ins) - 1: 0},
        compiler_params=_cparams(1),
        name="mixer_sample",
    )(*ins)


def _merge_kernel(h_ref, o_ref, wg0, wg1, wg2, wg3, wb_ref, m_ref, wgbf, wbbf, *, W):
    wgs = (wg0, wg1, wg2, wg3)

    @pl.when(pl.program_id(1) == 0)
    def _():
        for b in range(4):
            _cast_rows(wgs[b], wgbf.at[b], 256)
            wbbf[b] = wb_ref[b].astype(BF16)

    h = h_ref[...]
    acc = None
    for b in range(4):
        gate = _sigmoid(_dot(h, wgbf[b]))
        term = gate * _dot(o_ref[:, b * W:(b + 1) * W], wbbf[b])
        acc = term if acc is None else acc + term
    m_ref[...] = acc.astype(BF16)


def _merge(h, o, w_in, w_branch, l, gate_off):
    T, D = h.shape
    W = w_branch.shape[2]
    tm = _pick(T, (1040, 640, 512, 384, 256, 128))
    tn = 256
    nb = D // tn

    def gate_spec(b):
        blk0 = (gate_off + b * D) // tn
        return pl.BlockSpec((None, D, tn), lambda n, m: (l, 0, blk0 + n))

    return pl.pallas_call(
        functools.partial(_merge_kernel, W=W),
        out_shape=jax.ShapeDtypeStruct((T, D), BF16),
        grid=(nb, T // tm),
        in_specs=[pl.BlockSpec((tm, D), lambda n, m: (m, 0)),
                  pl.BlockSpec((tm, 4 * W), lambda n, m: (m, 0)),
                  gate_spec(0), gate_spec(1), gate_spec(2), gate_spec(3),
                  pl.BlockSpec((None, 4, W, tn), lambda n, m: (l, 0, 0, n))],
        out_specs=pl.BlockSpec((tm, tn), lambda n, m: (m, n)),
        scratch_shapes=[pltpu.VMEM((4, D, tn), BF16), pltpu.VMEM((4, W, tn), BF16)],
        compiler_params=_cparams(2),
        name="gated_merge",
    )(h, o, w_in, w_in, w_in, w_in, w_branch)


def _out_kernel(m_ref, x_ref, w_ref, g_ref, wr_ref, x1_ref, hp_ref, lg_ref, wbf, wr2):
    R = wr_ref.shape[-1]

    @pl.when(pl.program_id(0) == 0)
    def _():
        _cast_rows(w_ref, wbf, 256)
        wr = wr_ref[...]
        hi = wr.astype(BF16)
        wr2[:, 0:R] = hi
        wr2[:, R:2 * R] = (wr - hi.astype(F32)).astype(BF16)

    x1 = x_ref[...] + _dot(m_ref[...], wbf[...])
    x1_ref[...] = x1
    h = _rms_scale(x1, g_ref[...])
    hb = h.astype(BF16)
    hl = (h - hb.astype(F32)).astype(BF16)
    both = _dot(hb, wr2[...])
    lg_ref[...] = both[:, 0:R] + both[:, R:2 * R] + _dot(hl, wr2[:, 0:R])
    bits = lax.bitcast_convert_type(hb.astype(F32), U32)
    half = bits.shape[1] // 2
    word = (bits[:, :half] >> 16) | (bits[:, half:] & jnp.uint32(0xFFFF0000))
    tm = word.shape[0]
    for j in range(half // LANES):
        hp_ref[pl.ds(j, tm, stride=half // LANES), :] = word[:, j * LANES:(j + 1) * LANES]


def _out_proj(merged, x, w_out, g_all, wr_all, l):
    T, D = x.shape
    tm = _pick(T, (416, 320, 256, 128))
    R = wr_all.shape[-1]
    return pl.pallas_call(
        _out_kernel,
        out_shape=[jax.ShapeDtypeStruct((T, D), F32), jax.ShapeDtypeStruct((T * (D // 2 // LANES), LANES), U32),
                   jax.ShapeDtypeStruct((T, R), F32)],
        grid=(T // tm,),
        in_specs=[pl.BlockSpec((tm, D), lambda m: (m, 0)),
                  pl.BlockSpec((tm, D), lambda m: (m, 0)),
                  pl.BlockSpec((None, D, D), lambda m: (l, 0, 0), pipeline_mode=pl.Buffered(1)),
                  _layer_spec(g_all, l), _layer_spec(wr_all, l)],
        out_specs=[pl.BlockSpec((tm, D), lambda m: (m, 0)),
                   pl.BlockSpec((tm * (D // 2 // LANES), LANES), lambda m: (m, 0)),
                   pl.BlockSpec((tm, R), lambda m: (m, 0))],
        scratch_shapes=[pltpu.VMEM((D, D), BF16), pltpu.VMEM((D, 2 * R), BF16)],
        compiler_params=_cparams(1),
        name="out_proj",
    )(merged, x, w_out, g_all, wr_all)


def _route_kernel(lg_ref, bias_ref, rt_ref, meta_ref, cnt, off, *, tm, tile):
    p = pl.program_id(0)
    m = pl.program_id(1)
    lane = lax.broadcasted_iota(I32, (tm, LANES), 1).astype(F32)
    neg = jnp.float32(-jnp.inf)
    big = jnp.float32(1e9)

    @pl.when((p == 0) & (m == 0))
    def _():
        cnt[...] = jnp.zeros_like(cnt)

    lg = lg_ref[...] + bias_ref[...]
    is_g = lane < N_GROUPS
    gl = jnp.where(is_g, lg, neg)
    gmax = jnp.max(gl, axis=-1, keepdims=True)
    gidx = jnp.min(jnp.where(gl == gmax, lane, big), axis=-1, keepdims=True)
    gsum = jnp.sum(jnp.where(is_g, jnp.exp(gl - gmax), 0.0), axis=-1, keepdims=True)
    g_w = 1.0 / gsum
    lo = N_GROUPS + EXP_PER_GROUP * gidx
    in_grp = (lane >= lo) & (lane < lo + EXP_PER_GROUP)
    el = jnp.where(in_grp, lg, neg)
    v1 = jnp.max(el, axis=-1, keepdims=True)
    i1 = jnp.min(jnp.where(el == v1, lane, big), axis=-1, keepdims=True)
    el2 = jnp.where(lane == i1, neg, el)
    v2 = jnp.max(el2, axis=-1, keepdims=True)
    i2 = jnp.min(jnp.where(el2 == v2, lane, big), axis=-1, keepdims=True)
    e2x = jnp.exp(v2 - v1)
    w1 = g_w / (1.0 + e2x)
    w2 = g_w * e2x / (1.0 + e2x)
    e1 = i1 - N_GROUPS
    e2 = i2 - N_GROUPS
    a1 = jnp.where(lane == e1, 1.0, 0.0)
    a2 = jnp.where(lane == e2, 1.0, 0.0)
    a = a1 + a2

    @pl.when(p == 0)
    def _():
        cnt[0:1, :] = cnt[0:1, :] + jnp.sum(a, axis=0, keepdims=True)

    @pl.when((p == 1) & (m == 0))
    def _():
        counts = cnt[0:1, :]
        tiles = jnp.floor((counts + (tile - 1)) * (1.0 / tile))
        tiles8 = jnp.broadcast_to(tiles, (SUBLANES, LANES)).astype(BF16)
        ri = lax.broadcasted_iota(I32, (LANES, LANES), 0)
        ci = lax.broadcasted_iota(I32, (LANES, LANES), 1)
        upper = jnp.where(ri < ci, 1.0, 0.0).astype(BF16)
        toff = _dot(tiles8, upper)[0:1, :]
        off[0:1, :] = toff * tile
        tend = toff + tiles
        lane1 = lax.broadcasted_iota(I32, (1, LANES), 1).astype(F32)
        n_act = jnp.sum(jnp.where(lane1 == N_EXPERTS - 1, tend, 0.0), axis=-1, keepdims=True)
        texp = jnp.zeros((1, LANES), F32)
        for e in range(N_EXPERTS - 1):
            end_e = jnp.sum(jnp.where(lane1 == e, tend, 0.0), axis=-1, keepdims=True)
            texp = texp + jnp.where(jnp.minimum(lane1, n_act - 1.0) >= end_e, 1.0, 0.0)
        meta_ref[0:1, :] = counts
        meta_ref[1:2, :] = off[0:1, :]
        meta_ref[2:3, :] = texp
        meta_ref[3:4, :] = jnp.broadcast_to(n_act, (1, LANES))
        meta_ref[4:5, :] = tend
        meta_ref[5:8, :] = jnp.zeros((3, LANES), F32)
        cnt[...] = jnp.zeros_like(cnt)

    @pl.when(p == 1)
    def _():
        ri = lax.broadcasted_iota(I32, (tm, tm), 0)
        ci = lax.broadcasted_iota(I32, (tm, tm), 1)
        ltri = jnp.where(ci < ri, 1.0, 0.0).astype(BF16)
        cum = _dot(ltri, a.astype(BF16)) + cnt[0:1, :] + off[0:1, :]
        pos1 = jnp.sum(a1 * cum, axis=-1, keepdims=True)
        pos2 = jnp.sum(a2 * cum, axis=-1, keepdims=True)
        cnt[0:1, :] = cnt[0:1, :] + jnp.sum(a, axis=0, keepdims=True)
        rt = jnp.where(lane == 0, e1, 0.0)
        rt = jnp.where(lane == 1, e2, rt)
        rt = jnp.where(lane == 2, w1, rt)
        rt = jnp.where(lane == 3, w2, rt)
        rt = jnp.where(lane == 4, pos1, rt)
        rt = jnp.where(lane == 5, pos2, rt)
        rt_ref[...] = rt


def _route(logits, bias_all, l, tile):
    T, R = logits.shape
    tm = _pick(T, (640, 512, 384, 256, 128))
    return pl.pallas_call(
        functools.partial(_route_kernel, tm=tm, tile=tile),
        out_shape=[jax.ShapeDtypeStruct((T, R), F32), jax.ShapeDtypeStruct((SUBLANES, LANES), F32)],
        grid=(2, T // tm),
        in_specs=[pl.BlockSpec((tm, R), lambda p, m: (m, 0)), _layer_spec(bias_all, l)],
        out_specs=[pl.BlockSpec((tm, R), lambda p, m: (m * p, 0)),
                   pl.BlockSpec((SUBLANES, LANES), lambda p, m: (0, 0))],
        scratch_shapes=[pltpu.VMEM((SUBLANES, LANES), F32), pltpu.VMEM((SUBLANES, LANES), F32)],
        compiler_params=_cparams(2),
        name="route",
    )(logits, bias_all)


def _dispatch_kernel(pos_ref, tend_ref, hp_ref, hs_ref, stage, zbuf, sem, zsem, *, T, tk, tile, n_tiles, rpt):
    i = pl.program_id(0)
    n = pl.num_programs(0)
    trows = tile * rpt

    @pl.when(i == 0)
    def _():
        zbuf[...] = jnp.zeros_like(zbuf)

        def last_tile_copy(ex):
            end = tend_ref[ex]
            start = tend_ref[ex - 1] if ex > 0 else 0
            row0 = pl.multiple_of((end - 1) * trows, trows)
            return end > start, pltpu.make_async_copy(zbuf, hs_ref.at[pl.ds(row0, trows)], zsem)

        def spare_tile_copy(k):
            idx = tend_ref[N_EXPERTS - 1] + k
            row0 = pl.multiple_of(jnp.minimum(idx, n_tiles - 1) * trows, trows)
            return idx < n_tiles, pltpu.make_async_copy(zbuf, hs_ref.at[pl.ds(row0, trows)], zsem)

        fills = [last_tile_copy(ex) for ex in range(N_EXPERTS)] + [spare_tile_copy(k) for k in range(N_EXPERTS)]
        for go, cp in fills:
            @pl.when(go)
            def _(cp=cp):
                cp.start()

        for go, cp in fills:
            @pl.when(go)
            def _(cp=cp):
                cp.wait()

    def copies(step, slot):
        out = []
        for u in range(tk):
            t = step * tk + u
            for k in range(2):
                p = pl.multiple_of(pos_ref[k * T + t] * rpt, rpt)
                out.append(pltpu.make_async_copy(stage.at[slot, pl.ds(u * rpt, rpt)], hs_ref.at[pl.ds(p, rpt)],
                                                 sem.at[slot]))
        return out

    def wait_slot(s):
        rows = hs_ref.at[pl.ds(0, 2 * tk * rpt)]
        pltpu.make_async_copy(rows, rows, sem.at[s]).wait()

    slot = lax.rem(i, 2)
    for s in range(2):
        @pl.when(slot == s)
        def _(s=s):
            stage[s] = hp_ref[...]
            for idx, cp in enumerate(copies(i, s)):
                cp.start(priority=idx % 2)

        @pl.when((slot == 1 - s) & (i > 0))
        def _(s=s):
            wait_slot(s)

        @pl.when((slot == s) & (i == n - 1))
        def _(s=s):
            wait_slot(s)


def _dispatch(pos_flat, tend, hp, n_pad, tile, rpt):
    T = hp.shape[0] // rpt
    tk = _pick(T, (320, 128))
    return pl.pallas_call(
        functools.partial(_dispatch_kernel, T=T, tk=tk, tile=tile, n_tiles=n_pad // tile, rpt=rpt),
        out_shape=jax.ShapeDtypeStruct((n_pad * rpt, LANES), U32),
        grid_spec=pltpu.PrefetchScalarGridSpec(
            num_scalar_prefetch=2,
            grid=(T // tk,),
            in_specs=[pl.BlockSpec((tk * rpt, LANES), lambda i, pos, tend: (i, 0))],
            out_specs=pl.BlockSpec(memory_space=pl.ANY),
            scratch_shapes=[pltpu.VMEM((2, tk * rpt, LANES), U32), pltpu.VMEM((tile * rpt, LANES), U32),
                            pltpu.SemaphoreType.DMA((2,)), pltpu.SemaphoreType.DMA],
        ),
        compiler_params=_cparams(1),
        name="dispatch",
    )(pos_flat, tend, hp)


def _expert_kernel(te_ref, na_ref, hs_ref, wg_ref, wu_ref, wd_ref, ys_ref,
                   wgst, wust, wdst, wgbf, wubf, wdbf, sem, *, l, tile):
    i = pl.program_id(0)
    na = na_ref[0]
    e = te_ref[i]
    first = (i < na) & ((i == 0) | (e != te_ref[jnp.maximum(i - 1, 0)]))

    def weight_copies(ex):
        return (pltpu.make_async_copy(wg_ref.at[l, ex], wgst, sem),
                pltpu.make_async_copy(wu_ref.at[l, ex], wust, sem),
                pltpu.make_async_copy(wd_ref.at[l, ex], wdst, sem))

    @pl.when(i == 0)
    def _():
        for cp in weight_copies(e):
            cp.start()

    @pl.when(first)
    def _():
        for cp in weight_copies(e):
            cp.wait()
        _cast_rows(wgst, wgbf, 256)
        _cast_rows(wust, wubf, 256)
        _cast_rows(wdst, wdbf, 256)
        j = lax.while_loop(lambda j: (j < na) & (te_ref[jnp.minimum(j, na - 1)] == e), lambda j: j + 1, i + 1)

        @pl.when(j < na)
        def _():
            for cp in weight_copies(te_ref[jnp.minimum(j, na - 1)]):
                cp.start()

    @pl.when(i < na)
    def _():
        rin = hs_ref.shape[0] // tile
        w = jnp.concatenate([hs_ref[pl.ds(j, tile, stride=rin), :] for j in range(rin)], axis=1)
        half = w.shape[1]
        lo = lax.bitcast_convert_type(w << 16, F32).astype(BF16)
        hi = lax.bitcast_convert_type(w & jnp.uint32(0xFFFF0000), F32).astype(BF16)
        a = _dot(lo, wgbf[0:half, :]) + _dot(hi, wgbf[half:, :])
        b = _dot(lo, wubf[0:half, :]) + _dot(hi, wubf[half:, :])
        hid = (a * _sigmoid(a) * b).astype(BF16)
        ys_ref[...] = _dot(hid, wdbf[...])

    @pl.when(i >= na)
    def _():
        ys_ref[...] = jnp.zeros_like(ys_ref)


def _experts(te, na, hs, wg, wu, wd, l, tile, n_tiles_max):
    D, F = wg.shape[-2:]
    rin = D // 2 // LANES
    n_pad = hs.shape[0] // rin

    def row_map(i, te_ref, na_ref):
        return (jnp.minimum(i, na_ref[0] - 1), 0)

    any_spec = pl.BlockSpec(memory_space=pl.ANY)
    return pl.pallas_call(
        functools.partial(_expert_kernel, l=l, tile=tile),
        out_shape=jax.ShapeDtypeStruct((n_pad, D), F32),
        grid_spec=pltpu.PrefetchScalarGridSpec(
            num_scalar_prefetch=2,
            grid=(n_tiles_max,),
            in_specs=[pl.BlockSpec((tile * rin, LANES), row_map), any_spec, any_spec, any_spec],
            out_specs=pl.BlockSpec((tile, D), lambda i, te_ref, na_ref: (i, 0)),
            scratch_shapes=[pltpu.VMEM((D, F), F32), pltpu.VMEM((D, F), F32), pltpu.VMEM((F, D), F32),
                            pltpu.VMEM((D, F), BF16), pltpu.VMEM((D, F), BF16), pltpu.VMEM((F, D), BF16),
                            pltpu.SemaphoreType.DMA],
        ),
        compiler_params=_cparams(1),
        name="experts",
    )(te, na, hs, wg, wu, wd)


def _combine_kernel(pos_ref, x1_ref, rt_ref, g_ref, ys_ref, *rest, T, tk, n_prompt):
    final = n_prompt is not None
    out_a, out_b, buf, sem = rest
    i = pl.program_id(0)
    n = pl.num_programs(0)

    def copies(step, slot):
        out = []
        for u in range(tk):
            t = step * tk + u
            for k in range(2):
                p = pos_ref[k * T + t]
                out.append(pltpu.make_async_copy(ys_ref.at[pl.ds(p, 1)], buf.at[slot, k, pl.ds(u, 1)],
                                                 sem.at[slot]))
        return out

    slot = lax.rem(i, 2)

    @pl.when(i == 0)
    def _():
        for idx, cp in enumerate(copies(0, 0)):
            cp.start(priority=idx % 2)

    for s in range(2):
        @pl.when((i + 1 < n) & (slot == 1 - s))
        def _(s=s):
            for idx, cp in enumerate(copies(i + 1, s)):
                cp.start(priority=idx % 2)

    for s in range(2):
        @pl.when(slot == s)
        def _(s=s):
            pltpu.make_async_copy(buf.at[s], buf.at[s], sem.at[s]).wait()
            rt = rt_ref[...]
            w1 = rt[:, 2:3]
            w2 = rt[:, 3:4]
            x2 = x1_ref[...] + w1 * buf[s, 0] + w2 * buf[s, 1]
            hn = _rms_scale(x2, g_ref[...])
            if not final:
                out_a[...] = x2
                out_b[...] = hn.astype(out_b.dtype)
            else:
                @pl.when(i < n_prompt)
                def _():
                    out_a[...] = hn

                @pl.when(i >= n_prompt)
                def _():
                    out_b[...] = hn


def _combine(pos_flat, x1, route, g_all, gl, ys, n_sample=None):
    T, D = x1.shape
    tk = _pick(T, (320, 128)) if n_sample is None else n_sample
    R = route.shape[1]
    if n_sample is None:
        n_prompt = None
        g_spec = pl.BlockSpec((None, 1, D), lambda i, pos: (gl, 0, 0))
        out_shape = [jax.ShapeDtypeStruct((T, D), F32), jax.ShapeDtypeStruct((T, D), BF16)]
        out_specs = [pl.BlockSpec((tk, D), lambda i, pos: (i, 0)), pl.BlockSpec((tk, D), lambda i, pos: (i, 0))]
    else:
        assert n_sample == tk
        n_prompt = (T - n_sample) // tk
        g_spec = pl.BlockSpec((1, D), lambda i, pos: (0, 0))
        out_shape = [jax.ShapeDtypeStruct((T - n_sample, D), F32), jax.ShapeDtypeStruct((n_sample, D), F32)]
        out_specs = [pl.BlockSpec((tk, D), lambda i, pos: (jnp.minimum(i, n_prompt - 1), 0)),
                     pl.BlockSpec((tk, D), lambda i, pos: (0, 0))]
    return pl.pallas_call(
        functools.partial(_combine_kernel, T=T, tk=tk, n_prompt=n_prompt),
        out_shape=out_shape,
        grid_spec=pltpu.PrefetchScalarGridSpec(
            num_scalar_prefetch=1,
            grid=(T // tk,),
            in_specs=[pl.BlockSpec((tk, D), lambda i, pos: (i, 0)),
                      pl.BlockSpec((tk, R), lambda i, pos: (i, 0)),
                      g_spec,
                      pl.BlockSpec(memory_space=pl.ANY)],
            out_specs=out_specs,
            scratch_shapes=[pltpu.VMEM((2, 2, tk, D), F32), pltpu.SemaphoreType.DMA((2,))],
        ),
        compiler_params=_cparams(1),
        name="combine",
    )(pos_flat, x1, route, g_all, ys)


def kernel(x_prompt, x_sample, state_conv, state_pool, state_ssm_re, state_ssm_im,
           norm_mix_g, norm_ffn_g, w_in,
           sgu_ln_g, sgu_ln_b, sgu_w, sgu_b,
           conv_w, conv_b, conv_ln_g, conv_ln_b,
           ssm_a_re, ssm_a_im, ssm_log_dt, ssm_b_re, ssm_b_im, ssm_c_re, ssm_c_im, ssm_d, ssm_w_glu, ssm_b_glu,
           pool_w, pool_scale,
           w_branch, w_out,
           router_group_w, router_group_b, router_expert_w, router_expert_b,
           expert_w_gate, expert_w_up, expert_w_down,
           final_norm_g):
    B, L, D = x_prompt.shape
    DB = x_sample.shape[0]
    depth = w_in.shape[0]
    W = sgu_ln_g.shape[-1]
    G, P = ssm_a_re.shape[1:]
    GP = G * P
    T = B * L + DB
    gate_off = 6 * W
    assert x_sample.shape[1] == 1 and L % TIME_TILE == 0 and (B * L) % DB == 0
    assert all(w & (w - 1) == 0 and w <= 16 for w in POOL_WINDOWS)

    row = lambda a: a.reshape(depth, 1, a.shape[-1])
    small = {
        "sgu_ln_g": row(sgu_ln_g), "sgu_ln_b": row(sgu_ln_b),
        "sgu_bT": jnp.swapaxes(sgu_b, 1, 2),
        "sgu_w00": row(jnp.repeat(sgu_w[:, :, 0, 0], W // SGU_GROUPS, axis=-1)),
        "sgu_b0": row(jnp.repeat(sgu_b[:, :, 0], W // SGU_GROUPS, axis=-1)),
        "conv_w": conv_w, "conv_w8": jnp.repeat(conv_w, SUBLANES, axis=1), "conv_b": row(conv_b), "conv_ln_g": row(conv_ln_g), "conv_ln_b": row(conv_ln_b),
        "ssm_d": row(ssm_d), "ssm_b_glu": row(ssm_b_glu), "pool_scale": row(pool_scale),
    }
    S = TIME_TILE // SUBLANES
    lamr, lami, pwr, pwi, bbr, bbi, cr_bd, ci_bd, sguw_bf, wglu_bf, poolw_bf = _prep(
        ssm_a_re, ssm_a_im, ssm_log_dt, ssm_b_re, ssm_b_im, ssm_c_re, ssm_c_im, sgu_w, ssm_w_glu, pool_w, S)
    prep = {"lamr": lamr, "lami": lami, "pwr": pwr, "pwi": pwi, "bbr": bbr, "bbi": bbi,
            "sguw": sguw_bf, "wglu": wglu_bf, "poolw": poolw_bf}

    wr = jnp.concatenate([router_group_w,
                          jnp.transpose(router_expert_w, (0, 2, 1, 3)).reshape(depth, D, N_EXPERTS)], axis=-1)
    wr = jnp.pad(wr, ((0, 0), (0, 0), (0, LANES - wr.shape[-1])))
    rb = jnp.concatenate([router_group_b, router_expert_b.reshape(depth, N_EXPERTS)], axis=-1)
    rb = jnp.pad(rb, ((0, 0), (0, LANES - rb.shape[-1]))).reshape(depth, 1, LANES)

    norm_mix3 = row(norm_mix_g)
    norm_ffn3 = row(norm_ffn_g)
    final3 = final_norm_g.reshape(1, D)

    n_tiles_max = -(-2 * T // EXPERT_TILE) + N_EXPERTS
    n_pad = n_tiles_max * EXPERT_TILE

    x = jnp.concatenate([x_prompt.reshape(B * L, D), x_sample.reshape(DB, D)], axis=0)
    h = _rmsnorm(x, norm_mix3, 0, BF16)

    stc2 = state_conv.reshape(depth, DB, (CONV_K - 1) * W)
    stp2 = state_pool.reshape(depth, DB, POOL_HIST * W)
    hre2 = state_ssm_re.reshape(depth, DB, GP)
    him2 = state_ssm_im.reshape(depth, DB, GP)

    conv_p, conv_s, pool_p, pool_s = [], [], [], []
    sre_p, sim_p, sre_s, sim_s, v_s = [], [], [], [], []
    y_p = y_s = None
    for l in range(depth):
        o, cp, pp, rp, ip = _mixer_prompt(h, w_in, B, L, T, l, small, prep, cr_bd, ci_bd)
        o, g_s, vs, d_in_s, xs_re, xs_im = _mixer_sample(h, w_in, o, stc2, stp2, hre2, him2, DB, T, l,
                                                         small, prep, cr_bd, ci_bd)
        merged = _merge(h, o, w_in, w_branch, l, gate_off)
        x1, hp, logits = _out_proj(merged, x, w_out, norm_ffn3, wr, l)
        route, meta = _route(logits, rb, l, EXPERT_TILE)
        pos_flat = jnp.transpose(route[:, 4:6]).astype(I32).reshape(2 * T)
        te = meta[2, :n_tiles_max].astype(I32)
        na = meta[3, :1].astype(I32)
        tend = meta[4, :N_EXPERTS].astype(I32)
        hs = _dispatch(pos_flat, tend, hp, n_pad, EXPERT_TILE, D // 2 // LANES)
        ys = _experts(te, na, hs, expert_w_gate, expert_w_up, expert_w_down, l, EXPERT_TILE, n_tiles_max)
        if l + 1 < depth:
            x, h = _combine(pos_flat, x1, route, norm_mix3, l + 1, ys)
        else:
            y_p, y_s = _combine(pos_flat, x1, route, final3, 0, ys, n_sample=DB)

        conv_p.append(cp)
        pool_p.append(pp)
        sre_p.append(rp.reshape(B, G, P))
        sim_p.append(ip.reshape(B, G, P))
        conv_s.append(jnp.concatenate([state_conv[l][:, 1:], g_s[:, None, :]], axis=1))
        pool_s.append(jnp.concatenate([state_pool[l][:, 1:], d_in_s[:, None, :]], axis=1))
        sre_s.append(xs_re.reshape(DB, G, P))
        sim_s.append(xs_im.reshape(DB, G, P))
        v_s.append(vs[:, None, :])

    y_prompt = y_p.reshape(B, L, D)
    y_sample = y_s.reshape(DB, 1, D)
    return (y_prompt, y_sample, jnp.stack(conv_p), jnp.stack(conv_s), jnp.stack(pool_p), jnp.stack(pool_s),
            jnp.stack(sre_p), jnp.stack(sim_p), jnp.stack(sre_s), jnp.stack(sim_s), jnp.stack(v_s))
```
